```python
import math
import jax, jax.numpy as jnp
from jax import lax
import numpy as np

D_MODEL = 1024
BATCH = 4
SEQ = 4096
DEPTH = 2
DEC_BATCH = 128
DEC_SEQ = 8
PAST_LEN = 2048
PAGE_SIZE = 128

HEAD_DIM = 64
CONV_CH = D_MODEL // 2
CONV_W = 31
SB_HEADS = (D_MODEL // 2) // HEAD_DIM
SB_DIM = SB_HEADS * HEAD_DIM
NSA_HEADS = D_MODEL // HEAD_DIM
NSA_GQA = 4
NSA_KVH = NSA_HEADS // NSA_GQA
NSA_KV_DIM = NSA_KVH * HEAD_DIM
CMP_BLK = 64
CMP_HID = 256
SEL_BLK = CMP_BLK
SEL_TOPN = 16
FORCE_SCORE = 1.0e4
WINDOW = 512
MEM_LEN = 256
MEM_HEADS = 4
MEM_HD = D_MODEL // MEM_HEADS
D_FF = -(-8 * D_MODEL // (3 * 256)) * 256
ROPE_THETA = 10000.0
NORM_EPS = 1e-6
Q_BLK = 128
SEL_Q_ROWS = 128
NEG_INF = -1e30
N_EVEN = (DEPTH + 1) // 2
N_ODD = DEPTH // 2
IN_EVEN = 2 * CONV_CH + 3 * SB_DIM
IN_ODD = NSA_HEADS * HEAD_DIM + 6 * NSA_KV_DIM + 3 * NSA_HEADS

kernel_name = 'hybrid_conv_stickbreak_nsa_memory_step'


def rmsnorm(x, g):
    xf = x.astype(jnp.float32)
    y = xf * lax.rsqrt(jnp.mean(xf * xf, axis=-1, keepdims=True) + NORM_EPS)
    return (y * g.astype(jnp.float32)).astype(x.dtype)


def layernorm(x, g, b):
    xf = x.astype(jnp.float32)
    mu = jnp.mean(xf, axis=-1, keepdims=True)
    var = jnp.mean(jnp.square(xf - mu), axis=-1, keepdims=True)
    y = (xf - mu) * lax.rsqrt(var + NORM_EPS)
    return (y * g.astype(jnp.float32) + b.astype(jnp.float32)).astype(x.dtype)


def rope(x, pos):
    half = HEAD_DIM // 2
    inv = ROPE_THETA ** (-jnp.arange(half, dtype=jnp.float32) / half)
    ang = pos.astype(jnp.float32)[:, None] * inv[None, :]
    cos = jnp.cos(ang)[:, None, :]
    sin = jnp.sin(ang)[:, None, :]
    xf = x.astype(jnp.float32)
    x1, x2 = xf[..., :half], xf[..., half:]
    return jnp.concatenate([x1 * cos - x2 * sin, x2 * cos + x1 * sin], axis=-1).astype(x.dtype)


def masked_softmax(s, mask):
    s = jnp.where(mask, s, NEG_INF)
    m = jnp.max(s, axis=-1, keepdims=True)
    p = jnp.where(mask, jnp.exp(s - m), 0.0)
    return p / jnp.maximum(jnp.sum(p, axis=-1, keepdims=True), 1e-30)


def sweep_queries(fn, q, q_pos):
    B, T = q.shape[:2]
    if T <= Q_BLK or T % Q_BLK:
        return fn(q, q_pos)
    nb = T // Q_BLK
    qb = jnp.moveaxis(q.reshape(B, nb, Q_BLK, *q.shape[2:]), 1, 0)
    out = lax.map(lambda a: fn(a[0], a[1]), (qb, q_pos.reshape(nb, Q_BLK)))
    return jnp.moveaxis(out, 0, 1).reshape(B, T, *out.shape[3:])


def gather_pages(pool, page_table):
    g = pool[page_table]
    return g.reshape(g.shape[0], g.shape[1] * g.shape[2], *pool.shape[2:])


def conformer_conv(u, buf, w_dw, b_dw, ln_g, ln_b):
    glu = u[..., :CONV_CH] * jax.nn.sigmoid(u[..., CONV_CH:])
    hp = jnp.concatenate([buf.astype(glu.dtype), glu], axis=1)
    y = lax.conv_general_dilated(hp, w_dw[:, None, :].astype(hp.dtype), (1,), 'VALID',
                                 dimension_numbers=('NWC', 'WIO', 'NWC'),
                                 feature_group_count=CONV_CH) + b_dw
    y = layernorm(y, ln_g, ln_b)
    return jax.nn.silu(y), hp[:, -(CONV_W - 1):]


def stick_breaking(q, k, v, q_pos, k_pos):
    z = jnp.einsum('bqhd,bkhd->bhqk', q, k).astype(jnp.float32) * (HEAD_DIM ** -0.5)
    mask = k_pos[None, :] < q_pos[:, None]
    log_stay = jnp.where(mask, jax.nn.log_sigmoid(-z), 0.0)
    log_after = lax.cumsum(log_stay, axis=3, reverse=True) - log_stay
    w = jnp.where(mask, jnp.exp(jax.nn.log_sigmoid(z) + log_after), 0.0)
    return jnp.einsum('bhqk,bkhd->bqhd', w.astype(v.dtype), v)


def even_mixer(a, w_in, w_out, conv_buf, conv_w, conv_b, ln_g, ln_b, past_k, past_v):
    B, T, _ = a.shape
    P = past_k.shape[1]
    u = a @ w_in
    conv_in = u[..., :2 * CONV_CH]
    q, k, v = jnp.split(u[..., 2 * CONV_CH:], 3, axis=-1)
    q = q.reshape(B, T, SB_HEADS, HEAD_DIM)
    k = k.reshape(B, T, SB_HEADS, HEAD_DIM)
    v = v.reshape(B, T, SB_HEADS, HEAD_DIM)
    y_conv, new_buf = conformer_conv(conv_in, conv_buf, conv_w, conv_b, ln_g, ln_b)
    k_all = jnp.concatenate([past_k.astype(k.dtype), k], axis=1)
    v_all = jnp.concatenate([past_v.astype(v.dtype), v], axis=1)
    k_pos = jnp.arange(P + T, dtype=jnp.int32)
    q_pos = P + jnp.arange(T, dtype=jnp.int32)
    o = sweep_queries(lambda qb, pb: stick_breaking(qb, k_all, v_all, pb, k_pos), q, q_pos)
    y = jnp.concatenate([y_conv, o.reshape(B, T, SB_DIM)], axis=-1) @ w_out
    return y, new_buf, k, v


def compress_blocks(kv, pe, w1, w2):
    B, Tk, G, d = kv.shape
    nb = -(-Tk // CMP_BLK)
    kv = jnp.pad(kv, ((0, 0), (0, nb * CMP_BLK - Tk), (0, 0), (0, 0)))
    blk = kv.reshape(B, nb, CMP_BLK, G, d) + pe[None, None, :, None, :]
    hid = jax.nn.gelu(jnp.einsum('bnlgd,ldh->bngh', blk, w1))
    return jnp.einsum('bngh,he->bnge', hid, w2)


def selected_branch(q, kb, vb, sel_idx, q_pos):
    B, T = q.shape[:2]
    G = kb.shape[1]
    n = sel_idx.shape[-1]
    qc = math.gcd(T, max(1, SEL_Q_ROWS // B))
    nc = T // qc
    bi = jnp.arange(B)[:, None, None, None]
    gi = jnp.arange(G)[None, :, None, None]
    offs = jnp.arange(SEL_BLK, dtype=jnp.int32)

    def one_chunk(args):
        qq, idx, qp = args
        kg = kb[bi, gi, idx].reshape(B, G, qc, n * SEL_BLK, HEAD_DIM)
        vg = vb[bi, gi, idx].reshape(B, G, qc, n * SEL_BLK, HEAD_DIM)
        kpos = (idx[..., None] * SEL_BLK + offs).reshape(B, G, qc, n * SEL_BLK)
        s = jnp.einsum('bqgrd,bgqkd->bgrqk', qq, kg).astype(jnp.float32) * (HEAD_DIM ** -0.5)
        mask = (kpos <= qp[None, None, :, None])[:, :, None]
        p = masked_softmax(s, mask)
        return jnp.einsum('bgrqk,bgqkd->bqgrd', p.astype(vg.dtype), vg)

    qs = jnp.moveaxis(q.reshape(B, nc, qc, *q.shape[2:]), 1, 0)
    ids = jnp.moveaxis(sel_idx.reshape(B, G, nc, qc, n), 2, 0)
    out = lax.map(one_chunk, (qs, ids, q_pos.reshape(nc, qc)))
    return jnp.moveaxis(out, 0, 1).reshape(q.shape)


def window_attn(q, k, v, q_pos, k_pos):
    s = jnp.einsum('bqgrd,bkgd->bgrqk', q, k).astype(jnp.float32) * (HEAD_DIM ** -0.5)
    diff = q_pos[:, None] - k_pos[None, :]
    mask = (diff >= 0) & (diff < WINDOW) & (k_pos[None, :] >= 0)
    p = masked_softmax(s, mask)
    return jnp.einsum('bgrqk,bkgd->bqgrd', p.astype(v.dtype), v)


def window_branch(q, k, v, q_pos, k_pos):
    B, T = q.shape[:2]
    Tk = k.shape[1]
    if T <= Q_BLK or T % Q_BLK:
        return window_attn(q, k, v, q_pos, k_pos)
    nb = T // Q_BLK
    pad = ((0, 0), (WINDOW, 0), (0, 0), (0, 0))
    kp = jnp.pad(k, pad)
    vp = jnp.pad(v, pad)
    kpos = jnp.concatenate([jnp.full((WINDOW,), -1, jnp.int32), k_pos])
    band = WINDOW + Q_BLK
    base = Tk - T

    def one_block(args):
        j, qq, qp = args
        start = j * Q_BLK + base
        kk = lax.dynamic_slice_in_dim(kp, start, band, axis=1)
        vv = lax.dynamic_slice_in_dim(vp, start, band, axis=1)
        pp = lax.dynamic_slice_in_dim(kpos, start, band, axis=0)
        return window_attn(qq, kk, vv, qp, pp)

    qs = jnp.moveaxis(q.reshape(B, nb, Q_BLK, *q.shape[2:]), 1, 0)
    out = lax.map(one_block, (jnp.arange(nb, dtype=jnp.int32), qs, q_pos.reshape(nb, Q_BLK)))
    return jnp.moveaxis(out, 0, 1).reshape(q.shape)


def odd_mixer(a, w_in, w_out, pe_k, w1_k, w2_k, pe_v, w1_v, w2_v,
              past_ck, past_cv, past_sk, past_sv, win_k, win_v, win_keep):
    B, T, _ = a.shape
    P = past_ck.shape[1]
    Wb = win_k.shape[1]
    u = a @ w_in
    d_q = NSA_HEADS * HEAD_DIM
    kvs = jnp.split(u[..., d_q:d_q + 6 * NSA_KV_DIM], 6, axis=-1)
    ck, cv, sk, sv, wk, wv = [t.reshape(B, T, NSA_KVH, HEAD_DIM) for t in kvs]
    gates = jax.nn.sigmoid(u[..., d_q + 6 * NSA_KV_DIM:].astype(jnp.float32))
    gates = gates.reshape(B, T, NSA_KVH, NSA_GQA, 3)
    q_pos = P + jnp.arange(T, dtype=jnp.int32)
    q = rope(u[..., :d_q].reshape(B, T, NSA_HEADS, HEAD_DIM), q_pos)
    q = q.reshape(B, T, NSA_KVH, NSA_GQA, HEAD_DIM)
    sk = rope(sk, q_pos)
    wk = rope(wk, q_pos)
    Tk = P + T
    ck_all = jnp.concatenate([past_ck.astype(ck.dtype), ck], axis=1)
    cv_all = jnp.concatenate([past_cv.astype(cv.dtype), cv], axis=1)
    kc = compress_blocks(ck_all, pe_k, w1_k, w2_k)
    vc = compress_blocks(cv_all, pe_v, w1_v, w2_v)
    nb = kc.shape[1]
    blk_end = jnp.arange(nb, dtype=jnp.int32) * CMP_BLK + (CMP_BLK - 1)
    kc = rope(kc, blk_end)
    s = jnp.einsum('btgrd,bngd->bgrtn', q, kc).astype(jnp.float32) * (HEAD_DIM ** -0.5)
    p_cmp = masked_softmax(s, blk_end[None, :] <= q_pos[:, None])
    o_cmp = jnp.einsum('bgrtn,bngd->btgrd', p_cmp.astype(vc.dtype), vc)
    blk = jnp.arange(nb, dtype=jnp.int32)[None, :]
    cur = (q_pos // SEL_BLK)[:, None]
    forced = (blk == 0) | (blk == cur) | (blk == cur - 1)
    score = jnp.where(blk > cur, -1.0, jnp.where(forced, FORCE_SCORE, jnp.sum(p_cmp, axis=2)))
    _, sel_idx = lax.top_k(score, min(SEL_TOPN, nb))
    pad = nb * SEL_BLK - Tk

    def to_blocks(t):
        t = jnp.pad(t, ((0, 0), (0, pad), (0, 0), (0, 0)))
        return jnp.transpose(t.reshape(B, nb, SEL_BLK, NSA_KVH, HEAD_DIM), (0, 3, 1, 2, 4))

    sk_all = jnp.concatenate([past_sk.astype(sk.dtype), sk], axis=1)
    sv_all = jnp.concatenate([past_sv.astype(sv.dtype), sv], axis=1)
    o_sel = selected_branch(q, to_blocks(sk_all), to_blocks(sv_all), sel_idx, q_pos)
    wk_all = jnp.concatenate([win_k.astype(wk.dtype), wk], axis=1)
    wv_all = jnp.concatenate([win_v.astype(wv.dtype), wv], axis=1)
    w_pos = (P - Wb) + jnp.arange(Wb + T, dtype=jnp.int32)
    o_win = window_branch(q, wk_all, wv_all, q_pos, w_pos)
    o = gates[..., 0:1] * o_cmp + gates[..., 1:2] * o_sel + gates[..., 2:3] * o_win
    y = o.astype(a.dtype).reshape(B, T, d_q) @ w_out
    return y, ck, cv, sk, sv, wk_all[:, -win_keep:], wv_all[:, -win_keep:]


def memory_attn(a, mk, mv, w_q, w_o):
    B, T, _ = a.shape
    q = (a @ w_q).reshape(B, T, MEM_HEADS, MEM_HD)
    s = jnp.einsum('bthd,bmhd->bhtm', q, mk.astype(q.dtype)).astype(jnp.float32) * (MEM_HD ** -0.5)
    p = jax.nn.softmax(s, axis=-1)
    o = jnp.einsum('bhtm,bmhd->bthd', p.astype(a.dtype), mv.astype(a.dtype)).reshape(B, T, D_MODEL)
    return o @ w_o


def swiglu(a, w_gate, w_up, w_down):
    return (jax.nn.silu(a @ w_gate) * (a @ w_up)) @ w_down


def setup_inputs(seed: int = 0) -> dict:
    key = jax.random.key(seed)
    ks = jax.random.split(key, 40)

    def nrm(i, shape, scale):
        return jax.random.normal(ks[i], shape, jnp.float32) * scale

    n_pages = PAST_LEN // PAGE_SIZE
    n_used = DEC_BATCH * n_pages
    n_pool = -(-5 * n_used // 4)
    win_buf = min(WINDOW, PAST_LEN)
    page_table = jax.random.permutation(ks[14], n_pool)[:n_used].reshape(DEC_BATCH, n_pages).astype(jnp.int32)
    paged_sb = (N_EVEN, n_pool, PAGE_SIZE, SB_HEADS, HEAD_DIM)
    paged_nsa = (N_ODD, n_pool, PAGE_SIZE, NSA_KVH, HEAD_DIM)
    win_shape = (N_ODD, DEC_BATCH, win_buf, NSA_KVH, HEAD_DIM)
    mem_shape = (DEPTH, DEC_BATCH, MEM_LEN, MEM_HEADS, MEM_HD)
    d_in = D_MODEL ** -0.5
    return {
        'x_prompt': nrm(0, (BATCH, SEQ, D_MODEL), 1.0),
        'x_sample': nrm(1, (DEC_BATCH, DEC_SEQ, D_MODEL), 1.0),
        'mem_prompt': nrm(2, (BATCH, MEM_LEN, D_MODEL), 1.0),
        'cache_sb_k': nrm(3, paged_sb, 1.0),
        'cache_sb_v': nrm(4, paged_sb, 1.0),
        'state_conv': nrm(5, (N_EVEN, DEC_BATCH, CONV_W - 1, CONV_CH), 0.5),
        'cache_nsa_cmp_k': nrm(6, paged_nsa, 1.0),
        'cache_nsa_cmp_v': nrm(7, paged_nsa, 1.0),
        'cache_nsa_sel_k': nrm(8, paged_nsa, 1.0),
        'cache_nsa_sel_v': nrm(9, paged_nsa, 1.0),
        'cache_nsa_win_k': nrm(10, win_shape, 1.0),
        'cache_nsa_win_v': nrm(11, win_shape, 1.0),
        'cache_mem_k': nrm(12, mem_shape, 1.0),
        'cache_mem_v': nrm(13, mem_shape, 1.0),
        'page_table': page_table,
        'norm_mix': 1.0 + nrm(15, (DEPTH, D_MODEL), 0.01),
        'norm_mem': 1.0 + nrm(16, (DEPTH, D_MODEL), 0.01),
        'norm_ffn': 1.0 + nrm(17, (DEPTH, D_MODEL), 0.01),
        'final_norm': 1.0 + nrm(18, (D_MODEL,), 0.01),
        'w_in_even': nrm(19, (N_EVEN, D_MODEL, IN_EVEN), d_in),
        'w_in_odd': nrm(20, (N_ODD, D_MODEL, IN_ODD), d_in),
        'w_mix_out': nrm(21, (DEPTH, D_MODEL, D_MODEL), d_in),
        'conv_w': nrm(22, (N_EVEN, CONV_W, CONV_CH), CONV_W ** -0.5),
        'conv_b': nrm(23, (N_EVEN, CONV_CH), 0.01),
        'conv_ln_g': 1.0 + nrm(24, (N_EVEN, CONV_CH), 0.01),
        'conv_ln_b': nrm(25, (N_EVEN, CONV_CH), 0.01),
        'cmp_pe_k': nrm(26, (N_ODD, CMP_BLK, HEAD_DIM), 0.1),
        'cmp_w1_k': nrm(27, (N_ODD, CMP_BLK, HEAD_DIM, CMP_HID), (CMP_BLK * HEAD_DIM) ** -0.5),
        'cmp_w2_k': nrm(28, (N_ODD, CMP_HID, HEAD_DIM), CMP_HID ** -0.5),
        'cmp_pe_v': nrm(29, (N_ODD, CMP_BLK, HEAD_DIM), 0.1),
        'cmp_w1_v': nrm(30, (N_ODD, CMP_BLK, HEAD_DIM, CMP_HID), (CMP_BLK * HEAD_DIM) ** -0.5),
        'cmp_w2_v': nrm(31, (N_ODD, CMP_HID, HEAD_DIM), CMP_HID ** -0.5),
        'w_mem_q': nrm(32, (DEPTH, D_MODEL, D_MODEL), d_in),
        'w_mem_k': nrm(33, (DEPTH, D_MODEL, D_MODEL), d_in),
        'w_mem_v': nrm(34, (DEPTH, D_MODEL, D_MODEL), d_in),
        'w_mem_o': nrm(35, (DEPTH, D_MODEL, D_MODEL), d_in),
        'w_ffn_gate': nrm(36, (DEPTH, D_MODEL, D_FF), d_in),
        'w_ffn_up': nrm(37, (DEPTH, D_MODEL, D_FF), d_in),
        'w_ffn_down': nrm(38, (DEPTH, D_FF, D_MODEL), D_FF ** -0.5),
    }


def reference(x_prompt, x_sample, mem_prompt, cache_sb_k, cache_sb_v, state_conv,
              cache_nsa_cmp_k, cache_nsa_cmp_v, cache_nsa_sel_k, cache_nsa_sel_v,
              cache_nsa_win_k, cache_nsa_win_v, cache_mem_k, cache_mem_v, page_table,
              norm_mix, norm_mem, norm_ffn, final_norm, w_in_even, w_in_odd, w_mix_out,
              conv_w, conv_b, conv_ln_g, conv_ln_b,
              cmp_pe_k, cmp_w1_k, cmp_w2_k, cmp_pe_v, cmp_w1_v, cmp_w2_v,
              w_mem_q, w_mem_k, w_mem_v, w_mem_o, w_ffn_gate, w_ffn_up, w_ffn_down):
    Bp, T = x_prompt.shape[:2]
    dt = x_prompt.dtype

    def trunk(h, conv_bufs, sb_past, nsa_past, win_past, mem_kv, win_keep):
        conv_new, sbk_new, sbv_new = [], [], []
        ck_new, cv_new, sk_new, sv_new, wk_new, wv_new = [], [], [], [], [], []
        for l in range(DEPTH):
            a = rmsnorm(h, norm_mix[l])
            if l % 2 == 0:
                e = l // 2
                y, buf, k, v = even_mixer(a, w_in_even[e], w_mix_out[l], conv_bufs[e],
                                          conv_w[e], conv_b[e], conv_ln_g[e], conv_ln_b[e],
                                          sb_past[e][0], sb_past[e][1])
                conv_new.append(buf)
                sbk_new.append(k)
                sbv_new.append(v)
            else:
                o = l // 2
                y, ck, cv, sk, sv, wk, wv = odd_mixer(
                    a, w_in_odd[o], w_mix_out[l],
                    cmp_pe_k[o], cmp_w1_k[o], cmp_w2_k[o], cmp_pe_v[o], cmp_w1_v[o], cmp_w2_v[o],
                    nsa_past[o][0], nsa_past[o][1], nsa_past[o][2], nsa_past[o][3],
                    win_past[o][0], win_past[o][1], win_keep)
                ck_new.append(ck)
                cv_new.append(cv)
                sk_new.append(sk)
                sv_new.append(sv)
                wk_new.append(wk)
                wv_new.append(wv)
            h = h + y
            h = h + memory_attn(rmsnorm(h, norm_mem[l]), mem_kv[l][0], mem_kv[l][1], w_mem_q[l], w_mem_o[l])
            h = h + swiglu(rmsnorm(h, norm_ffn[l]), w_ffn_gate[l], w_ffn_up[l], w_ffn_down[l])
        return (rmsnorm(h, final_norm), jnp.stack(sbk_new), jnp.stack(sbv_new), jnp.stack(conv_new),
                jnp.stack(ck_new), jnp.stack(cv_new), jnp.stack(sk_new), jnp.stack(sv_new),
                jnp.stack(wk_new), jnp.stack(wv_new))

    mem_k_list = [(mem_prompt @ w_mem_k[l]).reshape(Bp, MEM_LEN, MEM_HEADS, MEM_HD) for l in range(DEPTH)]
    mem_v_list = [(mem_prompt @ w_mem_v[l]).reshape(Bp, MEM_LEN, MEM_HEADS, MEM_HD) for l in range(DEPTH)]
    empty_sb = jnp.zeros((Bp, 0, SB_HEADS, HEAD_DIM), dt)
    empty_kv = jnp.zeros((Bp, 0, NSA_KVH, HEAD_DIM), dt)
    zero_buf = jnp.zeros((Bp, CONV_W - 1, CONV_CH), dt)
    (y_prompt, sb_k_p, sb_v_p, conv_p, cmp_k_p, cmp_v_p, sel_k_p, sel_v_p, win_k_p, win_v_p) = trunk(
        x_prompt,
        [zero_buf] * N_EVEN,
        [(empty_sb, empty_sb)] * N_EVEN,
        [(empty_kv, empty_kv, empty_kv, empty_kv)] * N_ODD,
        [(empty_kv, empty_kv)] * N_ODD,
        [(mem_k_list[l], mem_v_list[l]) for l in range(DEPTH)],
        min(WINDOW, T))

    (y_sample, sb_k_s, sb_v_s, conv_s, cmp_k_s, cmp_v_s, sel_k_s, sel_v_s, win_k_s, win_v_s) = trunk(
        x_sample,
        [state_conv[e] for e in range(N_EVEN)],
        [(gather_pages(cache_sb_k[e], page_table), gather_pages(cache_sb_v[e], page_table))
         for e in range(N_EVEN)],
        [(gather_pages(cache_nsa_cmp_k[o], page_table), gather_pages(cache_nsa_cmp_v[o], page_table),
          gather_pages(cache_nsa_sel_k[o], page_table), gather_pages(cache_nsa_sel_v[o], page_table))
         for o in range(N_ODD)],
        [(cache_nsa_win_k[o], cache_nsa_win_v[o]) for o in range(N_ODD)],
        [(cache_mem_k[l], cache_mem_v[l]) for l in range(DEPTH)],
        cache_nsa_win_k.shape[2])

    mem_k_p = jnp.stack(mem_k_list)
    mem_v_p = jnp.stack(mem_v_list)
    return (y_prompt, y_sample,
            sb_k_p, sb_v_p, conv_p, cmp_k_p, cmp_v_p, sel_k_p, sel_v_p, win_k_p, win_v_p, mem_k_p, mem_v_p,
            sb_k_s, sb_v_s, conv_s, cmp_k_s, cmp_v_s, sel_k_s, sel_v_s, win_k_s, win_v_s)
```

```python
import functools
import math

import jax
import jax.numpy as jnp
from jax import lax
from jax.experimental import pallas as pl
from jax.experimental.pallas import tpu as pltpu

F32 = jnp.float32
BF16 = jnp.bfloat16

D_MODEL = 1024
HEAD_DIM = 64
CONV_CH = 512
CONV_W = 31
SB_HEADS = 8
SB_DIM = 512
NSA_HEADS = 16
NSA_GQA = 4
NSA_KVH = 4
NSA_KV_DIM = 256
CMP_BLK = 64
CMP_HID = 256
SEL_TOPN = 16
FORCE_SCORE = 1.0e4
WINDOW = 512
MEM_HEADS = 4
MEM_HD = 256
D_FF = 2816
ROPE_THETA = 10000.0
NORM_EPS = 1e-6
NEG_INF = -1e30
PAGE = 128

V7X_VMEM_BYTES = 64 * 1024 * 1024
VMEM_LIMIT = V7X_VMEM_BYTES - 8 * 1024 * 1024
ATT_TILE = 256


def _cp(sem):
    return pltpu.CompilerParams(dimension_semantics=sem, vmem_limit_bytes=VMEM_LIMIT)


def _nt(a, b):
    return lax.dot_general(a, b, (((1,), (1,)), ((), ())), preferred_element_type=F32)


def _dot(a, b):
    return jnp.dot(a, b, preferred_element_type=F32)


def _rms(x, g):
    y = x * lax.rsqrt(jnp.mean(x * x, axis=-1, keepdims=True) + NORM_EPS)
    return y * g


def _sigmoid(x):
    return 1.0 / (1.0 + jnp.exp(-x))


def _const_spec(shape):
    n = len(shape)
    return pl.BlockSpec(shape, lambda *a: (0,) * n)


def _row_spec(tm, n):
    return pl.BlockSpec((tm, n), lambda i: (i, 0))


def _hm_spec(heads, tm, width, nt):
    return pl.BlockSpec((1, heads, tm, width), lambda i: (i // nt, 0, i % nt, 0))


def _row_tile(m):
    return 512 if m >= 4096 else min(256, m)


def _rope_tables(pos, width):
    half = HEAD_DIM // 2
    inv = ROPE_THETA ** (-jnp.arange(half, dtype=F32) / half)
    ang = pos.astype(F32)[:, None] * inv[None, :]
    cos = jnp.cos(ang)
    sin = jnp.sin(ang)
    c = jnp.concatenate([cos, cos], axis=-1)
    s = jnp.concatenate([-sin, sin], axis=-1)
    reps = width // HEAD_DIM
    return jnp.tile(c, (1, reps)), jnp.tile(s, (1, reps))


def _rope128(x, c, s):
    lane = lax.broadcasted_iota(jnp.int32, x.shape, 1)
    first = (lane % HEAD_DIM) < (HEAD_DIM // 2)
    rot = jnp.where(first, pltpu.roll(x, 96, 1), pltpu.roll(x, 32, 1))
    return x * c + rot * s


def _rope_wide(x, c, s):
    n = x.shape[1] // 128
    return jnp.concatenate([_rope128(x[:, i * 128:(i + 1) * 128], c, s) for i in range(n)], axis=1)


def _even_in_body(head_major, h_ref, g_ref, w_ref, glu_ref, k32_ref, v32_ref, q_ref, *hm_refs):
    xn = _rms(h_ref[...], g_ref[...]).astype(BF16)

    def mm(lo, hi):
        return _dot(xn, w_ref[:, lo:hi])

    glu_ref[...] = mm(0, CONV_CH) * _sigmoid(mm(CONV_CH, 2 * CONV_CH))
    base = 2 * CONV_CH
    q = mm(base, base + SB_DIM) * (HEAD_DIM ** -0.5)
    k = mm(base + SB_DIM, base + 2 * SB_DIM)
    v = mm(base + 2 * SB_DIM, base + 3 * SB_DIM)
    k32_ref[...] = k
    v32_ref[...] = v
    if head_major:
        kh_ref, vh_ref = hm_refs
        for hh in range(SB_HEADS):
            sl = slice(hh * HEAD_DIM, (hh + 1) * HEAD_DIM)
            q_ref[0, hh] = q[:, sl].astype(BF16)
            kh_ref[0, hh] = k[:, sl].astype(BF16)
            vh_ref[0, hh] = v[:, sl].astype(BF16)
    else:
        q_ref[...] = q.astype(BF16)


def _even_in(h2, g, w, batch, head_major):
    m = h2.shape[0]
    t = m // batch
    tm = _row_tile(m)
    nt = max(t // tm, 1)
    outs = [jax.ShapeDtypeStruct((m, CONV_CH), F32), jax.ShapeDtypeStruct((m, SB_DIM), F32),
            jax.ShapeDtypeStruct((m, SB_DIM), F32)]
    specs = [_row_spec(tm, CONV_CH), _row_spec(tm, SB_DIM), _row_spec(tm, SB_DIM)]
    if head_major:
        hm = jax.ShapeDtypeStruct((batch, SB_HEADS, t, HEAD_DIM), BF16)
        outs += [hm, hm, hm]
        specs += [_hm_spec(SB_HEADS, tm, HEAD_DIM, nt)] * 3
    else:
        outs += [jax.ShapeDtypeStruct((m, SB_DIM), BF16)]
        specs += [_row_spec(tm, SB_DIM)]
    return pl.pallas_call(
        functools.partial(_even_in_body, head_major),
        grid=(m // tm,),
        in_specs=[_row_spec(tm, D_MODEL), _const_spec((1, D_MODEL)), _const_spec(w.shape)],
        out_specs=specs, out_shape=outs,
        compiler_params=_cp(("parallel",)), name="even_in",
    )(h2, g, w)


CONV_PAD = 32


def _conv_body(t, chunk, hp_ref, w_ref, b_ref, lg_ref, lb_ref, o_ref):
    win_rows = chunk + CONV_PAD

    def one_chunk(c, _):
        base = pl.multiple_of(c * chunk, chunk)
        win = hp_ref[0, pl.ds(base, win_rows), :]
        acc = jnp.zeros((chunk, CONV_CH), F32)
        for r in range(8):
            sh = win if r == 0 else pltpu.roll(win, win_rows - r, 0)
            for a in range(4):
                tap = 8 * a + r
                if tap < CONV_W:
                    acc = acc + sh[8 * a:8 * a + chunk] * w_ref[tap:tap + 1, :]
        y = acc + b_ref[...]
        mu = jnp.mean(y, axis=-1, keepdims=True)
        var = jnp.mean(jnp.square(y - mu), axis=-1, keepdims=True)
        y = (y - mu) * lax.rsqrt(var + NORM_EPS) * lg_ref[...] + lb_ref[...]
        o_ref[0, pl.ds(base, chunk), :] = (y * _sigmoid(y)).astype(o_ref.dtype)
        return 0

    lax.fori_loop(0, t // chunk, one_chunk, 0)


def _conformer_conv(glu, buf, w_dw, b_dw, ln_g, ln_b):
    b, t, c = glu.shape
    hp = jnp.concatenate([buf.astype(F32), glu, jnp.zeros((b, CONV_PAD - (CONV_W - 1), c), F32)], axis=1)
    chunk = min(t, 128)
    return pl.pallas_call(
        functools.partial(_conv_body, t, chunk),
        grid=(b,),
        in_specs=[pl.BlockSpec((1, t + CONV_PAD, c), lambda i: (i, 0, 0)),
                  _const_spec((CONV_PAD, c)), _const_spec((1, c)), _const_spec((1, c)), _const_spec((1, c))],
        out_specs=pl.BlockSpec((1, t, c), lambda i: (i, 0, 0)),
        out_shape=jax.ShapeDtypeStruct((b, t, c), BF16),
        compiler_params=_cp(("parallel",)), name="conformer_conv",
    )(hp, jnp.pad(w_dw, ((0, CONV_PAD - CONV_W), (0, 0))), b_dw[None], ln_g[None], ln_b[None])


def _softplus(z):
    return jnp.maximum(z, 0.0) + jnp.log(1.0 + jnp.exp(-jnp.abs(z)))


def _sb_tile(z, carry, valid, tri2):
    sp = _softplus(z)
    if valid is not None:
        sp = jnp.where(valid, sp, 0.0)
    hi = sp.astype(BF16)
    lo = (sp - hi.astype(F32)).astype(BF16)
    later = _dot(jnp.concatenate([hi, lo], axis=1), tri2)
    w = jnp.exp(z - sp - later - carry)
    if valid is not None:
        w = jnp.where(valid, w, 0.0)
    return w.astype(BF16), carry + later[:, 0:1] + sp[:, 0:1]


def _tri2(tk):
    s = jnp.arange(tk)[:, None]
    j = jnp.arange(tk)[None, :]
    tri = (s > j).astype(BF16)
    return jnp.concatenate([tri, tri], axis=0)


def _sb_prompt_body(q_ref, k_ref, v_ref, tri_ref, o_ref):
    i = pl.program_id(2)
    tq = q_ref.shape[2]
    row = lax.broadcasted_iota(jnp.int32, (tq, tq), 0)
    col = lax.broadcasted_iota(jnp.int32, (tq, tq), 1)
    diag_valid = col < row
    tri2 = tri_ref[...]
    outs = []
    for hh in range(2):
        q = q_ref[0, hh]

        def tile(j, carry, acc, valid, hh=hh, q=q):
            start = pl.multiple_of(j * tq, tq)
            kt = k_ref[0, hh, pl.ds(start, tq), :]
            vt = v_ref[0, hh, pl.ds(start, tq), :]
            w, carry = _sb_tile(_nt(q, kt), carry, valid, tri2)
            return carry, acc + _dot(w, vt)

        st = tile(i, jnp.zeros((tq, 1), F32), jnp.zeros((tq, HEAD_DIM), F32), diag_valid)
        st = lax.fori_loop(0, i, lambda n, s, tile=tile: tile(i - 1 - n, s[0], s[1], None), st)
        outs.append(st[1])
    o_ref[0] = jnp.concatenate(outs, axis=1).astype(o_ref.dtype)


def _sb_prompt(q_hm, k_hm, v_hm):
    b, h, t, d = q_hm.shape
    tq = min(ATT_TILE, t)
    kv_spec = pl.BlockSpec((1, 2, t, d), lambda bi, hp, i: (bi, hp, 0, 0))
    return pl.pallas_call(
        _sb_prompt_body,
        grid=(b, h // 2, t // tq),
        in_specs=[pl.BlockSpec((1, 2, tq, d), lambda bi, hp, i: (bi, hp, i, 0)), kv_spec, kv_spec,
                  _const_spec((2 * tq, tq))],
        out_specs=pl.BlockSpec((1, tq, 2 * d), lambda bi, hp, i: (bi, i, hp)),
        out_shape=jax.ShapeDtypeStruct((b, t, h * d), BF16),
        compiler_params=_cp(("parallel", "parallel", "arbitrary")), name="sb_prompt",
    )(q_hm, k_hm, v_hm, _tri2(tq))


def _block_diag_rows(q, groups):
    b, t, h, d = q.shape
    per = h // groups
    qg = jnp.transpose(q.reshape(b, t, groups, per, d), (0, 2, 3, 1, 4))
    eye = jnp.eye(groups, dtype=q.dtype)[None, :, None, None, :, None]
    return (qg[:, :, :, :, None, :] * eye).reshape(b, h * t, groups * d)


def _diag_rows_out(o, t, groups):
    b, rows, gd = o.shape
    d = gd // groups
    per = rows // (groups * t)
    o6 = o.reshape(b, groups, per, t, groups, d)
    idx = jnp.arange(groups)
    og = o6[:, idx, :, :, idx, :]
    return jnp.transpose(og, (1, 3, 0, 2, 4)).reshape(b, t, groups * per * d)


def _sb_sample_body(npages, t_new, pt_ref, qbd_ref, kn_ref, vn_ref, tri_ref, *refs):
    k_refs = refs[:npages]
    v_refs = refs[npages:2 * npages]
    o_ref = refs[2 * npages]
    qbd = qbd_ref[0]
    rows = qbd.shape[0]
    tri2 = tri_ref[...]
    width = kn_ref.shape[2]
    pad = jnp.zeros((PAGE - t_new, width), F32)
    kn = jnp.concatenate([kn_ref[0], pad], axis=0).astype(BF16)
    vn = jnp.concatenate([vn_ref[0], pad], axis=0).astype(BF16)
    qoff = lax.broadcasted_iota(jnp.int32, (rows, PAGE), 0) % t_new
    col = lax.broadcasted_iota(jnp.int32, (rows, PAGE), 1)
    w, carry = _sb_tile(_nt(qbd, kn), jnp.zeros((rows, 1), F32), col < qoff, tri2)
    acc = _dot(w, vn)
    for p in reversed(range(npages)):
        w, carry = _sb_tile(_nt(qbd, k_refs[p][0].astype(BF16)), carry, None, tri2)
        acc = acc + _dot(w, v_refs[p][0].astype(BF16))
    o_ref[0] = acc


def _page_specs(npages, width):
    return [pl.BlockSpec((1, PAGE, width), lambda b, pt, p=p: (pt[b, p], 0, 0)) for p in range(npages)]


def _sb_sample(q, k_new, v_new, cache_k, cache_v, page_table):
    b, t, w = k_new.shape
    npages = page_table.shape[1]
    qbd = _block_diag_rows(q.reshape(b, t, SB_HEADS, HEAD_DIM), SB_HEADS)
    rows = SB_HEADS * t
    bspec = lambda r, c: pl.BlockSpec((1, r, c), lambda bi, pt: (bi, 0, 0))
    grid_spec = pltpu.PrefetchScalarGridSpec(
        num_scalar_prefetch=1, grid=(b,),
        in_specs=[bspec(rows, w), bspec(t, w), bspec(t, w),
                  pl.BlockSpec((2 * PAGE, PAGE), lambda bi, pt: (0, 0))]
                 + _page_specs(npages, w) + _page_specs(npages, w),
        out_specs=bspec(rows, w))
    o = pl.pallas_call(
        functools.partial(_sb_sample_body, npages, t),
        grid_spec=grid_spec,
        out_shape=jax.ShapeDtypeStruct((b, rows, w), F32),
        compiler_params=_cp(("parallel",)), name="sb_sample",
    )(page_table, qbd, k_new, v_new, _tri2(PAGE), *([cache_k] * npages), *([cache_v] * npages))
    return _diag_rows_out(o, t, SB_HEADS)


def _proj_res_body(n_in, *refs):
    h_ref = refs[0]
    x_refs = refs[1:1 + n_in]
    w_ref = refs[1 + n_in]
    o_ref = refs[2 + n_in]
    acc = h_ref[...]
    off = 0
    for x_ref in x_refs:
        k = x_ref.shape[1]
        acc = acc + _dot(x_ref[...].astype(BF16), w_ref[off:off + k, :])
        off += k
    o_ref[...] = acc


def _proj_res(h2, xs, w):
    m = h2.shape[0]
    tm = _row_tile(m)
    return pl.pallas_call(
        functools.partial(_proj_res_body, len(xs)),
        grid=(m // tm,),
        in_specs=[_row_spec(tm, D_MODEL)] + [_row_spec(tm, x.shape[1]) for x in xs] + [_const_spec(w.shape)],
        out_specs=_row_spec(tm, D_MODEL),
        out_shape=jax.ShapeDtypeStruct((m, D_MODEL), F32),
        compiler_params=_cp(("parallel",)), name="proj_res",
    )(h2, *xs, w)


def _norm_proj_body(scale, h_ref, g_ref, w_ref, o_ref):
    xn = _rms(h_ref[...], g_ref[...]).astype(BF16)
    o_ref[...] = (_dot(xn, w_ref[...]) * scale).astype(o_ref.dtype)


def _norm_proj(h2, g, w, scale):
    m = h2.shape[0]
    tm = _row_tile(m)
    return pl.pallas_call(
        functools.partial(_norm_proj_body, scale),
        grid=(m // tm,),
        in_specs=[_row_spec(tm, D_MODEL), _const_spec((1, D_MODEL)), _const_spec(w.shape)],
        out_specs=_row_spec(tm, w.shape[1]),
        out_shape=jax.ShapeDtypeStruct((m, w.shape[1]), BF16),
        compiler_params=_cp(("parallel",)), name="norm_proj",
    )(h2, g, w)


def _mem_kv_body(x_ref, wk0, wv0, wk1, wv1, *o_refs):
    x = x_ref[...].astype(BF16)
    for n, w_ref in enumerate((wk0, wv0, wk1, wv1)):
        y = _dot(x, w_ref[...])
        o_refs[2 * n][...] = y
        o_refs[2 * n + 1][...] = y.astype(BF16)


def _mem_kv(x2, ws):
    m = x2.shape[0]
    tm = _row_tile(m)
    outs, specs = [], []
    for _ in ws:
        outs += [jax.ShapeDtypeStruct((m, D_MODEL), F32), jax.ShapeDtypeStruct((m, D_MODEL), BF16)]
        specs += [_row_spec(tm, D_MODEL)] * 2
    return pl.pallas_call(
        _mem_kv_body, grid=(m // tm,),
        in_specs=[_row_spec(tm, D_MODEL)] + [_const_spec(w.shape) for w in ws],
        out_specs=specs, out_shape=outs,
        compiler_params=_cp(("parallel",)), name="mem_kv",
    )(x2, *ws)


def _mem_attn_body(q_ref, k_ref, v_ref, o_ref):
    for hh in range(MEM_HEADS):
        sl = slice(hh * MEM_HD, (hh + 1) * MEM_HD)
        s = _nt(q_ref[0, :, sl], k_ref[0, :, sl].astype(BF16))
        m = jnp.max(s, axis=-1, keepdims=True)
        e = jnp.exp(s - m)
        p = e / jnp.sum(e, axis=-1, keepdims=True)
        o_ref[0, :, sl] = _dot(p.astype(BF16), v_ref[0, :, sl].astype(BF16)).astype(o_ref.dtype)


def _mem_attn(q, mk, mv):
    b, t, d = q.shape
    tq = min(512, t)
    ml = mk.shape[1]
    kv_spec = pl.BlockSpec((1, ml, d), lambda bi, i: (bi, 0, 0))
    return pl.pallas_call(
        _mem_attn_body, grid=(b, t // tq),
        in_specs=[pl.BlockSpec((1, tq, d), lambda bi, i: (bi, i, 0)), kv_spec, kv_spec],
        out_specs=pl.BlockSpec((1, tq, d), lambda bi, i: (bi, i, 0)),
        out_shape=jax.ShapeDtypeStruct((b, t, d), BF16),
        compiler_params=_cp(("parallel", "parallel")), name="mem_attn",
    )(q, mk, mv)


FFN_CHUNK = 512


def _ffn_body(final, h_ref, g_ref, wg_ref, wu_ref, wd_ref, *rest):
    if final:
        gf_ref, o_ref = rest
    else:
        (o_ref,) = rest
    h = h_ref[...]
    xn = _rms(h, g_ref[...]).astype(BF16)
    acc = h
    for lo in range(0, D_FF, FFN_CHUNK):
        hi = min(lo + FFN_CHUNK, D_FF)
        gate = _dot(xn, wg_ref[:, lo:hi])
        up = _dot(xn, wu_ref[:, lo:hi])
        act = (gate * _sigmoid(gate) * up).astype(BF16)
        acc = acc + _dot(act, wd_ref[lo:hi, :])
    if final:
        acc = _rms(acc, gf_ref[...])
    o_ref[...] = acc


def _ffn(h2, g, wg, wu, wd, final_g=None):
    m = h2.shape[0]
    tm = _row_tile(m)
    final = final_g is not None
    once = lambda shape: pl.BlockSpec(shape, lambda i: (0, 0), pipeline_mode=pl.Buffered(1))
    in_specs = [_row_spec(tm, D_MODEL), _const_spec((1, D_MODEL)), once(wg.shape), once(wu.shape), once(wd.shape)]
    args = [h2, g, wg, wu, wd]
    if final:
        in_specs.append(_const_spec((1, D_MODEL)))
        args.append(final_g)
    return pl.pallas_call(
        functools.partial(_ffn_body, final), grid=(m // tm,),
        in_specs=in_specs, out_specs=_row_spec(tm, D_MODEL),
        out_shape=jax.ShapeDtypeStruct((m, D_MODEL), F32),
        compiler_params=_cp(("parallel",)), name="ffn",
    )(*args)


ODD_Q = NSA_HEADS * HEAD_DIM
ODD_GATES = 3 * NSA_HEADS
ODD_IN = ODD_Q + 6 * NSA_KV_DIM + ODD_GATES
ODD_IN_PAD = -(-ODD_IN // 128) * 128


def _odd_in_body(head_major, h_ref, g_ref, w_ref, c_ref, s_ref, *o_refs):
    xn = _rms(h_ref[...], g_ref[...]).astype(BF16)
    c = c_ref[...]
    s = s_ref[...]

    def mm(lo, hi):
        return _dot(xn, w_ref[:, lo:hi])

    kv = lambda n: mm(ODD_Q + n * NSA_KV_DIM, ODD_Q + (n + 1) * NSA_KV_DIM)
    q = _rope_wide(mm(0, ODD_Q), c, s) * (HEAD_DIM ** -0.5)
    ck, cv = kv(0), kv(1)
    sk, sv = _rope_wide(kv(2), c, s), kv(3)
    wk, wv = _rope_wide(kv(4), c, s), kv(5)
    gates = _sigmoid(mm(ODD_Q + 6 * NSA_KV_DIM, ODD_IN_PAD))[:, :ODD_GATES]
    for ref, val in zip(o_refs[:6], (ck, cv, sk, sv, wk, wv)):
        ref[...] = val
    if head_major:
        q_ref, g_out, skh, svh, wkh, wvh = o_refs[6:]
        for hh in range(NSA_HEADS):
            q_ref[0, hh] = q[:, hh * HEAD_DIM:(hh + 1) * HEAD_DIM].astype(BF16)
        per = 3 * NSA_GQA
        for gg in range(NSA_KVH):
            sl = slice(gg * HEAD_DIM, (gg + 1) * HEAD_DIM)
            g_out[0, gg] = gates[:, gg * per:(gg + 1) * per]
            skh[0, gg] = sk[:, sl].astype(BF16)
            svh[0, gg] = sv[:, sl].astype(BF16)
            wkh[0, gg] = wk[:, sl].astype(BF16)
            wvh[0, gg] = wv[:, sl].astype(BF16)
    else:
        q_ref, g_out = o_refs[6:]
        q_ref[...] = q.astype(BF16)
        g_out[...] = gates


def _odd_in(h2, g, w, cos_t, sin_t, batch, head_major):
    m = h2.shape[0]
    t = m // batch
    tm = _row_tile(m)
    nt = max(t // tm, 1)
    ntab = cos_t.shape[0] // tm
    tab_spec = pl.BlockSpec((tm, 128), lambda i: (i % ntab, 0))
    kv = jax.ShapeDtypeStruct((m, NSA_KV_DIM), F32)
    outs = [kv] * 6
    specs = [_row_spec(tm, NSA_KV_DIM)] * 6
    if head_major:
        kvh = jax.ShapeDtypeStruct((batch, NSA_KVH, t, HEAD_DIM), BF16)
        outs += [jax.ShapeDtypeStruct((batch, NSA_HEADS, t, HEAD_DIM), BF16),
                 jax.ShapeDtypeStruct((batch, NSA_KVH, t, 3 * NSA_GQA), F32), kvh, kvh, kvh, kvh]
        specs += [_hm_spec(NSA_HEADS, tm, HEAD_DIM, nt), _hm_spec(NSA_KVH, tm, 3 * NSA_GQA, nt)]
        specs += [_hm_spec(NSA_KVH, tm, HEAD_DIM, nt)] * 4
    else:
        outs += [jax.ShapeDtypeStruct((m, ODD_Q), BF16), jax.ShapeDtypeStruct((m, ODD_GATES), F32)]
        specs += [_row_spec(tm, ODD_Q), _row_spec(tm, ODD_GATES)]
    return pl.pallas_call(
        functools.partial(_odd_in_body, head_major), grid=(m // tm,),
        in_specs=[_row_spec(tm, D_MODEL), _const_spec((1, D_MODEL)), _const_spec(w.shape), tab_spec, tab_spec],
        out_specs=specs, out_shape=outs,
        compiler_params=_cp(("parallel",)), name="odd_in",
    )(h2, g, w, cos_t, sin_t)


def _gelu_tanh(x):
    return 0.5 * x * (1.0 + jnp.tanh(math.sqrt(2.0 / math.pi) * (x + 0.044715 * x * x * x)))


def _compress_body(add_pe, use_rope, x_ref, pe_ref, w1_ref, w2_ref, c_ref, s_ref, o_ref, acc_ref, wbd_ref):
    l = pl.program_id(0)

    @pl.when(l == 0)
    def _():
        acc_ref[...] = jnp.zeros_like(acc_ref)
        wbd_ref[...] = jnp.zeros_like(wbd_ref)

    for gg in range(NSA_KVH):
        wbd_ref[gg * HEAD_DIM:(gg + 1) * HEAD_DIM, gg * CMP_HID:(gg + 1) * CMP_HID] = w1_ref[0]
    x = x_ref[...]
    if add_pe:
        x = (x + pe_ref[0]).astype(BF16)
    acc_ref[...] += _dot(x, wbd_ref[...])

    @pl.when(l == CMP_BLK - 1)
    def _():
        hid = _gelu_tanh(acc_ref[...]).astype(BF16)
        y = _dot(hid, w2_ref[...])
        if use_rope:
            y = _rope_wide(y, c_ref[...], s_ref[...])
        o_ref[...] = y


def _compress(x2, pe, w1, w2, tables, add_pe):
    rows = x2.shape[0]
    pe_t = jnp.tile(pe, (1, NSA_KVH))[:, None, :]
    w2bd = jnp.kron(jnp.eye(NSA_KVH, dtype=F32), w2).astype(BF16)
    use_rope = tables is not None
    if use_rope:
        c_t, s_t = tables
    else:
        c_t = s_t = jnp.zeros((rows, 128), F32)
    return pl.pallas_call(
        functools.partial(_compress_body, add_pe, use_rope), grid=(CMP_BLK,),
        in_specs=[pl.BlockSpec((rows, NSA_KV_DIM), lambda l: (0, l)),
                  pl.BlockSpec((1, 1, NSA_KV_DIM), lambda l: (l, 0, 0)),
                  pl.BlockSpec((1, HEAD_DIM, CMP_HID), lambda l: (l, 0, 0)),
                  _const_spec(w2bd.shape), _const_spec((rows, 128)), _const_spec((rows, 128))],
        out_specs=_const_spec((rows, NSA_KV_DIM)),
        out_shape=jax.ShapeDtypeStruct((rows, NSA_KV_DIM), F32),
        scratch_shapes=[pltpu.VMEM((rows, NSA_KVH * CMP_HID), F32),
                        pltpu.VMEM((NSA_KV_DIM, NSA_KVH * CMP_HID), BF16)],
        compiler_params=_cp(("arbitrary",)), name="compress",
    )(x2, pe_t, w1.astype(BF16), w2bd, c_t, s_t)


NB_PAD = 128
RANK_ROWS = 64


def _cmp_and_select(q, kc, vc, qpos, rows_per_q, sc_ref, nb):
    tq = qpos.shape[0]
    s = _nt(q, kc).reshape(rows_per_q, tq, NB_PAD)
    n_idx = lax.broadcasted_iota(jnp.int32, (tq, NB_PAD), 1)
    valid = (n_idx * CMP_BLK + (CMP_BLK - 1)) <= qpos
    sm = jnp.where(valid[None], s, NEG_INF)
    m = jnp.max(sm, axis=-1, keepdims=True)
    p = jnp.where(valid[None], jnp.exp(sm - m), 0.0)
    p = p / jnp.maximum(jnp.sum(p, axis=-1, keepdims=True), 1e-30)
    o_cmp = _dot(p.reshape(rows_per_q * tq, NB_PAD).astype(BF16), vc)
    cur = qpos // CMP_BLK
    forced = (n_idx == 0) | (n_idx == cur) | (n_idx == cur - 1)
    score = jnp.where(n_idx > cur, -1.0, jnp.where(forced, FORCE_SCORE, jnp.sum(p, axis=0)))
    tqp = sc_ref.shape[1]
    if tqp > tq:
        score = jnp.concatenate([score, jnp.zeros((tqp - tq, NB_PAD), F32)], axis=0)
    sc_ref[...] = score.T[:RANK_ROWS]
    j_idx = lax.broadcasted_iota(jnp.int32, (RANK_ROWS, tqp), 0)

    def rank_body(i, cnt):
        r = sc_ref[pl.ds(i, 1), :]
        st = sc_ref[...]
        gt = jnp.where(r > st, 1.0, 0.0)
        eq = jnp.where(r == st, jnp.where(i < j_idx, 1.0, 0.0), 0.0)
        return cnt + gt + eq

    cnt = lax.fori_loop(0, nb, rank_body, jnp.zeros((RANK_ROWS, tqp), F32))
    sel_t = jnp.where(cnt < float(SEL_TOPN), 1.0, 0.0)
    sel = jnp.concatenate([sel_t, jnp.zeros((NB_PAD - RANK_ROWS, tqp), F32)], axis=0).T
    return o_cmp, sel[:tq]


def _osm(z, st, vt):
    m, l, acc = st
    m_new = jnp.maximum(m, jnp.max(z, axis=-1, keepdims=True))
    alpha = jnp.exp(m - m_new)
    p = jnp.exp(z - m_new)
    return m_new, alpha * l + jnp.sum(p, axis=-1, keepdims=True), alpha * acc + _dot(p.astype(BF16), vt)


def _osm_init(rows, width):
    return (jnp.full((rows, 1), NEG_INF, F32), jnp.zeros((rows, 1), F32), jnp.zeros((rows, width), F32))


def _osm_out(st):
    return st[2] / jnp.maximum(st[1], 1e-30)


def _nsa_prompt_body(nb, q_ref, kc_ref, vc_ref, sk_ref, sv_ref, wk_ref, wv_ref, g_ref, e_ref, o_ref, sc_ref):
    i = pl.program_id(2)
    tq = q_ref.shape[2]
    tk = tq
    rq = NSA_GQA
    q4 = q_ref[0].reshape(rq * tq, HEAD_DIM)
    qpos = i * tq + lax.broadcasted_iota(jnp.int32, (tq, 1), 0)
    o_cmp, sel = _cmp_and_select(q4, kc_ref[0, 0], vc_ref[0, 0], qpos, rq, sc_ref, nb)
    selb = sel.astype(BF16)
    row = lax.broadcasted_iota(jnp.int32, (tq, tk), 0)
    col = lax.broadcasted_iota(jnp.int32, (tq, tk), 1)

    def bias_rows(bias):
        return jnp.broadcast_to(bias[None], (rq, tq, tk)).reshape(rq * tq, tk)

    def sel_tile(j, st, diag):
        start = pl.multiple_of(j * tk, tk)
        chosen = _dot(selb, e_ref[:, pl.ds(start, tk)])
        bias = (chosen - 1.0) * 1e30
        if diag:
            bias = jnp.where(col <= row, bias, NEG_INF)
        z = _nt(q4, sk_ref[0, 0, pl.ds(start, tk), :]) + bias_rows(bias)
        return _osm(z, st, sv_ref[0, 0, pl.ds(start, tk), :])

    st = lax.fori_loop(0, i, lambda j, s: sel_tile(j, s, False), _osm_init(rq * tq, HEAD_DIM))
    o_sel = _osm_out(sel_tile(i, st, True))

    def win_tile(back, st):
        j_raw = i - back
        start = pl.multiple_of(jnp.maximum(j_raw, 0) * tk, tk)
        if back == 0:
            ok = col <= row
        elif back < WINDOW // tk:
            ok = col >= 0
        else:
            ok = col > row
        bias = jnp.where(ok & (j_raw >= 0), 0.0, NEG_INF)
        z = _nt(q4, wk_ref[0, 0, pl.ds(start, tk), :]) + bias_rows(bias)
        return _osm(z, st, wv_ref[0, 0, pl.ds(start, tk), :])

    st = _osm_init(rq * tq, HEAD_DIM)
    for back in range(WINDOW // tk + 1):
        st = win_tile(back, st)
    o_win = _osm_out(st)

    gates = g_ref[0, 0]
    outs = []
    for r in range(rq):
        rows = slice(r * tq, (r + 1) * tq)
        gc, gs, gw = (gates[:, 3 * r + n:3 * r + n + 1] for n in range(3))
        outs.append(gc * o_cmp[rows] + gs * o_sel[rows] + gw * o_win[rows])
    o_ref[0] = jnp.concatenate(outs, axis=1).astype(o_ref.dtype)


def _nsa_prompt(q_hm, kc_hm, vc_hm, sk_hm, sv_hm, wk_hm, wv_hm, gates_hm, nb):
    b, _, t, d = q_hm.shape
    tq = min(ATT_TILE, t)
    assert WINDOW % tq == 0 and nb <= RANK_ROWS
    expand = (jnp.arange(NB_PAD)[:, None] == (jnp.arange(t)[None, :] // CMP_BLK)).astype(BF16)
    kv_spec = pl.BlockSpec((1, 1, t, d), lambda bi, g, i: (bi, g, 0, 0))
    c_spec = pl.BlockSpec((1, 1, NB_PAD, d), lambda bi, g, i: (bi, g, 0, 0))
    return pl.pallas_call(
        functools.partial(_nsa_prompt_body, nb),
        grid=(b, NSA_KVH, t // tq),
        in_specs=[pl.BlockSpec((1, NSA_GQA, tq, d), lambda bi, g, i: (bi, g, i, 0)), c_spec, c_spec,
                  kv_spec, kv_spec, kv_spec, kv_spec,
                  pl.BlockSpec((1, 1, tq, 3 * NSA_GQA), lambda bi, g, i: (bi, g, i, 0)),
                  _const_spec((NB_PAD, t))],
        out_specs=pl.BlockSpec((1, tq, NSA_GQA * d), lambda bi, g, i: (bi, i, g)),
        out_shape=jax.ShapeDtypeStruct((b, t, NSA_HEADS * d), BF16),
        scratch_shapes=[pltpu.VMEM((RANK_ROWS, max(tq, 128)), F32)],
        compiler_params=_cp(("parallel", "parallel", "arbitrary")), name="nsa_prompt",
    )(q_hm, kc_hm, vc_hm, sk_hm, sv_hm, wk_hm, wv_hm, gates_hm, expand)


def _cmp_gather_body(npages, t_new, pt_ref, new_ref, pe_ref, *refs):
    page_refs = refs[:npages]
    o_ref = refs[npages]
    pe = pe_ref[...]
    pe2 = jnp.concatenate([pe, pe], axis=0)
    for p in range(npages):
        o_ref[0, p * PAGE:(p + 1) * PAGE, :] = (page_refs[p][0] + pe2).astype(BF16)
    tail = jnp.concatenate([new_ref[0], jnp.zeros((CMP_BLK - t_new, NSA_KV_DIM), F32)], axis=0) + pe
    o_ref[0, npages * PAGE:npages * PAGE + CMP_BLK, :] = tail.astype(BF16)


def _cmp_gather(cache, new, pe, page_table):
    b, t, w = new.shape
    npages = page_table.shape[1]
    rows = npages * PAGE + CMP_BLK
    grid_spec = pltpu.PrefetchScalarGridSpec(
        num_scalar_prefetch=1, grid=(b,),
        in_specs=[pl.BlockSpec((1, t, w), lambda bi, pt: (bi, 0, 0)),
                  pl.BlockSpec((CMP_BLK, w), lambda bi, pt: (0, 0))] + _page_specs(npages, w),
        out_specs=pl.BlockSpec((1, rows, w), lambda bi, pt: (bi, 0, 0)))
    return pl.pallas_call(
        functools.partial(_cmp_gather_body, npages, t), grid_spec=grid_spec,
        out_shape=jax.ShapeDtypeStruct((b, rows, w), BF16),
        compiler_params=_cp(("parallel",)), name="cmp_gather",
    )(page_table, new, jnp.tile(pe, (1, NSA_KVH)), *([cache] * npages))


def _nsa_sample_body(npages, t_new, past, nb, pt_ref, qbd_ref, kc_ref, vc_ref, g_ref,
                     skn_ref, svn_ref, wkn_ref, wvn_ref, wkc_ref, wvc_ref, *refs):
    sk_refs = refs[:npages]
    sv_refs = refs[npages:2 * npages]
    o_ref = refs[2 * npages]
    sc_ref = refs[2 * npages + 1]
    qbd = qbd_ref[0]
    rows = qbd.shape[0]
    w = qbd.shape[1]
    rq = rows // t_new
    t_idx = lax.broadcasted_iota(jnp.int32, (t_new, 1), 0)
    qpos = past + t_idx

    o_cmp_parts, sel_parts = [], []
    per = NSA_GQA * t_new
    for gg in range(NSA_KVH):
        oc, sel = _cmp_and_select(qbd[gg * per:(gg + 1) * per], kc_ref[0], vc_ref[0], qpos, NSA_GQA,
                                  sc_ref, nb)
        o_cmp_parts.append(oc)
        sel_parts.append(jnp.broadcast_to(sel[None], (NSA_GQA, t_new, NB_PAD)).reshape(per, NB_PAD))
    o_cmp = jnp.concatenate(o_cmp_parts, axis=0)
    sel_bias = (jnp.concatenate(sel_parts, axis=0) - 1.0) * 1e30

    qoff = lax.broadcasted_iota(jnp.int32, (rows, PAGE), 0) % t_new
    col = lax.broadcasted_iota(jnp.int32, (rows, PAGE), 1)
    pad = jnp.zeros((PAGE - t_new, w), F32)
    new_bias = jnp.where(col <= qoff, 0.0, NEG_INF)

    def padded(ref):
        return jnp.concatenate([ref[0], pad], axis=0).astype(BF16)

    st = _osm_init(rows, w)
    half = col < CMP_BLK
    for p in range(npages):
        b0 = sel_bias[:, 2 * p:2 * p + 1]
        b1 = sel_bias[:, 2 * p + 1:2 * p + 2]
        z = _nt(qbd, sk_refs[p][0].astype(BF16)) + jnp.where(half, b0, b1)
        st = _osm(z, st, sv_refs[p][0].astype(BF16))
    last = (past // CMP_BLK)
    z = _nt(qbd, padded(skn_ref)) + new_bias + sel_bias[:, last:last + 1]
    o_sel = _osm_out(_osm(z, st, padded(svn_ref)))

    st = _osm(_nt(qbd, padded(wkn_ref)) + new_bias, _osm_init(rows, w), padded(wvn_ref))
    wb = wkc_ref.shape[1]
    for c in range(wb // PAGE):
        kpos = (past - wb) + c * PAGE + col
        diff = (past + qoff) - kpos
        bias = jnp.where((diff < WINDOW) & (kpos >= 0), 0.0, NEG_INF)
        z = _nt(qbd, wkc_ref[0, c * PAGE:(c + 1) * PAGE, :].astype(BF16)) + bias
        st = _osm(z, st, wvc_ref[0, c * PAGE:(c + 1) * PAGE, :].astype(BF16))
    o_win = _osm_out(st)

    gates = g_ref[0]
    o_ref[0] = gates[:, 0:1] * o_cmp + gates[:, 1:2] * o_sel + gates[:, 2:3] * o_win


def _nsa_sample(q, kc, vc, gates, sk_new, sv_new, wk_new, wv_new, win_k, win_v,
                cache_sk, cache_sv, page_table, past, nb):
    b, t, w = sk_new.shape
    npages = page_table.shape[1]
    assert nb <= RANK_ROWS and (past // CMP_BLK) < nb
    qbd = _block_diag_rows(q.reshape(b, t, NSA_HEADS, HEAD_DIM), NSA_KVH)
    rows = NSA_HEADS * t
    g_rows = jnp.transpose(gates.reshape(b, t, NSA_KVH, NSA_GQA, 3), (0, 2, 3, 1, 4)).reshape(b, rows, 3)
    bspec = lambda r, c: pl.BlockSpec((1, r, c), lambda bi, pt: (bi, 0, 0))
    wb = win_k.shape[1]
    grid_spec = pltpu.PrefetchScalarGridSpec(
        num_scalar_prefetch=1, grid=(b,),
        in_specs=[bspec(rows, w), bspec(NB_PAD, w), bspec(NB_PAD, w), bspec(rows, 3),
                  bspec(t, w), bspec(t, w), bspec(t, w), bspec(t, w), bspec(wb, w), bspec(wb, w)]
                 + _page_specs(npages, w) + _page_specs(npages, w),
        out_specs=bspec(rows, w),
        scratch_shapes=[pltpu.VMEM((RANK_ROWS, max(t, 128)), F32)])
    o = pl.pallas_call(
        functools.partial(_nsa_sample_body, npages, t, past, nb), grid_spec=grid_spec,
        out_shape=jax.ShapeDtypeStruct((b, rows, w), F32),
        compiler_params=_cp(("parallel",)), name="nsa_sample",
    )(page_table, qbd, kc, vc, g_rows, sk_new, sv_new, wk_new, wv_new, win_k, win_v,
      *([cache_sk] * npages), *([cache_sv] * npages))
    return _diag_rows_out(o, t, NSA_KVH)


def _common_tail(h2, batch, l, mem_k, mem_v, p, final_g):
    t = h2.shape[0] // batch
    q = _norm_proj(h2, p["norm_mem"][l][None], p["w_mem_q"][l], MEM_HD ** -0.5)
    o = _mem_attn(q.reshape(batch, t, D_MODEL), mem_k, mem_v)
    h2 = _proj_res(h2, [o.reshape(batch * t, D_MODEL)], p["w_mem_o"][l])
    return _ffn(h2, p["norm_ffn"][l][None], p["w_ffn_gate"][l], p["w_ffn_up"][l], p["w_ffn_down"][l], final_g)


def _even_layer(h2, batch, p, conv_buf, sample_ctx):
    t = h2.shape[0] // batch
    prompt = sample_ctx is None
    res = _even_in(h2, p["norm_mix"][0][None], p["w_in_even"][0], batch, prompt)
    glu, k32, v32 = res[0], res[1], res[2]
    glu3 = glu.reshape(batch, t, CONV_CH)
    y_conv = _conformer_conv(glu3, conv_buf, p["conv_w"][0], p["conv_b"][0], p["conv_ln_g"][0], p["conv_ln_b"][0])
    new_buf = jnp.concatenate([conv_buf.astype(F32), glu3], axis=1)[:, -(CONV_W - 1):]
    if prompt:
        o = _sb_prompt(res[3], res[4], res[5])
    else:
        cache_k, cache_v, page_table = sample_ctx
        o = _sb_sample(res[3].reshape(batch, t, SB_DIM), k32.reshape(batch, t, SB_DIM),
                       v32.reshape(batch, t, SB_DIM), cache_k, cache_v, page_table)
    h2 = _proj_res(h2, [y_conv.reshape(batch * t, CONV_CH), o.reshape(batch * t, SB_DIM)], p["w_mix_out"][0])
    shp = (batch, t, SB_HEADS, HEAD_DIM)
    return h2, new_buf, k32.reshape(shp), v32.reshape(shp)


def _odd_layer(h2, batch, p, sample_ctx):
    t = h2.shape[0] // batch
    prompt = sample_ctx is None
    past = 0 if prompt else sample_ctx["past"]
    tm = _row_tile(h2.shape[0])
    pos = past + jnp.arange(max(t, tm), dtype=jnp.int32) % t
    cos_t, sin_t = _rope_tables(pos, 128)
    res = _odd_in(h2, p["norm_mix"][1][None], p["w_in_odd"], cos_t, sin_t, batch, prompt)
    ck, cv, sk, sv, wk, wv = res[:6]
    nb = -(-(past + t) // CMP_BLK)
    blk_end = jnp.arange(nb, dtype=jnp.int32) * CMP_BLK + (CMP_BLK - 1)
    end_tabs = tuple(jnp.tile(x, (batch, 1)) for x in _rope_tables(blk_end, 128))
    cw = (p["cmp_pe_k"][0], p["cmp_w1_k"][0], p["cmp_w2_k"][0]), (p["cmp_pe_v"][0], p["cmp_w1_v"][0], p["cmp_w2_v"][0])
    if prompt:
        xk = ck.reshape(batch * nb, CMP_BLK * NSA_KV_DIM)
        xv = cv.reshape(batch * nb, CMP_BLK * NSA_KV_DIM)
        add_pe = True
    else:
        pt = sample_ctx["page_table"]
        xk = _cmp_gather(sample_ctx["cmp_k"], ck.reshape(batch, t, NSA_KV_DIM), cw[0][0], pt)
        xv = _cmp_gather(sample_ctx["cmp_v"], cv.reshape(batch, t, NSA_KV_DIM), cw[1][0], pt)
        xk = xk.reshape(batch * nb, CMP_BLK * NSA_KV_DIM)
        xv = xv.reshape(batch * nb, CMP_BLK * NSA_KV_DIM)
        add_pe = False
    kc = _compress(xk, *cw[0], end_tabs, add_pe).reshape(batch, nb, NSA_KV_DIM)
    vc = _compress(xv, *cw[1], None, add_pe).reshape(batch, nb, NSA_KV_DIM)
    padc = lambda x: jnp.pad(x, ((0, 0), (0, NB_PAD - nb), (0, 0))).astype(BF16)
    kvshape = (batch, t, NSA_KVH, HEAD_DIM)
    if prompt:
        hm = lambda x: jnp.transpose(padc(x).reshape(batch, NB_PAD, NSA_KVH, HEAD_DIM), (0, 2, 1, 3))
        q_hm, gates_hm, sk_hm, sv_hm, wk_hm, wv_hm = res[6:]
        o = _nsa_prompt(q_hm, hm(kc), hm(vc), sk_hm, sv_hm, wk_hm, wv_hm, gates_hm, nb)
        keep = min(WINDOW, t)
        win_k = wk.reshape(kvshape)[:, -keep:]
        win_v = wv.reshape(kvshape)[:, -keep:]
    else:
        q, gates = res[6:]
        r3 = lambda x: x.reshape(batch, t, NSA_KV_DIM)
        wkc, wvc = sample_ctx["win_k"], sample_ctx["win_v"]
        wb = wkc.shape[1]
        o = _nsa_sample(q.reshape(batch, t, ODD_Q), padc(kc), padc(vc), gates.reshape(batch, t, ODD_GATES),
                        r3(sk), r3(sv), r3(wk), r3(wv), wkc.reshape(batch, wb, NSA_KV_DIM),
                        wvc.reshape(batch, wb, NSA_KV_DIM), sample_ctx["sel_k"], sample_ctx["sel_v"],
                        sample_ctx["page_table"], past, nb)
        win_k = jnp.concatenate([wkc, wk.reshape(kvshape)], axis=1)[:, -wb:]
        win_v = jnp.concatenate([wvc, wv.reshape(kvshape)], axis=1)[:, -wb:]
    h2 = _proj_res(h2, [o.reshape(batch * t, ODD_Q)], p["w_mix_out"][1])
    r4 = lambda x: x.reshape(kvshape)
    return h2, r4(ck), r4(cv), r4(sk), r4(sv), win_k, win_v


def _trunk(x, p, conv_buf, mem_kv, even_ctx, odd_ctx):
    batch, t, _ = x.shape
    h2 = x.reshape(batch * t, D_MODEL)
    h2, new_buf, sbk, sbv = _even_layer(h2, batch, p, conv_buf, even_ctx)
    h2 = _common_tail(h2, batch, 0, mem_kv[0][0], mem_kv[0][1], p, None)
    h2, ck, cv, sk, sv, wk, wv = _odd_layer(h2, batch, p, odd_ctx)
    h2 = _common_tail(h2, batch, 1, mem_kv[1][0], mem_kv[1][1], p, p["final_norm"][None])
    st = lambda a: a[None]
    return (h2.reshape(batch, t, D_MODEL), st(sbk), st(sbv), st(new_buf),
            st(ck), st(cv), st(sk), st(sv), st(wk), st(wv))


def kernel(x_prompt, x_sample, mem_prompt, cache_sb_k, cache_sb_v, state_conv,
           cache_nsa_cmp_k, cache_nsa_cmp_v, cache_nsa_sel_k, cache_nsa_sel_v,
           cache_nsa_win_k, cache_nsa_win_v, cache_mem_k, cache_mem_v, page_table,
           norm_mix, norm_mem, norm_ffn, final_norm, w_in_even, w_in_odd, w_mix_out,
           conv_w, conv_b, conv_ln_g, conv_ln_b,
           cmp_pe_k, cmp_w1_k, cmp_w2_k, cmp_pe_v, cmp_w1_v, cmp_w2_v,
           w_mem_q, w_mem_k, w_mem_v, w_mem_o, w_ffn_gate, w_ffn_up, w_ffn_down):
    assert norm_mix.shape[0] == 2 and w_in_even.shape[0] == 1 and w_in_odd.shape[0] == 1
    bp = x_prompt.shape[0]
    bs = x_sample.shape[0]
    bf = lambda w: w.astype(BF16)
    p = dict(
        norm_mix=norm_mix, norm_mem=norm_mem, norm_ffn=norm_ffn, final_norm=final_norm,
        w_in_even=bf(w_in_even),
        w_in_odd=bf(jnp.pad(w_in_odd[0], ((0, 0), (0, ODD_IN_PAD - ODD_IN)))),
        w_mix_out=bf(w_mix_out), conv_w=conv_w, conv_b=conv_b, conv_ln_g=conv_ln_g, conv_ln_b=conv_ln_b,
        cmp_pe_k=cmp_pe_k, cmp_w1_k=cmp_w1_k, cmp_w2_k=cmp_w2_k,
        cmp_pe_v=cmp_pe_v, cmp_w1_v=cmp_w1_v, cmp_w2_v=cmp_w2_v,
        w_mem_q=bf(w_mem_q), w_mem_o=bf(w_mem_o),
        w_ffn_gate=bf(w_ffn_gate), w_ffn_up=bf(w_ffn_up), w_ffn_down=bf(w_ffn_down))

    ml = mem_prompt.shape[1]
    mem = _mem_kv(mem_prompt.reshape(bp * ml, D_MODEL),
                  [bf(w_mem_k[0]), bf(w_mem_v[0]), bf(w_mem_k[1]), bf(w_mem_v[1])])
    m3 = lambda a: a.reshape(bp, ml, D_MODEL)
    m4 = lambda a: a.reshape(bp, ml, MEM_HEADS, MEM_HD)
    mem_kv_p = [(m3(mem[1]), m3(mem[3])), (m3(mem[5]), m3(mem[7]))]
    mem_k_p = jnp.stack([m4(mem[0]), m4(mem[4])])
    mem_v_p = jnp.stack([m4(mem[2]), m4(mem[6])])
    zero_buf = jnp.zeros((bp, CONV_W - 1, CONV_CH), F32)
    outs_p = _trunk(x_prompt, p, zero_buf, mem_kv_p, None, None)

    n_pool = cache_sb_k.shape[1]
    past = page_table.shape[1] * PAGE
    flat = lambda c, w: c.reshape(n_pool, PAGE, w)
    even_ctx = (flat(cache_sb_k[0], SB_DIM), flat(cache_sb_v[0], SB_DIM), page_table)
    odd_ctx = dict(past=past, page_table=page_table,
                   cmp_k=flat(cache_nsa_cmp_k[0], NSA_KV_DIM), cmp_v=flat(cache_nsa_cmp_v[0], NSA_KV_DIM),
                   sel_k=flat(cache_nsa_sel_k[0], NSA_KV_DIM), sel_v=flat(cache_nsa_sel_v[0], NSA_KV_DIM),
                   win_k=cache_nsa_win_k[0], win_v=cache_nsa_win_v[0])
    c3 = lambda a: a.reshape(bs, a.shape[1], D_MODEL)
    mem_kv_s = [(c3(cache_mem_k[0]), c3(cache_mem_v[0])), (c3(cache_mem_k[1]), c3(cache_mem_v[1]))]
    outs_s = _trunk(x_sample, p, state_conv[0], mem_kv_s, even_ctx, odd_ctx)

    return (outs_p[0], outs_s[0]) + tuple(outs_p[1:]) + (mem_k_p, mem_v_p) + tuple(outs_s[1:])
```

```python
import functools
import math

import jax
import jax.numpy as jnp
from jax import lax
from jax.experimental import pallas as pl
from jax.experimental.pallas import tpu as pltpu

F32 = jnp.float32
BF16 = jnp.bfloat16

D_MODEL = 1024
HEAD_DIM = 64
CONV_CH = 512
CONV_W = 31
SB_HEADS = 8
SB_DIM = 512
NSA_HEADS = 16
NSA_GQA = 4
NSA_KVH = 4
NSA_KV_DIM = 256
CMP_BLK = 64
CMP_HID = 256
SEL_TOPN = 16
FORCE_SCORE = 1.0e4
WINDOW = 512
MEM_HEADS = 4
MEM_HD = 256
D_FF = 2816
ROPE_THETA = 10000.0
NORM_EPS = 1e-6
NEG_INF = -1e30
PAGE = 128

V7X_VMEM_BYTES = 64 * 1024 * 1024
VMEM_LIMIT = V7X_VMEM_BYTES - 8 * 1024 * 1024
ATT_TILE = 256
NSA_TILE = 512


def _cp(sem):
    return pltpu.CompilerParams(dimension_semantics=sem, vmem_limit_bytes=VMEM_LIMIT)


def _nt(a, b):
    return lax.dot_general(a, b, (((1,), (1,)), ((), ())), preferred_element_type=F32)


def _dot(a, b):
    return jnp.dot(a, b, preferred_element_type=F32)


def _rms(x, g):
    y = x * lax.rsqrt(jnp.mean(x * x, axis=-1, keepdims=True) + NORM_EPS)
    return y * g


def _sigmoid(x):
    return 1.0 / (1.0 + jnp.exp(-x))


def _const_spec(shape):
    n = len(shape)
    return pl.BlockSpec(shape, lambda *a: (0,) * n)


def _row_spec(tm, n):
    return pl.BlockSpec((tm, n), lambda i: (i, 0))


def _hm_spec(heads, tm, width, nt):
    return pl.BlockSpec((1, heads, tm, width), lambda i: (i // nt, 0, i % nt, 0))


def _tr_spec(heads, tm, nt):
    return pl.BlockSpec((1, heads, HEAD_DIM, tm), lambda i: (i // nt, 0, 0, i % nt))


def _store_transposed(x, heads, f32_ref, bf16_ref):
    xt = x.T.reshape(heads, HEAD_DIM, x.shape[0])
    if f32_ref is not None:
        f32_ref[0] = xt
    if bf16_ref is not None:
        bf16_ref[0] = xt.astype(BF16)


def _store_head_major(x, heads, bf16_ref):
    for hh in range(heads):
        bf16_ref[0, hh] = x[:, hh * HEAD_DIM:(hh + 1) * HEAD_DIM].astype(BF16)


def _from_transposed(xt):
    return jnp.transpose(xt, (0, 3, 1, 2))


def _row_tile(m):
    return 512 if m >= 4096 else min(256, m)


def _rope_tables(pos, width):
    half = HEAD_DIM // 2
    inv = ROPE_THETA ** (-jnp.arange(half, dtype=F32) / half)
    ang = pos.astype(F32)[:, None] * inv[None, :]
    cos = jnp.cos(ang)
    sin = jnp.sin(ang)
    c = jnp.concatenate([cos, cos], axis=-1)
    s = jnp.concatenate([-sin, sin], axis=-1)
    reps = width // HEAD_DIM
    return jnp.tile(c, (1, reps)), jnp.tile(s, (1, reps))


def _rope128(x, c, s):
    lane = lax.broadcasted_iota(jnp.int32, x.shape, 1)
    first = (lane % HEAD_DIM) < (HEAD_DIM // 2)
    rot = jnp.where(first, pltpu.roll(x, 96, 1), pltpu.roll(x, 32, 1))
    return x * c + rot * s


def _rope_wide(x, c, s):
    n = x.shape[1] // 128
    return jnp.concatenate([_rope128(x[:, i * 128:(i + 1) * 128], c, s) for i in range(n)], axis=1)


def _even_in_body(head_major, h_ref, g_ref, w_ref, glu_ref, k32_ref, v32_ref, q_ref, *hm_refs):
    xn = _rms(h_ref[...], g_ref[...]).astype(BF16)

    def mm(lo, hi):
        return _dot(xn, w_ref[:, lo:hi])

    glu_ref[...] = mm(0, CONV_CH) * _sigmoid(mm(CONV_CH, 2 * CONV_CH))
    base = 2 * CONV_CH
    q = mm(base, base + SB_DIM) * (HEAD_DIM ** -0.5)
    k = mm(base + SB_DIM, base + 2 * SB_DIM)
    v = mm(base + 2 * SB_DIM, base + 3 * SB_DIM)
    if head_major:
        kh_ref, vh_ref = hm_refs
        _store_transposed(q, SB_HEADS, None, q_ref)
        _store_transposed(k, SB_HEADS, k32_ref, None)
        _store_head_major(k, SB_HEADS, kh_ref)
        _store_transposed(v, SB_HEADS, v32_ref, vh_ref)
    else:
        k32_ref[...] = k
        v32_ref[...] = v
        q_ref[...] = q.astype(BF16)


def _even_in(h2, g, w, batch, head_major):
    m = h2.shape[0]
    t = m // batch
    tm = _row_tile(m)
    nt = max(t // tm, 1)
    outs = [jax.ShapeDtypeStruct((m, CONV_CH), F32)]
    specs = [_row_spec(tm, CONV_CH)]
    if head_major:
        kt = lambda dt: jax.ShapeDtypeStruct((batch, SB_HEADS, HEAD_DIM, t), dt)
        outs += [kt(F32), kt(F32), kt(BF16), jax.ShapeDtypeStruct((batch, SB_HEADS, t, HEAD_DIM), BF16), kt(BF16)]
        specs += [_tr_spec(SB_HEADS, tm, nt)] * 3 + [_hm_spec(SB_HEADS, tm, HEAD_DIM, nt), _tr_spec(SB_HEADS, tm, nt)]
    else:
        outs += [jax.ShapeDtypeStruct((m, SB_DIM), F32)] * 2 + [jax.ShapeDtypeStruct((m, SB_DIM), BF16)]
        specs += [_row_spec(tm, SB_DIM)] * 3
    return pl.pallas_call(
        functools.partial(_even_in_body, head_major),
        grid=(m // tm,),
        in_specs=[_row_spec(tm, D_MODEL), _const_spec((1, D_MODEL)), _const_spec(w.shape)],
        out_specs=specs, out_shape=outs,
        compiler_params=_cp(("parallel",)), name="even_in",
    )(h2, g, w)


CONV_PAD = 32


def _conv_body(t, chunk, hp_ref, w_ref, b_ref, lg_ref, lb_ref, o_ref):
    win_rows = chunk + CONV_PAD

    def one_chunk(c, _):
        base = pl.multiple_of(c * chunk, chunk)
        win = hp_ref[0, pl.ds(base, win_rows), :]
        acc = jnp.zeros((chunk, CONV_CH), F32)
        for r in range(8):
            sh = win if r == 0 else pltpu.roll(win, win_rows - r, 0)
            for a in range(4):
                tap = 8 * a + r
                if tap < CONV_W:
                    acc = acc + sh[8 * a:8 * a + chunk] * w_ref[tap:tap + 1, :]
        y = acc + b_ref[...]
        mu = jnp.mean(y, axis=-1, keepdims=True)
        var = jnp.mean(jnp.square(y - mu), axis=-1, keepdims=True)
        y = (y - mu) * lax.rsqrt(var + NORM_EPS) * lg_ref[...] + lb_ref[...]
        o_ref[0, pl.ds(base, chunk), :] = (y * _sigmoid(y)).astype(o_ref.dtype)
        return 0

    lax.fori_loop(0, t // chunk, one_chunk, 0)


def _conformer_conv(glu, buf, w_dw, b_dw, ln_g, ln_b):
    b, t, c = glu.shape
    hp = jnp.concatenate([buf.astype(F32), glu, jnp.zeros((b, CONV_PAD - (CONV_W - 1), c), F32)], axis=1)
    chunk = min(t, 128)
    return pl.pallas_call(
        functools.partial(_conv_body, t, chunk),
        grid=(b,),
        in_specs=[pl.BlockSpec((1, t + CONV_PAD, c), lambda i: (i, 0, 0)),
                  _const_spec((CONV_PAD, c)), _const_spec((1, c)), _const_spec((1, c)), _const_spec((1, c))],
        out_specs=pl.BlockSpec((1, t, c), lambda i: (i, 0, 0)),
        out_shape=jax.ShapeDtypeStruct((b, t, c), BF16),
        compiler_params=_cp(("parallel",)), name="conformer_conv",
    )(hp, jnp.pad(w_dw, ((0, CONV_PAD - CONV_W), (0, 0))), b_dw[None], ln_g[None], ln_b[None])


def _softplus(z):
    return jnp.maximum(z, 0.0) + jnp.log(1.0 + jnp.exp(-jnp.abs(z)))


def _sb_tiles(zs, carries, valid, tri2, keys_axis, chained=False):
    valids = valid if isinstance(valid, (list, tuple)) else [valid] * len(zs)
    sps = []
    for z, ok in zip(zs, valids):
        sp = _softplus(z)
        sps.append(sp if ok is None else jnp.where(ok, sp, 0.0))
    laters = []
    for sp in sps:
        hi = sp.astype(BF16)
        lo = (sp - hi.astype(F32)).astype(BF16)
        split = jnp.concatenate([hi, lo], axis=keys_axis)
        laters.append(_dot(split, tri2) if keys_axis == 1 else _dot(tri2, split))
    first = (lambda a: a[:, 0:1]) if keys_axis == 1 else (lambda a: a[0:1, :])
    ws, new = [], []
    for n, (z, sp, later, ok) in enumerate(zip(zs, sps, laters, valids)):
        carry = new[-1] if chained and n else carries[0 if chained else n]
        w = jnp.exp(z - sp - later - carry)
        ws.append((w if ok is None else jnp.where(ok, w, 0.0)).astype(BF16))
        new.append(carry + first(later) + first(sp))
    return ws, new


def _tri2(tk, keys_axis):
    s = jnp.arange(tk)[:, None]
    j = jnp.arange(tk)[None, :]
    if keys_axis == 1:
        tri = (s > j).astype(BF16)
        return jnp.concatenate([tri, tri], axis=0)
    tri = (j > s).astype(BF16)
    return jnp.concatenate([tri, tri], axis=1)


def _sb_prompt_body(q_ref, k_ref, v_ref, tri_ref, o_ref):
    i = pl.program_id(2)
    tq = q_ref.shape[3]
    key = lax.broadcasted_iota(jnp.int32, (tq, tq), 0)
    qry = lax.broadcasted_iota(jnp.int32, (tq, tq), 1)
    diag_valid = key < qry
    tri2 = tri_ref[...]
    nh = q_ref.shape[1]

    def tiles(j, st, valid):
        start = pl.multiple_of(j * tq, tq)
        zs = [_dot(k_ref[0, hh, pl.ds(start, tq), :], q_ref[0, hh]) for hh in range(nh)]
        ws, carries = _sb_tiles(zs, [s[0] for s in st], valid, tri2, 0)
        accs = [st[hh][1] + _dot(v_ref[0, hh, :, pl.ds(start, tq)], ws[hh]) for hh in range(nh)]
        return tuple(zip(carries, accs))

    init = tuple((jnp.zeros((1, tq), F32), jnp.zeros((HEAD_DIM, tq), F32)) for _ in range(nh))
    st = tiles(i, init, diag_valid)
    st = lax.fori_loop(0, i, lambda n, s: tiles(i - 1 - n, s, None), st)
    o_ref[0] = jnp.concatenate([s[1] for s in st], axis=0).T.astype(o_ref.dtype)


SB_HEADS_PER_STEP = 4


def _sb_prompt(q_t, k_hm, v_t):
    b, h, d, t = q_t.shape
    tq = min(ATT_TILE, t)
    nh = SB_HEADS_PER_STEP
    return pl.pallas_call(
        _sb_prompt_body,
        grid=(b, h // nh, t // tq),
        in_specs=[pl.BlockSpec((1, nh, d, tq), lambda bi, hp, i: (bi, hp, 0, i)),
                  pl.BlockSpec((1, nh, t, d), lambda bi, hp, i: (bi, hp, 0, 0)),
                  pl.BlockSpec((1, nh, d, t), lambda bi, hp, i: (bi, hp, 0, 0)),
                  _const_spec((tq, 2 * tq))],
        out_specs=pl.BlockSpec((1, tq, nh * d), lambda bi, hp, i: (bi, i, hp)),
        out_shape=jax.ShapeDtypeStruct((b, t, h * d), BF16),
        compiler_params=_cp(("parallel", "parallel", "arbitrary")), name="sb_prompt",
    )(q_t, k_hm, v_t, _tri2(tq, 0))


def _block_diag_rows(q, groups):
    b, t, h, d = q.shape
    per = h // groups
    qg = jnp.transpose(q.reshape(b, t, groups, per, d), (0, 2, 3, 1, 4))
    eye = jnp.eye(groups, dtype=q.dtype)[None, :, None, None, :, None]
    return (qg[:, :, :, :, None, :] * eye).reshape(b, h * t, groups * d)


def _diag_rows_out(o, t, groups):
    b, rows, gd = o.shape
    d = gd // groups
    per = rows // (groups * t)
    o6 = o.reshape(b, groups, per, t, groups, d)
    idx = jnp.arange(groups)
    og = o6[:, idx, :, :, idx, :]
    return jnp.transpose(og, (1, 3, 0, 2, 4)).reshape(b, t, groups * per * d)


def _sb_sample_body(npages, t_new, pt_ref, qbd_ref, kn_ref, vn_ref, tri_ref, *refs):
    k_refs = refs[:npages]
    v_refs = refs[npages:2 * npages]
    o_ref = refs[2 * npages]
    qbd = qbd_ref[0]
    rows = qbd.shape[0]
    tri2 = tri_ref[...]
    width = kn_ref.shape[2]
    pad = jnp.zeros((PAGE - t_new, width), F32)
    kn = jnp.concatenate([kn_ref[0], pad], axis=0).astype(BF16)
    vn = jnp.concatenate([vn_ref[0], pad], axis=0).astype(BF16)
    qoff = lax.broadcasted_iota(jnp.int32, (rows, PAGE), 0) % t_new
    col = lax.broadcasted_iota(jnp.int32, (rows, PAGE), 1)
    order = list(reversed(range(npages)))
    zs = [_nt(qbd, kn)] + [_dot(qbd, k_refs[p][0].reshape(width, PAGE).astype(BF16)) for p in order]
    ws, _ = _sb_tiles(zs, [jnp.zeros((rows, 1), F32)], [col < qoff] + [None] * npages, tri2, 1, chained=True)
    acc = _dot(ws[0], vn)
    for w, p in zip(ws[1:], order):
        acc = acc + _nt(w, v_refs[p][0].reshape(width, PAGE).astype(BF16))
    o_ref[0] = acc


def _page_specs(npages, heads):
    return [pl.BlockSpec((1, heads, HEAD_DIM, PAGE), lambda b, pt, p=p: (pt[b, p], 0, 0, 0))
            for p in range(npages)]


def _keys_minor(cache):
    nd = cache.ndim
    return jnp.transpose(cache, tuple(range(nd - 3)) + (nd - 2, nd - 1, nd - 3))


def _sb_sample(q, k_new, v_new, cache_k, cache_v, page_table):
    b, t, w = k_new.shape
    npages = page_table.shape[1]
    qbd = _block_diag_rows(q.reshape(b, t, SB_HEADS, HEAD_DIM), SB_HEADS)
    rows = SB_HEADS * t
    bspec = lambda r, c: pl.BlockSpec((1, r, c), lambda bi, pt: (bi, 0, 0))
    grid_spec = pltpu.PrefetchScalarGridSpec(
        num_scalar_prefetch=1, grid=(b,),
        in_specs=[bspec(rows, w), bspec(t, w), bspec(t, w),
                  pl.BlockSpec((2 * PAGE, PAGE), lambda bi, pt: (0, 0))]
                 + _page_specs(npages, SB_HEADS) + _page_specs(npages, SB_HEADS),
        out_specs=bspec(rows, w))
    o = pl.pallas_call(
        functools.partial(_sb_sample_body, npages, t),
        grid_spec=grid_spec,
        out_shape=jax.ShapeDtypeStruct((b, rows, w), F32),
        compiler_params=_cp(("parallel",)), name="sb_sample",
    )(page_table, qbd, k_new, v_new, _tri2(PAGE, 1), *([cache_k] * npages), *([cache_v] * npages))
    return _diag_rows_out(o, t, SB_HEADS)


def _proj_res_body(n_in, *refs):
    h_ref = refs[0]
    x_refs = refs[1:1 + n_in]
    w_ref = refs[1 + n_in]
    o_ref = refs[2 + n_in]
    acc = h_ref[...]
    off = 0
    for x_ref in x_refs:
        k = x_ref.shape[1]
        acc = acc + _dot(x_ref[...].astype(BF16), w_ref[off:off + k, :])
        off += k
    o_ref[...] = acc


def _proj_res(h2, xs, w):
    m = h2.shape[0]
    tm = _row_tile(m)
    return pl.pallas_call(
        functools.partial(_proj_res_body, len(xs)),
        grid=(m // tm,),
        in_specs=[_row_spec(tm, D_MODEL)] + [_row_spec(tm, x.shape[1]) for x in xs] + [_const_spec(w.shape)],
        out_specs=_row_spec(tm, D_MODEL),
        out_shape=jax.ShapeDtypeStruct((m, D_MODEL), F32),
        compiler_params=_cp(("parallel",)), name="proj_res",
    )(h2, *xs, w)


def _norm_proj_body(scale, h_ref, g_ref, w_ref, o_ref):
    xn = _rms(h_ref[...], g_ref[...]).astype(BF16)
    o_ref[...] = (_dot(xn, w_ref[...]) * scale).astype(o_ref.dtype)


def _norm_proj(h2, g, w, scale):
    m = h2.shape[0]
    tm = _row_tile(m)
    return pl.pallas_call(
        functools.partial(_norm_proj_body, scale),
        grid=(m // tm,),
        in_specs=[_row_spec(tm, D_MODEL), _const_spec((1, D_MODEL)), _const_spec(w.shape)],
        out_specs=_row_spec(tm, w.shape[1]),
        out_shape=jax.ShapeDtypeStruct((m, w.shape[1]), BF16),
        compiler_params=_cp(("parallel",)), name="norm_proj",
    )(h2, g, w)


def _mem_kv_body(x_ref, wk0, wv0, wk1, wv1, *o_refs):
    x = x_ref[...].astype(BF16)
    for n, w_ref in enumerate((wk0, wv0, wk1, wv1)):
        y = _dot(x, w_ref[...])
        o_refs[2 * n][...] = y
        o_refs[2 * n + 1][...] = y.astype(BF16)


def _mem_kv(x2, ws):
    m = x2.shape[0]
    tm = _row_tile(m)
    outs, specs = [], []
    for _ in ws:
        outs += [jax.ShapeDtypeStruct((m, D_MODEL), F32), jax.ShapeDtypeStruct((m, D_MODEL), BF16)]
        specs += [_row_spec(tm, D_MODEL)] * 2
    return pl.pallas_call(
        _mem_kv_body, grid=(m // tm,),
        in_specs=[_row_spec(tm, D_MODEL)] + [_const_spec(w.shape) for w in ws],
        out_specs=specs, out_shape=outs,
        compiler_params=_cp(("parallel",)), name="mem_kv",
    )(x2, *ws)


def _mem_attn_body(q_ref, k_ref, v_ref, o_ref):
    for hh in range(MEM_HEADS):
        sl = slice(hh * MEM_HD, (hh + 1) * MEM_HD)
        s = _nt(q_ref[0, :, sl], k_ref[0, :, sl].astype(BF16))
        m = jnp.max(s, axis=-1, keepdims=True)
        e = jnp.exp(s - m)
        p = e / jnp.sum(e, axis=-1, keepdims=True)
        o_ref[0, :, sl] = _dot(p.astype(BF16), v_ref[0, :, sl].astype(BF16)).astype(o_ref.dtype)


def _mem_attn(q, mk, mv):
    b, t, d = q.shape
    tq = min(512, t)
    ml = mk.shape[1]
    kv_spec = pl.BlockSpec((1, ml, d), lambda bi, i: (bi, 0, 0))
    return pl.pallas_call(
        _mem_attn_body, grid=(b, t // tq),
        in_specs=[pl.BlockSpec((1, tq, d), lambda bi, i: (bi, i, 0)), kv_spec, kv_spec],
        out_specs=pl.BlockSpec((1, tq, d), lambda bi, i: (bi, i, 0)),
        out_shape=jax.ShapeDtypeStruct((b, t, d), BF16),
        compiler_params=_cp(("parallel", "parallel")), name="mem_attn",
    )(q, mk, mv)


FFN_CHUNK = 512


def _ffn_body(final, h_ref, g_ref, wg_ref, wu_ref, wd_ref, *rest):
    if final:
        gf_ref, o_ref = rest
    else:
        (o_ref,) = rest
    h = h_ref[...]
    xn = _rms(h, g_ref[...]).astype(BF16)
    acc = h
    for lo in range(0, D_FF, FFN_CHUNK):
        hi = min(lo + FFN_CHUNK, D_FF)
        gate = _dot(xn, wg_ref[:, lo:hi])
        up = _dot(xn, wu_ref[:, lo:hi])
        act = (gate * _sigmoid(gate) * up).astype(BF16)
        acc = acc + _dot(act, wd_ref[lo:hi, :])
    if final:
        acc = _rms(acc, gf_ref[...])
    o_ref[...] = acc


def _ffn(h2, g, wg, wu, wd, final_g=None):
    m = h2.shape[0]
    tm = _row_tile(m)
    final = final_g is not None
    once = lambda shape: pl.BlockSpec(shape, lambda i: (0, 0), pipeline_mode=pl.Buffered(1))
    in_specs = [_row_spec(tm, D_MODEL), _const_spec((1, D_MODEL)), once(wg.shape), once(wu.shape), once(wd.shape)]
    args = [h2, g, wg, wu, wd]
    if final:
        in_specs.append(_const_spec((1, D_MODEL)))
        args.append(final_g)
    return pl.pallas_call(
        functools.partial(_ffn_body, final), grid=(m // tm,),
        in_specs=in_specs, out_specs=_row_spec(tm, D_MODEL),
        out_shape=jax.ShapeDtypeStruct((m, D_MODEL), F32),
        compiler_params=_cp(("parallel",)), name="ffn",
    )(*args)


ODD_Q = NSA_HEADS * HEAD_DIM
ODD_GATES = 3 * NSA_HEADS
ODD_IN = ODD_Q + 6 * NSA_KV_DIM + ODD_GATES
GATE_PAD = 16
GATE_LANES = NSA_KVH * GATE_PAD
ODD_IN_PAD = ODD_Q + 6 * NSA_KV_DIM + 128


def _odd_in_weights(w_in_odd):
    gates = w_in_odd[:, ODD_Q + 6 * NSA_KV_DIM:].reshape(D_MODEL, NSA_KVH, 3 * NSA_GQA)
    gates = jnp.pad(gates, ((0, 0), (0, 0), (0, GATE_PAD - 3 * NSA_GQA))).reshape(D_MODEL, GATE_LANES)
    return jnp.concatenate([w_in_odd[:, :ODD_Q + 6 * NSA_KV_DIM],
                            jnp.pad(gates, ((0, 0), (0, 128 - GATE_LANES)))], axis=1).astype(BF16)


def _odd_in_body(head_major, h_ref, g_ref, w_ref, c_ref, s_ref, *o_refs):
    xn = _rms(h_ref[...], g_ref[...]).astype(BF16)
    c = c_ref[...]
    s = s_ref[...]

    def mm(lo, hi):
        return _dot(xn, w_ref[:, lo:hi])

    kv = lambda n: mm(ODD_Q + n * NSA_KV_DIM, ODD_Q + (n + 1) * NSA_KV_DIM)
    q = _rope_wide(mm(0, ODD_Q), c, s) * (HEAD_DIM ** -0.5)
    ck, cv = kv(0), kv(1)
    sk, sv = _rope_wide(kv(2), c, s), kv(3)
    wk, wv = _rope_wide(kv(4), c, s), kv(5)
    gates = _sigmoid(mm(ODD_Q + 6 * NSA_KV_DIM, ODD_IN_PAD))[:, :GATE_LANES]
    if head_major:
        ck_rows, cv_rows = o_refs[:2]
        t32 = o_refs[2:8]
        q_ref, g_out, sk_rows, sv_t, wk_rows, wv_t = o_refs[8:14]
        ck_rows[...] = ck
        cv_rows[...] = cv
        for val, r32, r16 in zip((ck, cv, sk, sv, wk, wv), t32, (None, None, None, sv_t, None, wv_t)):
            _store_transposed(val, NSA_KVH, r32, r16)
        _store_transposed(q, NSA_HEADS, None, q_ref)
        _store_head_major(sk, NSA_KVH, sk_rows)
        _store_head_major(wk, NSA_KVH, wk_rows)
        g_out[0] = gates.T.reshape(NSA_KVH, GATE_PAD, gates.shape[0])
    else:
        for ref, val in zip(o_refs[:6], (ck, cv, sk, sv, wk, wv)):
            ref[...] = val
        q_ref, g_out = o_refs[6:]
        q_ref[...] = q.astype(BF16)
        g_out[...] = gates


def _odd_in(h2, g, w, cos_t, sin_t, batch, head_major):
    m = h2.shape[0]
    t = m // batch
    tm = _row_tile(m)
    nt = max(t // tm, 1)
    ntab = cos_t.shape[0] // tm
    tab_spec = pl.BlockSpec((tm, 128), lambda i: (i % ntab, 0))
    kv = jax.ShapeDtypeStruct((m, NSA_KV_DIM), F32)
    if head_major:
        kt = lambda dt: jax.ShapeDtypeStruct((batch, NSA_KVH, HEAD_DIM, t), dt)
        rows16 = jax.ShapeDtypeStruct((batch, NSA_KVH, t, HEAD_DIM), BF16)
        outs = [kv] * 2 + [kt(F32)] * 6
        outs += [jax.ShapeDtypeStruct((batch, NSA_HEADS, HEAD_DIM, t), BF16),
                 jax.ShapeDtypeStruct((batch, NSA_KVH, GATE_PAD, t), F32), rows16, kt(BF16), rows16, kt(BF16)]
        rows_spec = _hm_spec(NSA_KVH, tm, HEAD_DIM, nt)
        specs = [_row_spec(tm, NSA_KV_DIM)] * 2 + [_tr_spec(NSA_KVH, tm, nt)] * 6
        specs += [_tr_spec(NSA_HEADS, tm, nt),
                  pl.BlockSpec((1, NSA_KVH, GATE_PAD, tm), lambda i: (i // nt, 0, 0, i % nt)),
                  rows_spec, _tr_spec(NSA_KVH, tm, nt), rows_spec, _tr_spec(NSA_KVH, tm, nt)]
    else:
        outs = [kv] * 6 + [jax.ShapeDtypeStruct((m, ODD_Q), BF16), jax.ShapeDtypeStruct((m, GATE_LANES), F32)]
        specs = [_row_spec(tm, NSA_KV_DIM)] * 6 + [_row_spec(tm, ODD_Q), _row_spec(tm, GATE_LANES)]
    return pl.pallas_call(
        functools.partial(_odd_in_body, head_major), grid=(m // tm,),
        in_specs=[_row_spec(tm, D_MODEL), _const_spec((1, D_MODEL)), _const_spec(w.shape), tab_spec, tab_spec],
        out_specs=specs, out_shape=outs,
        compiler_params=_cp(("parallel",)), name="odd_in",
    )(h2, g, w, cos_t, sin_t)


def _gelu_tanh(x):
    return 0.5 * x * (1.0 + jnp.tanh(math.sqrt(2.0 / math.pi) * (x + 0.044715 * x * x * x)))


def _compress_body(add_pe, use_rope, x_ref, pe_ref, w1_ref, w2_ref, c_ref, s_ref, o_ref, acc_ref, wbd_ref):
    l = pl.program_id(0)

    @pl.when(l == 0)
    def _():
        acc_ref[...] = jnp.zeros_like(acc_ref)
        wbd_ref[...] = jnp.zeros_like(wbd_ref)

    for gg in range(NSA_KVH):
        wbd_ref[gg * HEAD_DIM:(gg + 1) * HEAD_DIM, gg * CMP_HID:(gg + 1) * CMP_HID] = w1_ref[0]
    x = x_ref[...]
    if add_pe:
        x = (x + pe_ref[0]).astype(BF16)
    acc_ref[...] += _dot(x, wbd_ref[...])

    @pl.when(l == CMP_BLK - 1)
    def _():
        hid = _gelu_tanh(acc_ref[...]).astype(BF16)
        y = _dot(hid, w2_ref[...])
        if use_rope:
            y = _rope_wide(y, c_ref[...], s_ref[...])
        o_ref[...] = y


def _compress(x2, pe, w1, w2, tables, add_pe):
    rows = x2.shape[0]
    pe_t = jnp.tile(pe, (1, NSA_KVH))[:, None, :]
    w2bd = jnp.kron(jnp.eye(NSA_KVH, dtype=F32), w2).astype(BF16)
    use_rope = tables is not None
    if use_rope:
        c_t, s_t = tables
    else:
        c_t = s_t = jnp.zeros((rows, 128), F32)
    return pl.pallas_call(
        functools.partial(_compress_body, add_pe, use_rope), grid=(CMP_BLK,),
        in_specs=[pl.BlockSpec((rows, NSA_KV_DIM), lambda l: (0, l)),
                  pl.BlockSpec((1, 1, NSA_KV_DIM), lambda l: (l, 0, 0)),
                  pl.BlockSpec((1, HEAD_DIM, CMP_HID), lambda l: (l, 0, 0)),
                  _const_spec(w2bd.shape), _const_spec((rows, 128)), _const_spec((rows, 128))],
        out_specs=_const_spec((rows, NSA_KV_DIM)),
        out_shape=jax.ShapeDtypeStruct((rows, NSA_KV_DIM), F32),
        scratch_shapes=[pltpu.VMEM((rows, NSA_KVH * CMP_HID), F32),
                        pltpu.VMEM((NSA_KV_DIM, NSA_KVH * CMP_HID), BF16)],
        compiler_params=_cp(("arbitrary",)), name="compress",
    )(x2, pe_t, w1.astype(BF16), w2bd, c_t, s_t)


NB_PAD = 128
RANK_ROWS = 64


def _cmp_and_select(q, kc, vc, qpos, rows_per_q, nb):
    tq = qpos.shape[0]
    s = _nt(q, kc).reshape(rows_per_q, tq, NB_PAD)
    n_idx = lax.broadcasted_iota(jnp.int32, (tq, NB_PAD), 1)
    valid = (n_idx * CMP_BLK + (CMP_BLK - 1)) <= qpos
    sm = jnp.where(valid[None], s, NEG_INF)
    m = jnp.max(sm, axis=-1, keepdims=True)
    p = jnp.where(valid[None], jnp.exp(sm - m), 0.0)
    p = p / jnp.maximum(jnp.sum(p, axis=-1, keepdims=True), 1e-30)
    o_cmp = _dot(p.reshape(rows_per_q * tq, NB_PAD).astype(BF16), vc)
    cur = qpos // CMP_BLK
    forced = (n_idx == 0) | (n_idx == cur) | (n_idx == cur - 1)
    score = jnp.where(n_idx > cur, -1.0, jnp.where(forced, FORCE_SCORE, jnp.sum(p, axis=0)))
    tqp = max(tq, 128)
    if tqp > tq:
        score = jnp.concatenate([score, jnp.zeros((tqp - tq, NB_PAD), F32)], axis=0)
    sel_t = _top_n_mask(score.T[:RANK_ROWS], nb)
    sel = jnp.concatenate([sel_t, jnp.zeros((NB_PAD - RANK_ROWS, tqp), F32)], axis=0).T
    return o_cmp, sel[:tq]


def _top_n_mask(st, nb):
    groups = RANK_ROWS // 8
    blocks = [st[8 * g:8 * g + 8] for g in range(groups)]
    cnt = [jnp.zeros_like(b) for b in blocks]
    row = lax.broadcasted_iota(jnp.int32, blocks[0].shape, 0)
    for i in range(nb):
        r = st[i:i + 1]
        for g in range(groups):
            gt = jnp.where(r > blocks[g], 1.0, 0.0)
            if 8 * g + 7 <= i:
                inc = gt
            else:
                ge = jnp.where(r >= blocks[g], 1.0, 0.0)
                inc = ge if 8 * g > i else jnp.where(row + 8 * g > i, ge, gt)
            cnt[g] = cnt[g] + inc
    return jnp.concatenate([jnp.where(c < float(SEL_TOPN), 1.0, 0.0) for c in cnt], axis=0)


def _osm(z, st, v, keys_axis, v_keys_minor):
    m, l, acc = st
    m_new = jnp.maximum(m, jnp.max(z, axis=keys_axis, keepdims=True))
    alpha = jnp.exp(m - m_new)
    p = jnp.exp(z - m_new)
    l = alpha * l + jnp.sum(p, axis=keys_axis, keepdims=True)
    pb = p.astype(BF16)
    if keys_axis == 1:
        pv = _nt(pb, v) if v_keys_minor else _dot(pb, v)
    else:
        pv = _dot(v, pb)
    return m_new, l, alpha * acc + pv


def _softmax_tiles(zs, vs):
    m = functools.reduce(jnp.maximum, [jnp.max(z, axis=1, keepdims=True) for z in zs])
    ps = [jnp.exp(z - m) for z in zs]
    l = functools.reduce(lambda a, b: a + b, [jnp.sum(p, axis=1, keepdims=True) for p in ps])
    acc = None
    for p, (v, keys_minor) in zip(ps, vs):
        pv = _nt(p.astype(BF16), v) if keys_minor else _dot(p.astype(BF16), v)
        acc = pv if acc is None else acc + pv
    return acc / jnp.maximum(l, 1e-30)


def _osm_groups_t(q_ts, k_rows, v_t, bias, sts):
    n = len(q_ts)
    z_next = _dot(k_rows, q_ts[0]) + bias
    out = []
    for r in range(n):
        z = z_next
        if r + 1 < n:
            z_next = _dot(k_rows, q_ts[r + 1]) + bias
        out.append(_osm(z, sts[r], v_t, 0, True))
    return tuple(out)


def _osm_init(rows, width, keys_axis=1):
    if keys_axis == 1:
        return (jnp.full((rows, 1), NEG_INF, F32), jnp.zeros((rows, 1), F32), jnp.zeros((rows, width), F32))
    return (jnp.full((1, rows), NEG_INF, F32), jnp.zeros((1, rows), F32), jnp.zeros((width, rows), F32))


def _osm_out(st):
    return st[2] / jnp.maximum(st[1], 1e-30)


def _nsa_prompt_body(nb, q_ref, kc_ref, vc_ref, sk_ref, sv_ref, wk_ref, wv_ref, g_ref, e_ref, o_ref):
    i = pl.program_id(2)
    tq = q_ref.shape[3]
    tk = tq
    rq = NSA_GQA
    q_ts = [q_ref[0, r] for r in range(rq)]
    qpos = i * tq + lax.broadcasted_iota(jnp.int32, (1, tq), 1)

    n_idx = lax.broadcasted_iota(jnp.int32, (NB_PAD, tq), 0)
    valid = (n_idx * CMP_BLK + (CMP_BLK - 1)) <= qpos
    kc = kc_ref[0, 0]
    vc_t = vc_ref[0, 0]
    o_cmp, score = [], None
    for r in range(rq):
        sm = jnp.where(valid, _dot(kc, q_ts[r]), NEG_INF)
        m = jnp.max(sm, axis=0, keepdims=True)
        p = jnp.where(valid, jnp.exp(sm - m), 0.0)
        p = p / jnp.maximum(jnp.sum(p, axis=0, keepdims=True), 1e-30)
        o_cmp.append(_dot(vc_t, p.astype(BF16)))
        score = p if score is None else score + p
    cur = qpos // CMP_BLK
    forced = (n_idx == 0) | (n_idx == cur) | (n_idx == cur - 1)
    score = jnp.where(n_idx > cur, -1.0, jnp.where(forced, FORCE_SCORE, score))
    sel_t = _top_n_mask(score[:RANK_ROWS], nb)
    selb = jnp.concatenate([sel_t, jnp.zeros((NB_PAD - RANK_ROWS, tq), F32)], axis=0).astype(BF16)

    key = lax.broadcasted_iota(jnp.int32, (tk, tq), 0)
    qry = lax.broadcasted_iota(jnp.int32, (tk, tq), 1)
    init = tuple(_osm_init(tq, HEAD_DIM, 0) for _ in range(rq))

    def sel_tile(j, sts, diag):
        start = pl.multiple_of(j * tk, tk)
        chosen = _dot(e_ref[pl.ds(start, tk), :], selb)
        bias = (chosen - 1.0) * 1e30
        if diag:
            bias = jnp.where(key <= qry, bias, NEG_INF)
        return _osm_groups_t(q_ts, sk_ref[0, 0, pl.ds(start, tk), :], sv_ref[0, 0, :, pl.ds(start, tk)], bias, sts)

    sts = lax.fori_loop(0, i, lambda j, s: sel_tile(j, s, False), init)
    o_sel = [_osm_out(st) for st in sel_tile(i, sts, True)]

    def win_tile(back, sts):
        j_raw = i - back
        start = pl.multiple_of(jnp.maximum(j_raw, 0) * tk, tk)
        if back == 0:
            ok = key <= qry
        elif back < WINDOW // tk:
            ok = key >= 0
        else:
            ok = key > qry
        bias = jnp.where(ok & (j_raw >= 0), 0.0, NEG_INF)
        return _osm_groups_t(q_ts, wk_ref[0, 0, pl.ds(start, tk), :], wv_ref[0, 0, :, pl.ds(start, tk)], bias, sts)

    sts = init
    for back in range(WINDOW // tk + 1):
        sts = win_tile(back, sts)
    o_win = [_osm_out(st) for st in sts]

    gates = g_ref[0, 0]
    outs = []
    for r in range(rq):
        gc, gs, gw = (gates[3 * r + n:3 * r + n + 1, :] for n in range(3))
        outs.append(gc * o_cmp[r] + gs * o_sel[r] + gw * o_win[r])
    o_ref[0] = jnp.concatenate(outs, axis=0).T.astype(o_ref.dtype)


def _nsa_prompt(q_t, kc_hm, vc_t, sk_hm, sv_t, wk_hm, wv_t, gates_t, nb):
    b, _, d, t = q_t.shape
    tq = min(NSA_TILE, t)
    assert WINDOW % tq == 0 and nb <= RANK_ROWS and tq % 128 == 0
    expand = ((jnp.arange(t)[:, None] // CMP_BLK) == jnp.arange(NB_PAD)[None, :]).astype(BF16)
    per_group = lambda shape: pl.BlockSpec((1, 1) + shape, lambda bi, g, i: (bi, g, 0, 0))
    return pl.pallas_call(
        functools.partial(_nsa_prompt_body, nb),
        grid=(b, NSA_KVH, t // tq),
        in_specs=[pl.BlockSpec((1, NSA_GQA, d, tq), lambda bi, g, i: (bi, g, 0, i)),
                  per_group((NB_PAD, d)), per_group((d, NB_PAD)),
                  per_group((t, d)), per_group((d, t)), per_group((t, d)), per_group((d, t)),
                  pl.BlockSpec((1, 1, GATE_PAD, tq), lambda bi, g, i: (bi, g, 0, i)),
                  _const_spec((t, NB_PAD))],
        out_specs=pl.BlockSpec((1, tq, NSA_GQA * d), lambda bi, g, i: (bi, i, g)),
        out_shape=jax.ShapeDtypeStruct((b, t, NSA_HEADS * d), BF16),
        compiler_params=_cp(("parallel", "parallel", "arbitrary")), name="nsa_prompt",
    )(q_t, kc_hm, vc_t, sk_hm, sv_t, wk_hm, wv_t, gates_t, expand)


def _cmp_gather_body(npages, t_new, pt_ref, new_ref, pe_ref, *refs):
    page_refs = refs[:npages]
    o_ref = refs[npages]
    pe = pe_ref[...]
    pe2 = jnp.concatenate([pe, pe], axis=0)
    for p in range(npages):
        o_ref[0, p * PAGE:(p + 1) * PAGE, :] = (page_refs[p][0] + pe2).astype(BF16)
    tail = jnp.concatenate([new_ref[0], jnp.zeros((CMP_BLK - t_new, NSA_KV_DIM), F32)], axis=0) + pe
    o_ref[0, npages * PAGE:npages * PAGE + CMP_BLK, :] = tail.astype(BF16)


def _cmp_gather(cache, new, pe, page_table):
    b, t, w = new.shape
    npages = page_table.shape[1]
    rows = npages * PAGE + CMP_BLK
    grid_spec = pltpu.PrefetchScalarGridSpec(
        num_scalar_prefetch=1, grid=(b,),
        in_specs=[pl.BlockSpec((1, t, w), lambda bi, pt: (bi, 0, 0)),
                  pl.BlockSpec((CMP_BLK, w), lambda bi, pt: (0, 0))]
                 + [pl.BlockSpec((1, PAGE, w), lambda bi, pt, p=p: (pt[bi, p], 0, 0)) for p in range(npages)],
        out_specs=pl.BlockSpec((1, rows, w), lambda bi, pt: (bi, 0, 0)))
    return pl.pallas_call(
        functools.partial(_cmp_gather_body, npages, t), grid_spec=grid_spec,
        out_shape=jax.ShapeDtypeStruct((b, rows, w), BF16),
        compiler_params=_cp(("parallel",)), name="cmp_gather",
    )(page_table, new, jnp.tile(pe, (1, NSA_KVH)), *([cache] * npages))


def _nsa_sample_body(npages, t_new, past, nb, pt_ref, qbd_ref, kc_ref, vc_ref, g_ref,
                     skn_ref, svn_ref, wkn_ref, wvn_ref, wkc_ref, wvc_ref, *refs):
    sk_refs = refs[:npages]
    sv_refs = refs[npages:2 * npages]
    o_ref = refs[2 * npages]
    qbd = qbd_ref[0]
    rows = qbd.shape[0]
    w = qbd.shape[1]
    rq = rows // t_new
    t_idx = lax.broadcasted_iota(jnp.int32, (t_new, 1), 0)
    qpos = past + t_idx

    o_cmp_parts, sel_parts = [], []
    per = NSA_GQA * t_new
    for gg in range(NSA_KVH):
        oc, sel = _cmp_and_select(qbd[gg * per:(gg + 1) * per], kc_ref[0], vc_ref[0], qpos, NSA_GQA, nb)
        o_cmp_parts.append(oc)
        sel_parts.append(jnp.broadcast_to(sel[None], (NSA_GQA, t_new, NB_PAD)).reshape(per, NB_PAD))
    o_cmp = jnp.concatenate(o_cmp_parts, axis=0)
    sel_bias = (jnp.concatenate(sel_parts, axis=0) - 1.0) * 1e30

    qoff = lax.broadcasted_iota(jnp.int32, (rows, PAGE), 0) % t_new
    col = lax.broadcasted_iota(jnp.int32, (rows, PAGE), 1)
    pad = jnp.zeros((PAGE - t_new, w), F32)
    new_bias = jnp.where(col <= qoff, 0.0, NEG_INF)

    def padded(ref):
        return jnp.concatenate([ref[0], pad], axis=0).astype(BF16)

    half = col < CMP_BLK
    zs, vs = [], []
    for p in range(npages):
        b0 = sel_bias[:, 2 * p:2 * p + 1]
        b1 = sel_bias[:, 2 * p + 1:2 * p + 2]
        zs.append(_dot(qbd, sk_refs[p][0].reshape(w, PAGE).astype(BF16)) + jnp.where(half, b0, b1))
        vs.append((sv_refs[p][0].reshape(w, PAGE).astype(BF16), True))
    last = (past // CMP_BLK)
    zs.append(_nt(qbd, padded(skn_ref)) + new_bias + sel_bias[:, last:last + 1])
    vs.append((padded(svn_ref), False))
    o_sel = _softmax_tiles(zs, vs)

    zs, vs = [_nt(qbd, padded(wkn_ref)) + new_bias], [(padded(wvn_ref), False)]
    wb = wkc_ref.shape[3]
    for c in range(wb // PAGE):
        kpos = (past - wb) + c * PAGE + col
        diff = (past + qoff) - kpos
        bias = jnp.where((diff < WINDOW) & (kpos >= 0), 0.0, NEG_INF)
        chunk = lambda ref: ref[0, :, :, c * PAGE:(c + 1) * PAGE].reshape(w, PAGE).astype(BF16)
        zs.append(_dot(qbd, chunk(wkc_ref)) + bias)
        vs.append((chunk(wvc_ref), True))
    o_win = _softmax_tiles(zs, vs)

    gates = g_ref[0]
    o_ref[0] = gates[:, 0:1] * o_cmp + gates[:, 1:2] * o_sel + gates[:, 2:3] * o_win


def _nsa_sample(q, kc, vc, gates, sk_new, sv_new, wk_new, wv_new, win_k, win_v,
                cache_sk, cache_sv, page_table, past, nb):
    b, t, w = sk_new.shape
    npages = page_table.shape[1]
    assert nb <= RANK_ROWS and (past // CMP_BLK) < nb
    qbd = _block_diag_rows(q.reshape(b, t, NSA_HEADS, HEAD_DIM), NSA_KVH)
    rows = NSA_HEADS * t
    g_rows = jnp.transpose(gates.reshape(b, t, NSA_KVH, NSA_GQA, 3), (0, 2, 3, 1, 4)).reshape(b, rows, 3)
    bspec = lambda r, c: pl.BlockSpec((1, r, c), lambda bi, pt: (bi, 0, 0))
    wb = win_k.shape[3]
    win_spec = pl.BlockSpec((1, NSA_KVH, HEAD_DIM, wb), lambda bi, pt: (bi, 0, 0, 0))
    grid_spec = pltpu.PrefetchScalarGridSpec(
        num_scalar_prefetch=1, grid=(b,),
        in_specs=[bspec(rows, w), bspec(NB_PAD, w), bspec(NB_PAD, w), bspec(rows, 3),
                  bspec(t, w), bspec(t, w), bspec(t, w), bspec(t, w), win_spec, win_spec]
                 + _page_specs(npages, NSA_KVH) + _page_specs(npages, NSA_KVH),
        out_specs=bspec(rows, w))
    o = pl.pallas_call(
        functools.partial(_nsa_sample_body, npages, t, past, nb), grid_spec=grid_spec,
        out_shape=jax.ShapeDtypeStruct((b, rows, w), F32),
        compiler_params=_cp(("parallel",)), name="nsa_sample",
    )(page_table, qbd, kc, vc, g_rows, sk_new, sv_new, wk_new, wv_new, win_k, win_v,
      *([cache_sk] * npages), *([cache_sv] * npages))
    return _diag_rows_out(o, t, NSA_KVH)


def _common_tail(h2, batch, l, mem_k, mem_v, p, final_g):
    t = h2.shape[0] // batch
    q = _norm_proj(h2, p["norm_mem"][l][None], p["w_mem_q"][l], MEM_HD ** -0.5)
    o = _mem_attn(q.reshape(batch, t, D_MODEL), mem_k, mem_v)
    h2 = _proj_res(h2, [o.reshape(batch * t, D_MODEL)], p["w_mem_o"][l])
    return _ffn(h2, p["norm_ffn"][l][None], p["w_ffn_gate"][l], p["w_ffn_up"][l], p["w_ffn_down"][l], final_g)


def _even_layer(h2, batch, p, conv_buf, sample_ctx):
    t = h2.shape[0] // batch
    prompt = sample_ctx is None
    res = _even_in(h2, p["norm_mix"][0][None], p["w_in_even"][0], batch, prompt)
    glu3 = res[0].reshape(batch, t, CONV_CH)
    y_conv = _conformer_conv(glu3, conv_buf, p["conv_w"][0], p["conv_b"][0], p["conv_ln_g"][0], p["conv_ln_b"][0])
    new_buf = jnp.concatenate([conv_buf.astype(F32), glu3], axis=1)[:, -(CONV_W - 1):]
    shp = (batch, t, SB_HEADS, HEAD_DIM)
    if prompt:
        o = _sb_prompt(res[3], res[4], res[5])
        k_out, v_out = _from_transposed(res[1]), _from_transposed(res[2])
    else:
        cache_k, cache_v, page_table = sample_ctx
        k32, v32 = res[1], res[2]
        o = _sb_sample(res[3].reshape(batch, t, SB_DIM), k32.reshape(batch, t, SB_DIM),
                       v32.reshape(batch, t, SB_DIM), cache_k, cache_v, page_table)
        k_out, v_out = k32.reshape(shp), v32.reshape(shp)
    h2 = _proj_res(h2, [y_conv.reshape(batch * t, CONV_CH), o.reshape(batch * t, SB_DIM)], p["w_mix_out"][0])
    return h2, new_buf, k_out, v_out


def _odd_layer(h2, batch, p, sample_ctx):
    t = h2.shape[0] // batch
    prompt = sample_ctx is None
    past = 0 if prompt else sample_ctx["past"]
    tm = _row_tile(h2.shape[0])
    pos = past + jnp.arange(max(t, tm), dtype=jnp.int32) % t
    cos_t, sin_t = _rope_tables(pos, 128)
    res = _odd_in(h2, p["norm_mix"][1][None], p["w_in_odd"], cos_t, sin_t, batch, prompt)
    ck, cv = res[0], res[1]
    nb = -(-(past + t) // CMP_BLK)
    blk_end = jnp.arange(nb, dtype=jnp.int32) * CMP_BLK + (CMP_BLK - 1)
    end_tabs = tuple(jnp.tile(x, (batch, 1)) for x in _rope_tables(blk_end, 128))
    cw = (p["cmp_pe_k"][0], p["cmp_w1_k"][0], p["cmp_w2_k"][0]), (p["cmp_pe_v"][0], p["cmp_w1_v"][0], p["cmp_w2_v"][0])
    if prompt:
        xk = ck.reshape(batch * nb, CMP_BLK * NSA_KV_DIM)
        xv = cv.reshape(batch * nb, CMP_BLK * NSA_KV_DIM)
        add_pe = True
    else:
        pt = sample_ctx["page_table"]
        xk = _cmp_gather(sample_ctx["cmp_k"], ck.reshape(batch, t, NSA_KV_DIM), cw[0][0], pt)
        xv = _cmp_gather(sample_ctx["cmp_v"], cv.reshape(batch, t, NSA_KV_DIM), cw[1][0], pt)
        xk = xk.reshape(batch * nb, CMP_BLK * NSA_KV_DIM)
        xv = xv.reshape(batch * nb, CMP_BLK * NSA_KV_DIM)
        add_pe = False
    kc = _compress(xk, *cw[0], end_tabs, add_pe).reshape(batch, nb, NSA_KV_DIM)
    vc = _compress(xv, *cw[1], None, add_pe).reshape(batch, nb, NSA_KV_DIM)
    padc = lambda x: jnp.pad(x, ((0, 0), (0, NB_PAD - nb), (0, 0))).astype(BF16)
    kvshape = (batch, t, NSA_KVH, HEAD_DIM)
    if prompt:
        hm = lambda x: jnp.transpose(padc(x).reshape(batch, NB_PAD, NSA_KVH, HEAD_DIM), (0, 2, 1, 3))
        t32 = res[2:8]
        q_t, gates_t, sk_hm, sv_t, wk_hm, wv_t = res[8:]
        o = _nsa_prompt(q_t, hm(kc), jnp.swapaxes(hm(vc), 2, 3), sk_hm, sv_t, wk_hm, wv_t, gates_t, nb)
        keep = min(WINDOW, t)
        outs = [_from_transposed(x) for x in t32[:4]] + [_from_transposed(x[..., -keep:]) for x in t32[4:]]
    else:
        sk, sv, wk, wv = res[2:6]
        q, gates = res[6:]
        gates = gates.reshape(batch, t, NSA_KVH, GATE_PAD)[..., :3 * NSA_GQA]
        r3 = lambda x: x.reshape(batch, t, NSA_KV_DIM)
        wkc, wvc = sample_ctx["win_k"], sample_ctx["win_v"]
        wb = wkc.shape[1]
        o = _nsa_sample(q.reshape(batch, t, ODD_Q), padc(kc), padc(vc), gates,
                        r3(sk), r3(sv), r3(wk), r3(wv), _keys_minor(wkc), _keys_minor(wvc),
                        sample_ctx["sel_k"], sample_ctx["sel_v"], sample_ctx["page_table"], past, nb)
        r4 = lambda x: x.reshape(kvshape)
        outs = [r4(ck), r4(cv), r4(sk), r4(sv),
                jnp.concatenate([wkc, r4(wk)], axis=1)[:, -wb:], jnp.concatenate([wvc, r4(wv)], axis=1)[:, -wb:]]
    h2 = _proj_res(h2, [o.reshape(batch * t, ODD_Q)], p["w_mix_out"][1])
    return (h2,) + tuple(outs)


def _trunk(x, p, conv_buf, mem_kv, even_ctx, odd_ctx):
    batch, t, _ = x.shape
    h2 = x.reshape(batch * t, D_MODEL)
    h2, new_buf, sbk, sbv = _even_layer(h2, batch, p, conv_buf, even_ctx)
    h2 = _common_tail(h2, batch, 0, mem_kv[0][0], mem_kv[0][1], p, None)
    h2, ck, cv, sk, sv, wk, wv = _odd_layer(h2, batch, p, odd_ctx)
    h2 = _common_tail(h2, batch, 1, mem_kv[1][0], mem_kv[1][1], p, p["final_norm"][None])
    st = lambda a: a[None]
    return (h2.reshape(batch, t, D_MODEL), st(sbk), st(sbv), st(new_buf),
            st(ck), st(cv), st(sk), st(sv), st(wk), st(wv))


def kernel(x_prompt, x_sample, mem_prompt, cache_sb_k, cache_sb_v, state_conv,
           cache_nsa_cmp_k, cache_nsa_cmp_v, cache_nsa_sel_k, cache_nsa_sel_v,
           cache_nsa_win_k, cache_nsa_win_v, cache_mem_k, cache_mem_v, page_table,
           norm_mix, norm_mem, norm_ffn, final_norm, w_in_even, w_in_odd, w_mix_out,
           conv_w, conv_b, conv_ln_g, conv_ln_b,
           cmp_pe_k, cmp_w1_k, cmp_w2_k, cmp_pe_v, cmp_w1_v, cmp_w2_v,
           w_mem_q, w_mem_k, w_mem_v, w_mem_o, w_ffn_gate, w_ffn_up, w_ffn_down):
    assert norm_mix.shape[0] == 2 and w_in_even.shape[0] == 1 and w_in_odd.shape[0] == 1
    bp = x_prompt.shape[0]
    bs = x_sample.shape[0]
    bf = lambda w: w.astype(BF16)
    p = dict(
        norm_mix=norm_mix, norm_mem=norm_mem, norm_ffn=norm_ffn, final_norm=final_norm,
        w_in_even=bf(w_in_even),
        w_in_odd=_odd_in_weights(w_in_odd[0]),
        w_mix_out=bf(w_mix_out), conv_w=conv_w, conv_b=conv_b, conv_ln_g=conv_ln_g, conv_ln_b=conv_ln_b,
        cmp_pe_k=cmp_pe_k, cmp_w1_k=cmp_w1_k, cmp_w2_k=cmp_w2_k,
        cmp_pe_v=cmp_pe_v, cmp_w1_v=cmp_w1_v, cmp_w2_v=cmp_w2_v,
        w_mem_q=bf(w_mem_q), w_mem_o=bf(w_mem_o),
        w_ffn_gate=bf(w_ffn_gate), w_ffn_up=bf(w_ffn_up), w_ffn_down=bf(w_ffn_down))

    ml = mem_prompt.shape[1]
    mem = _mem_kv(mem_prompt.reshape(bp * ml, D_MODEL),
                  [bf(w_mem_k[0]), bf(w_mem_v[0]), bf(w_mem_k[1]), bf(w_mem_v[1])])
    m3 = lambda a: a.reshape(bp, ml, D_MODEL)
    m4 = lambda a: a.reshape(bp, ml, MEM_HEADS, MEM_HD)
    mem_kv_p = [(m3(mem[1]), m3(mem[3])), (m3(mem[5]), m3(mem[7]))]
    mem_k_p = jnp.stack([m4(mem[0]), m4(mem[4])])
    mem_v_p = jnp.stack([m4(mem[2]), m4(mem[6])])
    zero_buf = jnp.zeros((bp, CONV_W - 1, CONV_CH), F32)
    outs_p = _trunk(x_prompt, p, zero_buf, mem_kv_p, None, None)

    n_pool = cache_sb_k.shape[1]
    past = page_table.shape[1] * PAGE
    flat = lambda c, w: c.reshape(n_pool, PAGE, w)
    even_ctx = (_keys_minor(cache_sb_k[0]), _keys_minor(cache_sb_v[0]), page_table)
    odd_ctx = dict(past=past, page_table=page_table,
                   cmp_k=flat(cache_nsa_cmp_k[0], NSA_KV_DIM), cmp_v=flat(cache_nsa_cmp_v[0], NSA_KV_DIM),
                   sel_k=_keys_minor(cache_nsa_sel_k[0]), sel_v=_keys_minor(cache_nsa_sel_v[0]),
                   win_k=cache_nsa_win_k[0], win_v=cache_nsa_win_v[0])
    c3 = lambda a: a.reshape(bs, a.shape[1], D_MODEL)
    mem_kv_s = [(c3(cache_mem_k[0]), c3(cache_mem_v[0])), (c3(cache_mem_k[1]), c3(cache_mem_v[1]))]
    outs_s = _trunk(x_sample, p, state_conv[0], mem_kv_s, even_ctx, odd_ctx)

    return (outs_p[0], outs_s[0]) + tuple(outs_p[1:]) + (mem_k_p, mem_v_p) + tuple(outs_s[1:])
```

```python
import functools
import math

import jax
import jax.numpy as jnp
from jax import lax
from jax.experimental import pallas as pl
from jax.experimental.pallas import tpu as pltpu

F32 = jnp.float32
BF16 = jnp.bfloat16

D_MODEL = 1024
HEAD_DIM = 64
CONV_CH = 512
CONV_W = 31
SB_HEADS = 8
SB_DIM = 512
NSA_HEADS = 16
NSA_GQA = 4
NSA_KVH = 4
NSA_KV_DIM = 256
CMP_BLK = 64
CMP_HID = 256
SEL_TOPN = 16
FORCE_SCORE = 1.0e4
WINDOW = 512
MEM_HEADS = 4
MEM_HD = 256
D_FF = 2816
ROPE_THETA = 10000.0
NORM_EPS = 1e-6
NEG_INF = -1e30
PAGE = 128

V7X_VMEM_BYTES = 64 * 1024 * 1024
VMEM_LIMIT = V7X_VMEM_BYTES - 8 * 1024 * 1024
ATT_TILE = 256
NSA_TILE = 512


def _cp(sem):
    return pltpu.CompilerParams(dimension_semantics=sem, vmem_limit_bytes=VMEM_LIMIT)


def _nt(a, b):
    return lax.dot_general(a, b, (((1,), (1,)), ((), ())), preferred_element_type=F32)


def _dot(a, b):
    return jnp.dot(a, b, preferred_element_type=F32)


def _rms(x, g):
    y = x * lax.rsqrt(jnp.mean(x * x, axis=-1, keepdims=True) + NORM_EPS)
    return y * g


def _sigmoid(x):
    return 1.0 / (1.0 + jnp.exp(-x))


def _const_spec(shape):
    n = len(shape)
    return pl.BlockSpec(shape, lambda *a: (0,) * n)


def _row_spec(tm, n):
    return pl.BlockSpec((tm, n), lambda i: (i, 0))


def _hm_spec(heads, tm, width, nt):
    return pl.BlockSpec((1, heads, tm, width), lambda i: (i // nt, 0, i % nt, 0))


def _tr_spec(heads, tm, nt):
    return pl.BlockSpec((1, heads, HEAD_DIM, tm), lambda i: (i // nt, 0, 0, i % nt))


def _store_transposed(x, heads, f32_ref, bf16_ref):
    xt = x.T.reshape(heads, HEAD_DIM, x.shape[0])
    if f32_ref is not None:
        f32_ref[0] = xt
    if bf16_ref is not None:
        bf16_ref[0] = xt.astype(BF16)


def _store_head_major(x, heads, bf16_ref):
    for hh in range(heads):
        bf16_ref[0, hh] = x[:, hh * HEAD_DIM:(hh + 1) * HEAD_DIM].astype(BF16)


def _from_transposed(xt):
    return jnp.transpose(xt, (0, 3, 1, 2))


def _row_tile(m):
    return 512 if m >= 4096 else min(256, m)


def _rope_tables(pos, width):
    half = HEAD_DIM // 2
    inv = ROPE_THETA ** (-jnp.arange(half, dtype=F32) / half)
    ang = pos.astype(F32)[:, None] * inv[None, :]
    cos = jnp.cos(ang)
    sin = jnp.sin(ang)
    c = jnp.concatenate([cos, cos], axis=-1)
    s = jnp.concatenate([-sin, sin], axis=-1)
    reps = width // HEAD_DIM
    return jnp.tile(c, (1, reps)), jnp.tile(s, (1, reps))


def _rope128(x, c, s):
    lane = lax.broadcasted_iota(jnp.int32, x.shape, 1)
    first = (lane % HEAD_DIM) < (HEAD_DIM // 2)
    rot = jnp.where(first, pltpu.roll(x, 96, 1), pltpu.roll(x, 32, 1))
    return x * c + rot * s


def _rope_wide(x, c, s):
    n = x.shape[1] // 128
    return jnp.concatenate([_rope128(x[:, i * 128:(i + 1) * 128], c, s) for i in range(n)], axis=1)


def _even_in_body(head_major, h_ref, g_ref, w_ref, glu_ref, k32_ref, v32_ref, q_ref, *hm_refs):
    xn = _rms(h_ref[...], g_ref[...]).astype(BF16)

    def mm(lo, hi):
        return _dot(xn, w_ref[:, lo:hi])

    glu_ref[...] = mm(0, CONV_CH) * _sigmoid(mm(CONV_CH, 2 * CONV_CH))
    base = 2 * CONV_CH
    q = mm(base, base + SB_DIM) * (HEAD_DIM ** -0.5)
    k = mm(base + SB_DIM, base + 2 * SB_DIM)
    v = mm(base + 2 * SB_DIM, base + 3 * SB_DIM)
    if head_major:
        kh_ref, vh_ref = hm_refs
        _store_transposed(q, SB_HEADS, None, q_ref)
        _store_transposed(k, SB_HEADS, k32_ref, None)
        _store_head_major(k, SB_HEADS, kh_ref)
        _store_transposed(v, SB_HEADS, v32_ref, vh_ref)
    else:
        k32_ref[...] = k
        v32_ref[...] = v
        q_ref[...] = q.astype(BF16)


def _even_in(h2, g, w, batch, head_major):
    m = h2.shape[0]
    t = m // batch
    tm = _row_tile(m)
    nt = max(t // tm, 1)
    outs = [jax.ShapeDtypeStruct((m, CONV_CH), F32)]
    specs = [_row_spec(tm, CONV_CH)]
    if head_major:
        kt = lambda dt: jax.ShapeDtypeStruct((batch, SB_HEADS, HEAD_DIM, t), dt)
        outs += [kt(F32), kt(F32), kt(BF16), jax.ShapeDtypeStruct((batch, SB_HEADS, t, HEAD_DIM), BF16), kt(BF16)]
        specs += [_tr_spec(SB_HEADS, tm, nt)] * 3 + [_hm_spec(SB_HEADS, tm, HEAD_DIM, nt), _tr_spec(SB_HEADS, tm, nt)]
    else:
        outs += [jax.ShapeDtypeStruct((m, SB_DIM), F32)] * 2 + [jax.ShapeDtypeStruct((m, SB_DIM), BF16)]
        specs += [_row_spec(tm, SB_DIM)] * 3
    return pl.pallas_call(
        functools.partial(_even_in_body, head_major),
        grid=(m // tm,),
        in_specs=[_row_spec(tm, D_MODEL), _const_spec((1, D_MODEL)), _const_spec(w.shape)],
        out_specs=specs, out_shape=outs,
        compiler_params=_cp(("parallel",)), name="even_in",
    )(h2, g, w)


CONV_PAD = 32


def _conv_body(t, chunk, hp_ref, w_ref, b_ref, lg_ref, lb_ref, o_ref):
    win_rows = chunk + CONV_PAD

    def one_chunk(c, _):
        base = pl.multiple_of(c * chunk, chunk)
        win = hp_ref[0, pl.ds(base, win_rows), :]
        acc = jnp.zeros((chunk, CONV_CH), F32)
        for r in range(8):
            sh = win if r == 0 else pltpu.roll(win, win_rows - r, 0)
            for a in range(4):
                tap = 8 * a + r
                if tap < CONV_W:
                    acc = acc + sh[8 * a:8 * a + chunk] * w_ref[tap:tap + 1, :]
        y = acc + b_ref[...]
        mu = jnp.mean(y, axis=-1, keepdims=True)
        var = jnp.mean(jnp.square(y - mu), axis=-1, keepdims=True)
        y = (y - mu) * lax.rsqrt(var + NORM_EPS) * lg_ref[...] + lb_ref[...]
        o_ref[0, pl.ds(base, chunk), :] = (y * _sigmoid(y)).astype(o_ref.dtype)
        return 0

    lax.fori_loop(0, t // chunk, one_chunk, 0)


def _conformer_conv(glu, buf, w_dw, b_dw, ln_g, ln_b):
    b, t, c = glu.shape
    hp = jnp.concatenate([buf.astype(F32), glu, jnp.zeros((b, CONV_PAD - (CONV_W - 1), c), F32)], axis=1)
    chunk = min(t, 128)
    return pl.pallas_call(
        functools.partial(_conv_body, t, chunk),
        grid=(b,),
        in_specs=[pl.BlockSpec((1, t + CONV_PAD, c), lambda i: (i, 0, 0)),
                  _const_spec((CONV_PAD, c)), _const_spec((1, c)), _const_spec((1, c)), _const_spec((1, c))],
        out_specs=pl.BlockSpec((1, t, c), lambda i: (i, 0, 0)),
        out_shape=jax.ShapeDtypeStruct((b, t, c), BF16),
        compiler_params=_cp(("parallel",)), name="conformer_conv",
    )(hp, jnp.pad(w_dw, ((0, CONV_PAD - CONV_W), (0, 0))), b_dw[None], ln_g[None], ln_b[None])


def _softplus(z):
    return jnp.maximum(z, 0.0) + jnp.log(1.0 + jnp.exp(-jnp.abs(z)))


def _sb_tiles(zs, carries, valid, tri2, keys_axis, chained=False):
    valids = valid if isinstance(valid, (list, tuple)) else [valid] * len(zs)
    sps = []
    for z, ok in zip(zs, valids):
        sp = _softplus(z)
        sps.append(sp if ok is None else jnp.where(ok, sp, 0.0))
    laters = []
    for sp in sps:
        hi = sp.astype(BF16)
        lo = (sp - hi.astype(F32)).astype(BF16)
        split = jnp.concatenate([hi, lo], axis=keys_axis)
        laters.append(_dot(split, tri2) if keys_axis == 1 else _dot(tri2, split))
    first = (lambda a: a[:, 0:1]) if keys_axis == 1 else (lambda a: a[0:1, :])
    ws, new = [], []
    for n, (z, sp, later, ok) in enumerate(zip(zs, sps, laters, valids)):
        carry = new[-1] if chained and n else carries[0 if chained else n]
        w = jnp.exp(z - sp - later - carry)
        ws.append((w if ok is None else jnp.where(ok, w, 0.0)).astype(BF16))
        new.append(carry + first(later) + first(sp))
    return ws, new


def _tri2(tk, keys_axis):
    s = jnp.arange(tk)[:, None]
    j = jnp.arange(tk)[None, :]
    if keys_axis == 1:
        tri = (s > j).astype(BF16)
        return jnp.concatenate([tri, tri], axis=0)
    tri = (j > s).astype(BF16)
    return jnp.concatenate([tri, tri], axis=1)


def _sb_prompt_body(q_ref, k_ref, v_ref, tri_ref, o_ref):
    i = pl.program_id(2)
    tq = q_ref.shape[3]
    key = lax.broadcasted_iota(jnp.int32, (tq, tq), 0)
    qry = lax.broadcasted_iota(jnp.int32, (tq, tq), 1)
    diag_valid = key < qry
    tri2 = tri_ref[...]
    nh = q_ref.shape[1]

    def tiles(j, st, valid):
        start = pl.multiple_of(j * tq, tq)
        zs = [_dot(k_ref[0, hh, pl.ds(start, tq), :], q_ref[0, hh]) for hh in range(nh)]
        ws, carries = _sb_tiles(zs, [s[0] for s in st], valid, tri2, 0)
        accs = [st[hh][1] + _dot(v_ref[0, hh, :, pl.ds(start, tq)], ws[hh]) for hh in range(nh)]
        return tuple(zip(carries, accs))

    init = tuple((jnp.zeros((1, tq), F32), jnp.zeros((HEAD_DIM, tq), F32)) for _ in range(nh))
    st = tiles(i, init, diag_valid)
    st = lax.fori_loop(0, i, lambda n, s: tiles(i - 1 - n, s, None), st)
    o_ref[0] = jnp.concatenate([s[1] for s in st], axis=0).T.astype(o_ref.dtype)


SB_HEADS_PER_STEP = 4


def _sb_prompt(q_t, k_hm, v_t):
    b, h, d, t = q_t.shape
    tq = min(ATT_TILE, t)
    nh = SB_HEADS_PER_STEP
    return pl.pallas_call(
        _sb_prompt_body,
        grid=(b, h // nh, t // tq),
        in_specs=[pl.BlockSpec((1, nh, d, tq), lambda bi, hp, i: (bi, hp, 0, i)),
                  pl.BlockSpec((1, nh, t, d), lambda bi, hp, i: (bi, hp, 0, 0)),
                  pl.BlockSpec((1, nh, d, t), lambda bi, hp, i: (bi, hp, 0, 0)),
                  _const_spec((tq, 2 * tq))],
        out_specs=pl.BlockSpec((1, tq, nh * d), lambda bi, hp, i: (bi, i, hp)),
        out_shape=jax.ShapeDtypeStruct((b, t, h * d), BF16),
        compiler_params=_cp(("parallel", "parallel", "arbitrary")), name="sb_prompt",
    )(q_t, k_hm, v_t, _tri2(tq, 0))


def _block_diag_rows(q, groups):
    b, t, h, d = q.shape
    per = h // groups
    qg = jnp.transpose(q.reshape(b, t, groups, per, d), (0, 2, 3, 1, 4))
    eye = jnp.eye(groups, dtype=q.dtype)[None, :, None, None, :, None]
    return (qg[:, :, :, :, None, :] * eye).reshape(b, h * t, groups * d)


def _diag_rows_out(o, t, groups):
    b, rows, gd = o.shape
    d = gd // groups
    per = rows // (groups * t)
    o6 = o.reshape(b, groups, per, t, groups, d)
    idx = jnp.arange(groups)
    og = o6[:, idx, :, :, idx, :]
    return jnp.transpose(og, (1, 3, 0, 2, 4)).reshape(b, t, groups * per * d)


def _sb_sample_body(npages, t_new, pt_ref, qbd_ref, kn_ref, vn_ref, tri_ref, *refs):
    k_refs = refs[:npages]
    v_refs = refs[npages:2 * npages]
    o_ref = refs[2 * npages]
    qbd = qbd_ref[0]
    rows = qbd.shape[0]
    tri2 = tri_ref[...]
    width = kn_ref.shape[2]
    pad = jnp.zeros((PAGE - t_new, width), F32)
    kn = jnp.concatenate([kn_ref[0], pad], axis=0).astype(BF16)
    vn = jnp.concatenate([vn_ref[0], pad], axis=0).astype(BF16)
    qoff = lax.broadcasted_iota(jnp.int32, (rows, PAGE), 0) % t_new
    col = lax.broadcasted_iota(jnp.int32, (rows, PAGE), 1)
    order = list(reversed(range(npages)))
    zs = [_nt(qbd, kn)] + [_dot(qbd, k_refs[p][0].reshape(width, PAGE).astype(BF16)) for p in order]
    ws, _ = _sb_tiles(zs, [jnp.zeros((rows, 1), F32)], [col < qoff] + [None] * npages, tri2, 1, chained=True)
    acc = _dot(ws[0], vn)
    for w, p in zip(ws[1:], order):
        acc = acc + _nt(w, v_refs[p][0].reshape(width, PAGE).astype(BF16))
    o_ref[0] = acc


def _page_specs(npages, heads):
    return [pl.BlockSpec((1, heads, HEAD_DIM, PAGE), lambda b, pt, p=p: (pt[b, p], 0, 0, 0))
            for p in range(npages)]


def _keys_minor(cache):
    nd = cache.ndim
    return jnp.transpose(cache, tuple(range(nd - 3)) + (nd - 2, nd - 1, nd - 3))


def _sb_sample(q, k_new, v_new, cache_k, cache_v, page_table):
    b, t, w = k_new.shape
    npages = page_table.shape[1]
    qbd = _block_diag_rows(q.reshape(b, t, SB_HEADS, HEAD_DIM), SB_HEADS)
    rows = SB_HEADS * t
    bspec = lambda r, c: pl.BlockSpec((1, r, c), lambda bi, pt: (bi, 0, 0))
    grid_spec = pltpu.PrefetchScalarGridSpec(
        num_scalar_prefetch=1, grid=(b,),
        in_specs=[bspec(rows, w), bspec(t, w), bspec(t, w),
                  pl.BlockSpec((2 * PAGE, PAGE), lambda bi, pt: (0, 0))]
                 + _page_specs(npages, SB_HEADS) + _page_specs(npages, SB_HEADS),
        out_specs=bspec(rows, w))
    o = pl.pallas_call(
        functools.partial(_sb_sample_body, npages, t),
        grid_spec=grid_spec,
        out_shape=jax.ShapeDtypeStruct((b, rows, w), F32),
        compiler_params=_cp(("parallel",)), name="sb_sample",
    )(page_table, qbd, k_new, v_new, _tri2(PAGE, 1), *([cache_k] * npages), *([cache_v] * npages))
    return _diag_rows_out(o, t, SB_HEADS)


def _proj_res_body(n_in, *refs):
    h_ref = refs[0]
    x_refs = refs[1:1 + n_in]
    w_ref = refs[1 + n_in]
    o_ref = refs[2 + n_in]
    acc = h_ref[...]
    off = 0
    for x_ref in x_refs:
        k = x_ref.shape[1]
        acc = acc + _dot(x_ref[...].astype(BF16), w_ref[off:off + k, :])
        off += k
    o_ref[...] = acc


def _proj_res(h2, xs, w):
    m = h2.shape[0]
    tm = _row_tile(m)
    return pl.pallas_call(
        functools.partial(_proj_res_body, len(xs)),
        grid=(m // tm,),
        in_specs=[_row_spec(tm, D_MODEL)] + [_row_spec(tm, x.shape[1]) for x in xs] + [_const_spec(w.shape)],
        out_specs=_row_spec(tm, D_MODEL),
        out_shape=jax.ShapeDtypeStruct((m, D_MODEL), F32),
        compiler_params=_cp(("parallel",)), name="proj_res",
    )(h2, *xs, w)


def _norm_proj_body(scale, h_ref, g_ref, w_ref, o_ref):
    xn = _rms(h_ref[...], g_ref[...]).astype(BF16)
    o_ref[...] = (_dot(xn, w_ref[...]) * scale).astype(o_ref.dtype)


def _norm_proj(h2, g, w, scale):
    m = h2.shape[0]
    tm = _row_tile(m)
    return pl.pallas_call(
        functools.partial(_norm_proj_body, scale),
        grid=(m // tm,),
        in_specs=[_row_spec(tm, D_MODEL), _const_spec((1, D_MODEL)), _const_spec(w.shape)],
        out_specs=_row_spec(tm, w.shape[1]),
        out_shape=jax.ShapeDtypeStruct((m, w.shape[1]), BF16),
        compiler_params=_cp(("parallel",)), name="norm_proj",
    )(h2, g, w)


def _mem_kv_body(x_ref, wk0, wv0, wk1, wv1, *o_refs):
    x = x_ref[...].astype(BF16)
    for n, w_ref in enumerate((wk0, wv0, wk1, wv1)):
        y = _dot(x, w_ref[...])
        o_refs[2 * n][...] = y
        o_refs[2 * n + 1][...] = y.astype(BF16)


def _mem_kv(x2, ws):
    m = x2.shape[0]
    tm = _row_tile(m)
    outs, specs = [], []
    for _ in ws:
        outs += [jax.ShapeDtypeStruct((m, D_MODEL), F32), jax.ShapeDtypeStruct((m, D_MODEL), BF16)]
        specs += [_row_spec(tm, D_MODEL)] * 2
    return pl.pallas_call(
        _mem_kv_body, grid=(m // tm,),
        in_specs=[_row_spec(tm, D_MODEL)] + [_const_spec(w.shape) for w in ws],
        out_specs=specs, out_shape=outs,
        compiler_params=_cp(("parallel",)), name="mem_kv",
    )(x2, *ws)


def _mem_attn_body(q_ref, k_ref, v_ref, o_ref):
    for hh in range(MEM_HEADS):
        sl = slice(hh * MEM_HD, (hh + 1) * MEM_HD)
        s = _nt(q_ref[0, :, sl], k_ref[0, :, sl].astype(BF16))
        m = jnp.max(s, axis=-1, keepdims=True)
        e = jnp.exp(s - m)
        p = e / jnp.sum(e, axis=-1, keepdims=True)
        o_ref[0, :, sl] = _dot(p.astype(BF16), v_ref[0, :, sl].astype(BF16)).astype(o_ref.dtype)


def _mem_attn(q, mk, mv):
    b, t, d = q.shape
    tq = min(512, t)
    ml = mk.shape[1]
    kv_spec = pl.BlockSpec((1, ml, d), lambda bi, i: (bi, 0, 0))
    return pl.pallas_call(
        _mem_attn_body, grid=(b, t // tq),
        in_specs=[pl.BlockSpec((1, tq, d), lambda bi, i: (bi, i, 0)), kv_spec, kv_spec],
        out_specs=pl.BlockSpec((1, tq, d), lambda bi, i: (bi, i, 0)),
        out_shape=jax.ShapeDtypeStruct((b, t, d), BF16),
        compiler_params=_cp(("parallel", "parallel")), name="mem_attn",
    )(q, mk, mv)


def _mem_attn_cached_body(q_ref, k_ref, v_ref, o_ref):
    ml, heads, hd = k_ref.shape[2:]
    k = k_ref[0, 0].reshape(ml * heads, hd).astype(BF16)
    v = v_ref[0, 0].reshape(ml * heads, hd).astype(BF16)
    q = q_ref[0]
    s = _nt(k, q)
    t = q.shape[0] // heads
    same = (lax.broadcasted_iota(jnp.int32, s.shape, 0) % heads) == (lax.broadcasted_iota(jnp.int32, s.shape, 1) // t)
    sm = jnp.where(same, s, NEG_INF)
    m = jnp.max(sm, axis=0, keepdims=True)
    e = jnp.where(same, jnp.exp(sm - m), 0.0)
    p = (e / jnp.sum(e, axis=0, keepdims=True)).astype(BF16)
    o_ref[0] = lax.dot_general(p, v, (((0,), (0,)), ((), ())), preferred_element_type=F32).astype(o_ref.dtype)


def _mem_attn_cached(q, cache_k, cache_v, layer):
    b, t, d = q.shape
    _, _, ml, heads, hd = cache_k.shape
    rows = heads * t
    q_rows = jnp.transpose(q.reshape(b, t, heads, hd), (0, 2, 1, 3)).reshape(b, rows, hd)
    kv_spec = pl.BlockSpec((1, 1, ml, heads, hd), lambda bi: (layer, bi, 0, 0, 0))
    o = pl.pallas_call(
        _mem_attn_cached_body, grid=(b,),
        in_specs=[pl.BlockSpec((1, rows, hd), lambda bi: (bi, 0, 0)), kv_spec, kv_spec],
        out_specs=pl.BlockSpec((1, rows, hd), lambda bi: (bi, 0, 0)),
        out_shape=jax.ShapeDtypeStruct((b, rows, hd), BF16),
        compiler_params=_cp(("parallel",)), name="mem_attn_cached",
    )(q_rows, cache_k, cache_v)
    return jnp.transpose(o.reshape(b, heads, t, hd), (0, 2, 1, 3)).reshape(b, t, d)


FFN_CHUNK = 512


def _ffn_body(final, h_ref, g_ref, wg_ref, wu_ref, wd_ref, *rest):
    if final:
        gf_ref, o_ref = rest
    else:
        (o_ref,) = rest
    h = h_ref[...]
    xn = _rms(h, g_ref[...]).astype(BF16)
    acc = h
    for lo in range(0, D_FF, FFN_CHUNK):
        hi = min(lo + FFN_CHUNK, D_FF)
        gate = _dot(xn, wg_ref[:, lo:hi])
        up = _dot(xn, wu_ref[:, lo:hi])
        act = (gate * _sigmoid(gate) * up).astype(BF16)
        acc = acc + _dot(act, wd_ref[lo:hi, :])
    if final:
        acc = _rms(acc, gf_ref[...])
    o_ref[...] = acc


def _ffn(h2, g, wg, wu, wd, final_g=None):
    m = h2.shape[0]
    tm = _row_tile(m)
    final = final_g is not None
    once = lambda shape: pl.BlockSpec(shape, lambda i: (0, 0), pipeline_mode=pl.Buffered(1))
    in_specs = [_row_spec(tm, D_MODEL), _const_spec((1, D_MODEL)), once(wg.shape), once(wu.shape), once(wd.shape)]
    args = [h2, g, wg, wu, wd]
    if final:
        in_specs.append(_const_spec((1, D_MODEL)))
        args.append(final_g)
    return pl.pallas_call(
        functools.partial(_ffn_body, final), grid=(m // tm,),
        in_specs=in_specs, out_specs=_row_spec(tm, D_MODEL),
        out_shape=jax.ShapeDtypeStruct((m, D_MODEL), F32),
        compiler_params=_cp(("parallel",)), name="ffn",
    )(*args)


ODD_Q = NSA_HEADS * HEAD_DIM
ODD_GATES = 3 * NSA_HEADS
ODD_IN = ODD_Q + 6 * NSA_KV_DIM + ODD_GATES
GATE_PAD = 16
GATE_LANES = NSA_KVH * GATE_PAD
ODD_IN_PAD = ODD_Q + 6 * NSA_KV_DIM + 128


def _odd_in_weights(w_in_odd):
    gates = w_in_odd[:, ODD_Q + 6 * NSA_KV_DIM:].reshape(D_MODEL, NSA_KVH, 3 * NSA_GQA)
    gates = jnp.pad(gates, ((0, 0), (0, 0), (0, GATE_PAD - 3 * NSA_GQA))).reshape(D_MODEL, GATE_LANES)
    return jnp.concatenate([w_in_odd[:, :ODD_Q + 6 * NSA_KV_DIM],
                            jnp.pad(gates, ((0, 0), (0, 128 - GATE_LANES)))], axis=1).astype(BF16)


def _odd_in_body(head_major, h_ref, g_ref, w_ref, c_ref, s_ref, *o_refs):
    xn = _rms(h_ref[...], g_ref[...]).astype(BF16)
    c = c_ref[...]
    s = s_ref[...]

    def mm(lo, hi):
        return _dot(xn, w_ref[:, lo:hi])

    kv = lambda n: mm(ODD_Q + n * NSA_KV_DIM, ODD_Q + (n + 1) * NSA_KV_DIM)
    q = _rope_wide(mm(0, ODD_Q), c, s) * (HEAD_DIM ** -0.5)
    ck, cv = kv(0), kv(1)
    sk, sv = _rope_wide(kv(2), c, s), kv(3)
    wk, wv = _rope_wide(kv(4), c, s), kv(5)
    gates = _sigmoid(mm(ODD_Q + 6 * NSA_KV_DIM, ODD_IN_PAD))[:, :GATE_LANES]
    if head_major:
        ck_rows, cv_rows = o_refs[:2]
        t32 = o_refs[2:8]
        q_ref, g_out, sk_rows, sv_t, wk_rows, wv_t = o_refs[8:14]
        ck_rows[...] = ck
        cv_rows[...] = cv
        for val, r32, r16 in zip((ck, cv, sk, sv, wk, wv), t32, (None, None, None, sv_t, None, wv_t)):
            _store_transposed(val, NSA_KVH, r32, r16)
        _store_transposed(q, NSA_HEADS, None, q_ref)
        _store_head_major(sk, NSA_KVH, sk_rows)
        _store_head_major(wk, NSA_KVH, wk_rows)
        g_out[0] = gates.T.reshape(NSA_KVH, GATE_PAD, gates.shape[0])
    else:
        for ref, val in zip(o_refs[:6], (ck, cv, sk, sv, wk, wv)):
            ref[...] = val
        q_ref, g_out = o_refs[6:]
        q_ref[...] = q.astype(BF16)
        g_out[...] = gates


def _odd_in(h2, g, w, cos_t, sin_t, batch, head_major):
    m = h2.shape[0]
    t = m // batch
    tm = _row_tile(m)
    nt = max(t // tm, 1)
    ntab = cos_t.shape[0] // tm
    tab_spec = pl.BlockSpec((tm, 128), lambda i: (i % ntab, 0))
    kv = jax.ShapeDtypeStruct((m, NSA_KV_DIM), F32)
    if head_major:
        kt = lambda dt: jax.ShapeDtypeStruct((batch, NSA_KVH, HEAD_DIM, t), dt)
        rows16 = jax.ShapeDtypeStruct((batch, NSA_KVH, t, HEAD_DIM), BF16)
        outs = [kv] * 2 + [kt(F32)] * 6
        outs += [jax.ShapeDtypeStruct((batch, NSA_HEADS, HEAD_DIM, t), BF16),
                 jax.ShapeDtypeStruct((batch, NSA_KVH, GATE_PAD, t), F32), rows16, kt(BF16), rows16, kt(BF16)]
        rows_spec = _hm_spec(NSA_KVH, tm, HEAD_DIM, nt)
        specs = [_row_spec(tm, NSA_KV_DIM)] * 2 + [_tr_spec(NSA_KVH, tm, nt)] * 6
        specs += [_tr_spec(NSA_HEADS, tm, nt),
                  pl.BlockSpec((1, NSA_KVH, GATE_PAD, tm), lambda i: (i // nt, 0, 0, i % nt)),
                  rows_spec, _tr_spec(NSA_KVH, tm, nt), rows_spec, _tr_spec(NSA_KVH, tm, nt)]
    else:
        outs = [kv] * 6 + [jax.ShapeDtypeStruct((m, ODD_Q), BF16), jax.ShapeDtypeStruct((m, GATE_LANES), F32)]
        specs = [_row_spec(tm, NSA_KV_DIM)] * 6 + [_row_spec(tm, ODD_Q), _row_spec(tm, GATE_LANES)]
    return pl.pallas_call(
        functools.partial(_odd_in_body, head_major), grid=(m // tm,),
        in_specs=[_row_spec(tm, D_MODEL), _const_spec((1, D_MODEL)), _const_spec(w.shape), tab_spec, tab_spec],
        out_specs=specs, out_shape=outs,
        compiler_params=_cp(("parallel",)), name="odd_in",
    )(h2, g, w, cos_t, sin_t)


def _gelu_tanh(x):
    return 0.5 * x * (1.0 + jnp.tanh(math.sqrt(2.0 / math.pi) * (x + 0.044715 * x * x * x)))


def _compress_body(add_pe, use_rope, x_ref, pe_ref, w1_ref, w2_ref, c_ref, s_ref, o_ref, acc_ref, wbd_ref):
    l = pl.program_id(0)

    @pl.when(l == 0)
    def _():
        acc_ref[...] = jnp.zeros_like(acc_ref)
        wbd_ref[...] = jnp.zeros_like(wbd_ref)

    for gg in range(NSA_KVH):
        wbd_ref[gg * HEAD_DIM:(gg + 1) * HEAD_DIM, gg * CMP_HID:(gg + 1) * CMP_HID] = w1_ref[0]
    x = x_ref[...]
    if add_pe:
        x = (x + pe_ref[0]).astype(BF16)
    acc_ref[...] += _dot(x, wbd_ref[...])

    @pl.when(l == CMP_BLK - 1)
    def _():
        hid = _gelu_tanh(acc_ref[...]).astype(BF16)
        y = _dot(hid, w2_ref[...])
        if use_rope:
            y = _rope_wide(y, c_ref[...], s_ref[...])
        o_ref[...] = y


def _compress(x2, pe, w1, w2, tables, add_pe):
    rows = x2.shape[0]
    pe_t = jnp.tile(pe, (1, NSA_KVH))[:, None, :]
    w2bd = jnp.kron(jnp.eye(NSA_KVH, dtype=F32), w2).astype(BF16)
    use_rope = tables is not None
    if use_rope:
        c_t, s_t = tables
    else:
        c_t = s_t = jnp.zeros((rows, 128), F32)
    return pl.pallas_call(
        functools.partial(_compress_body, add_pe, use_rope), grid=(CMP_BLK,),
        in_specs=[pl.BlockSpec((rows, NSA_KV_DIM), lambda l: (0, l)),
                  pl.BlockSpec((1, 1, NSA_KV_DIM), lambda l: (l, 0, 0)),
                  pl.BlockSpec((1, HEAD_DIM, CMP_HID), lambda l: (l, 0, 0)),
                  _const_spec(w2bd.shape), _const_spec((rows, 128)), _const_spec((rows, 128))],
        out_specs=_const_spec((rows, NSA_KV_DIM)),
        out_shape=jax.ShapeDtypeStruct((rows, NSA_KV_DIM), F32),
        scratch_shapes=[pltpu.VMEM((rows, NSA_KVH * CMP_HID), F32),
                        pltpu.VMEM((NSA_KV_DIM, NSA_KVH * CMP_HID), BF16)],
        compiler_params=_cp(("arbitrary",)), name="compress",
    )(x2, pe_t, w1.astype(BF16), w2bd, c_t, s_t)


NB_PAD = 128
RANK_ROWS = 64


def _cmp_and_select(q, kc, vc, qpos, rows_per_q, nb):
    tq = qpos.shape[0]
    s = _nt(q, kc).reshape(rows_per_q, tq, NB_PAD)
    n_idx = lax.broadcasted_iota(jnp.int32, (tq, NB_PAD), 1)
    valid = (n_idx * CMP_BLK + (CMP_BLK - 1)) <= qpos
    sm = jnp.where(valid[None], s, NEG_INF)
    m = jnp.max(sm, axis=-1, keepdims=True)
    p = jnp.where(valid[None], jnp.exp(sm - m), 0.0)
    p = p / jnp.maximum(jnp.sum(p, axis=-1, keepdims=True), 1e-30)
    o_cmp = _dot(p.reshape(rows_per_q * tq, NB_PAD).astype(BF16), vc)
    cur = qpos // CMP_BLK
    forced = (n_idx == 0) | (n_idx == cur) | (n_idx == cur - 1)
    score = jnp.where(n_idx > cur, -1.0, jnp.where(forced, FORCE_SCORE, jnp.sum(p, axis=0)))
    tqp = max(tq, 128)
    if tqp > tq:
        score = jnp.concatenate([score, jnp.zeros((tqp - tq, NB_PAD), F32)], axis=0)
    sel_t = _top_n_mask(score.T[:RANK_ROWS], nb)
    sel = jnp.concatenate([sel_t, jnp.zeros((NB_PAD - RANK_ROWS, tqp), F32)], axis=0).T
    return o_cmp, sel[:tq]


def _top_n_mask(st, nb):
    groups = RANK_ROWS // 8
    blocks = [st[8 * g:8 * g + 8] for g in range(groups)]
    cnt = [jnp.zeros_like(b) for b in blocks]
    row = lax.broadcasted_iota(jnp.int32, blocks[0].shape, 0)
    for i in range(nb):
        r = st[i:i + 1]
        for g in range(groups):
            gt = jnp.where(r > blocks[g], 1.0, 0.0)
            if 8 * g + 7 <= i:
                inc = gt
            else:
                ge = jnp.where(r >= blocks[g], 1.0, 0.0)
                inc = ge if 8 * g > i else jnp.where(row + 8 * g > i, ge, gt)
            cnt[g] = cnt[g] + inc
    return jnp.concatenate([jnp.where(c < float(SEL_TOPN), 1.0, 0.0) for c in cnt], axis=0)


def _osm(z, st, v, keys_axis, v_keys_minor):
    m, l, acc = st
    m_new = jnp.maximum(m, jnp.max(z, axis=keys_axis, keepdims=True))
    alpha = jnp.exp(m - m_new)
    p = jnp.exp(z - m_new)
    l = alpha * l + jnp.sum(p, axis=keys_axis, keepdims=True)
    pb = p.astype(BF16)
    if keys_axis == 1:
        pv = _nt(pb, v) if v_keys_minor else _dot(pb, v)
    else:
        pv = _dot(v, pb)
    return m_new, l, alpha * acc + pv


def _softmax_tiles(zs, vs):
    m = functools.reduce(jnp.maximum, [jnp.max(z, axis=1, keepdims=True) for z in zs])
    ps = [jnp.exp(z - m) for z in zs]
    l = functools.reduce(lambda a, b: a + b, [jnp.sum(p, axis=1, keepdims=True) for p in ps])
    acc = None
    for p, (v, keys_minor) in zip(ps, vs):
        pv = _nt(p.astype(BF16), v) if keys_minor else _dot(p.astype(BF16), v)
        acc = pv if acc is None else acc + pv
    return acc / jnp.maximum(l, 1e-30)


def _osm_groups_t(q_ts, k_rows, v_t, bias, sts):
    n = len(q_ts)
    z_next = _dot(k_rows, q_ts[0]) + bias
    out = []
    for r in range(n):
        z = z_next
        if r + 1 < n:
            z_next = _dot(k_rows, q_ts[r + 1]) + bias
        out.append(_osm(z, sts[r], v_t, 0, True))
    return tuple(out)


def _osm_init(rows, width, keys_axis=1):
    if keys_axis == 1:
        return (jnp.full((rows, 1), NEG_INF, F32), jnp.zeros((rows, 1), F32), jnp.zeros((rows, width), F32))
    return (jnp.full((1, rows), NEG_INF, F32), jnp.zeros((1, rows), F32), jnp.zeros((width, rows), F32))


def _osm_out(st):
    return st[2] / jnp.maximum(st[1], 1e-30)


def _nsa_prompt_body(nb, q_ref, kc_ref, vc_ref, sk_ref, sv_ref, wk_ref, wv_ref, g_ref, e_ref, o_ref):
    i = pl.program_id(2)
    tq = q_ref.shape[3]
    tk = tq
    rq = NSA_GQA
    q_ts = [q_ref[0, r] for r in range(rq)]
    qpos = i * tq + lax.broadcasted_iota(jnp.int32, (1, tq), 1)

    n_idx = lax.broadcasted_iota(jnp.int32, (NB_PAD, tq), 0)
    valid = (n_idx * CMP_BLK + (CMP_BLK - 1)) <= qpos
    kc = kc_ref[0, 0]
    vc_t = vc_ref[0, 0]
    o_cmp, score = [], None
    for r in range(rq):
        sm = jnp.where(valid, _dot(kc, q_ts[r]), NEG_INF)
        m = jnp.max(sm, axis=0, keepdims=True)
        p = jnp.where(valid, jnp.exp(sm - m), 0.0)
        p = p / jnp.maximum(jnp.sum(p, axis=0, keepdims=True), 1e-30)
        o_cmp.append(_dot(vc_t, p.astype(BF16)))
        score = p if score is None else score + p
    cur = qpos // CMP_BLK
    forced = (n_idx == 0) | (n_idx == cur) | (n_idx == cur - 1)
    score = jnp.where(n_idx > cur, -1.0, jnp.where(forced, FORCE_SCORE, score))
    sel_t = _top_n_mask(score[:RANK_ROWS], nb)
    selb = jnp.concatenate([sel_t, jnp.zeros((NB_PAD - RANK_ROWS, tq), F32)], axis=0).astype(BF16)

    key = lax.broadcasted_iota(jnp.int32, (tk, tq), 0)
    qry = lax.broadcasted_iota(jnp.int32, (tk, tq), 1)
    init = tuple(_osm_init(tq, HEAD_DIM, 0) for _ in range(rq))

    def sel_tile(j, sts, diag):
        start = pl.multiple_of(j * tk, tk)
        chosen = _dot(e_ref[pl.ds(start, tk), :], selb)
        bias = (chosen - 1.0) * 1e30
        if diag:
            bias = jnp.where(key <= qry, bias, NEG_INF)
        return _osm_groups_t(q_ts, sk_ref[0, 0, pl.ds(start, tk), :], sv_ref[0, 0, :, pl.ds(start, tk)], bias, sts)

    sts = lax.fori_loop(0, i, lambda j, s: sel_tile(j, s, False), init)
    o_sel = [_osm_out(st) for st in sel_tile(i, sts, True)]

    def win_tile(back, sts):
        j_raw = i - back
        start = pl.multiple_of(jnp.maximum(j_raw, 0) * tk, tk)
        if back == 0:
            ok = key <= qry
        elif back < WINDOW // tk:
            ok = key >= 0
        else:
            ok = key > qry
        bias = jnp.where(ok & (j_raw >= 0), 0.0, NEG_INF)
        return _osm_groups_t(q_ts, wk_ref[0, 0, pl.ds(start, tk), :], wv_ref[0, 0, :, pl.ds(start, tk)], bias, sts)

    sts = init
    for back in range(WINDOW // tk + 1):
        sts = win_tile(back, sts)
    o_win = [_osm_out(st) for st in sts]

    gates = g_ref[0, 0]
    outs = []
    for r in range(rq):
        gc, gs, gw = (gates[3 * r + n:3 * r + n + 1, :] for n in range(3))
        outs.append(gc * o_cmp[r] + gs * o_sel[r] + gw * o_win[r])
    o_ref[0] = jnp.concatenate(outs, axis=0).T.astype(o_ref.dtype)


def _nsa_prompt(q_t, kc_hm, vc_t, sk_hm, sv_t, wk_hm, wv_t, gates_t, nb):
    b, _, d, t = q_t.shape
    tq = min(NSA_TILE, t)
    assert WINDOW % tq == 0 and nb <= RANK_ROWS and tq % 128 == 0
    expand = ((jnp.arange(t)[:, None] // CMP_BLK) == jnp.arange(NB_PAD)[None, :]).astype(BF16)
    per_group = lambda shape: pl.BlockSpec((1, 1) + shape, lambda bi, g, i: (bi, g, 0, 0))
    return pl.pallas_call(
        functools.partial(_nsa_prompt_body, nb),
        grid=(b, NSA_KVH, t // tq),
        in_specs=[pl.BlockSpec((1, NSA_GQA, d, tq), lambda bi, g, i: (bi, g, 0, i)),
                  per_group((NB_PAD, d)), per_group((d, NB_PAD)),
                  per_group((t, d)), per_group((d, t)), per_group((t, d)), per_group((d, t)),
                  pl.BlockSpec((1, 1, GATE_PAD, tq), lambda bi, g, i: (bi, g, 0, i)),
                  _const_spec((t, NB_PAD))],
        out_specs=pl.BlockSpec((1, tq, NSA_GQA * d), lambda bi, g, i: (bi, i, g)),
        out_shape=jax.ShapeDtypeStruct((b, t, NSA_HEADS * d), BF16),
        compiler_params=_cp(("parallel", "parallel", "arbitrary")), name="nsa_prompt",
    )(q_t, kc_hm, vc_t, sk_hm, sv_t, wk_hm, wv_t, gates_t, expand)


def _cmp_gather_body(npages, pt_ref, pe_ref, *refs):
    page_refs = refs[:npages]
    o_ref = refs[npages]
    tok_ref = refs[npages + 1]
    pe2 = pe_ref[...]
    lane_tiles = NSA_KV_DIM // 128
    for p in range(npages):
        tok = page_refs[p][0].reshape(NSA_KV_DIM, PAGE).T + pe2
        for c in range(lane_tiles):
            tok_ref[c, p * PAGE:(p + 1) * PAGE, :] = tok[:, c * 128:(c + 1) * 128]
    nblk = npages * PAGE // CMP_BLK
    def regroup(l, _):
        for c in range(lane_tiles):
            lo = pl.multiple_of(l * NSA_KV_DIM + c * 128, 128)
            o_ref[0, :, pl.ds(lo, 128)] = tok_ref[c, pl.ds(l, nblk, stride=CMP_BLK), :].astype(BF16)
        return 0

    lax.fori_loop(0, CMP_BLK, regroup, 0, unroll=8)


def _cmp_gather(cache, pe, page_table):
    b, npages = page_table.shape
    nblk = npages * PAGE // CMP_BLK
    width = CMP_BLK * NSA_KV_DIM
    pe2 = jnp.tile(pe, (PAGE // CMP_BLK, NSA_KVH))
    grid_spec = pltpu.PrefetchScalarGridSpec(
        num_scalar_prefetch=1, grid=(b,),
        in_specs=[pl.BlockSpec((PAGE, NSA_KV_DIM), lambda bi, pt: (0, 0))] + _page_specs(npages, NSA_KVH),
        out_specs=pl.BlockSpec((1, nblk, width), lambda bi, pt: (bi, 0, 0)),
        scratch_shapes=[pltpu.VMEM((NSA_KV_DIM // 128, npages * PAGE, 128), F32)])
    return pl.pallas_call(
        functools.partial(_cmp_gather_body, npages), grid_spec=grid_spec,
        out_shape=jax.ShapeDtypeStruct((b, nblk, width), BF16),
        compiler_params=_cp(("parallel",)), name="cmp_gather",
    )(page_table, pe2, *([cache] * npages))


def _nsa_sample_body(npages, t_new, past, nb, pt_ref, qbd_ref, kc_ref, vc_ref, g_ref,
                     skn_ref, svn_ref, wkn_ref, wvn_ref, wkc_ref, wvc_ref, *refs):
    sk_refs = refs[:npages]
    sv_refs = refs[npages:2 * npages]
    o_ref = refs[2 * npages]
    qbd = qbd_ref[0]
    rows = qbd.shape[0]
    w = qbd.shape[1]
    rq = rows // t_new
    t_idx = lax.broadcasted_iota(jnp.int32, (t_new, 1), 0)
    qpos = past + t_idx

    o_cmp_parts, sel_parts = [], []
    per = NSA_GQA * t_new
    for gg in range(NSA_KVH):
        oc, sel = _cmp_and_select(qbd[gg * per:(gg + 1) * per], kc_ref[0], vc_ref[0], qpos, NSA_GQA, nb)
        o_cmp_parts.append(oc)
        sel_parts.append(jnp.broadcast_to(sel[None], (NSA_GQA, t_new, NB_PAD)).reshape(per, NB_PAD))
    o_cmp = jnp.concatenate(o_cmp_parts, axis=0)
    sel_bias = (jnp.concatenate(sel_parts, axis=0) - 1.0) * 1e30

    qoff = lax.broadcasted_iota(jnp.int32, (rows, PAGE), 0) % t_new
    col = lax.broadcasted_iota(jnp.int32, (rows, PAGE), 1)
    pad = jnp.zeros((PAGE - t_new, w), F32)
    new_bias = jnp.where(col <= qoff, 0.0, NEG_INF)

    def padded(ref):
        return jnp.concatenate([ref[0], pad], axis=0).astype(BF16)

    half = col < CMP_BLK
    zs, vs = [], []
    for p in range(npages):
        b0 = sel_bias[:, 2 * p:2 * p + 1]
        b1 = sel_bias[:, 2 * p + 1:2 * p + 2]
        zs.append(_dot(qbd, sk_refs[p][0].reshape(w, PAGE).astype(BF16)) + jnp.where(half, b0, b1))
        vs.append((sv_refs[p][0].reshape(w, PAGE).astype(BF16), True))
    last = (past // CMP_BLK)
    zs.append(_nt(qbd, padded(skn_ref)) + new_bias + sel_bias[:, last:last + 1])
    vs.append((padded(svn_ref), False))
    o_sel = _softmax_tiles(zs, vs)

    zs, vs = [_nt(qbd, padded(wkn_ref)) + new_bias], [(padded(wvn_ref), False)]
    wb = wkc_ref.shape[3]
    for c in range(wb // PAGE):
        kpos = (past - wb) + c * PAGE + col
        diff = (past + qoff) - kpos
        bias = jnp.where((diff < WINDOW) & (kpos >= 0), 0.0, NEG_INF)
        chunk = lambda ref: ref[0, :, :, c * PAGE:(c + 1) * PAGE].reshape(w, PAGE).astype(BF16)
        zs.append(_dot(qbd, chunk(wkc_ref)) + bias)
        vs.append((chunk(wvc_ref), True))
    o_win = _softmax_tiles(zs, vs)

    gates = g_ref[0]
    o_ref[0] = gates[:, 0:1] * o_cmp + gates[:, 1:2] * o_sel + gates[:, 2:3] * o_win


def _nsa_sample(q, kc, vc, gates, sk_new, sv_new, wk_new, wv_new, win_k, win_v,
                cache_sk, cache_sv, page_table, past, nb):
    b, t, w = sk_new.shape
    npages = page_table.shape[1]
    assert nb <= RANK_ROWS and (past // CMP_BLK) < nb
    qbd = _block_diag_rows(q.reshape(b, t, NSA_HEADS, HEAD_DIM), NSA_KVH)
    rows = NSA_HEADS * t
    g_rows = jnp.transpose(gates.reshape(b, t, NSA_KVH, NSA_GQA, 3), (0, 2, 3, 1, 4)).reshape(b, rows, 3)
    bspec = lambda r, c: pl.BlockSpec((1, r, c), lambda bi, pt: (bi, 0, 0))
    wb = win_k.shape[3]
    win_spec = pl.BlockSpec((1, NSA_KVH, HEAD_DIM, wb), lambda bi, pt: (bi, 0, 0, 0))
    grid_spec = pltpu.PrefetchScalarGridSpec(
        num_scalar_prefetch=1, grid=(b,),
        in_specs=[bspec(rows, w), bspec(NB_PAD, w), bspec(NB_PAD, w), bspec(rows, 3),
                  bspec(t, w), bspec(t, w), bspec(t, w), bspec(t, w), win_spec, win_spec]
                 + _page_specs(npages, NSA_KVH) + _page_specs(npages, NSA_KVH),
        out_specs=bspec(rows, w))
    o = pl.pallas_call(
        functools.partial(_nsa_sample_body, npages, t, past, nb), grid_spec=grid_spec,
        out_shape=jax.ShapeDtypeStruct((b, rows, w), F32),
        compiler_params=_cp(("parallel",)), name="nsa_sample",
    )(page_table, qbd, kc, vc, g_rows, sk_new, sv_new, wk_new, wv_new, win_k, win_v,
      *([cache_sk] * npages), *([cache_sv] * npages))
    return _diag_rows_out(o, t, NSA_KVH)


def _common_tail(h2, batch, l, mem_k, mem_v, p, final_g):
    t = h2.shape[0] // batch
    q = _norm_proj(h2, p["norm_mem"][l][None], p["w_mem_q"][l], MEM_HD ** -0.5).reshape(batch, t, D_MODEL)
    if mem_k.ndim == 5:
        o = _mem_attn_cached(q, mem_k, mem_v, l)
    else:
        o = _mem_attn(q, mem_k, mem_v)
    h2 = _proj_res(h2, [o.reshape(batch * t, D_MODEL)], p["w_mem_o"][l])
    return _ffn(h2, p["norm_ffn"][l][None], p["w_ffn_gate"][l], p["w_ffn_up"][l], p["w_ffn_down"][l], final_g)


def _even_layer(h2, batch, p, conv_buf, sample_ctx):
    t = h2.shape[0] // batch
    prompt = sample_ctx is None
    res = _even_in(h2, p["norm_mix"][0][None], p["w_in_even"][0], batch, prompt)
    glu3 = res[0].reshape(batch, t, CONV_CH)
    y_conv = _conformer_conv(glu3, conv_buf, p["conv_w"][0], p["conv_b"][0], p["conv_ln_g"][0], p["conv_ln_b"][0])
    new_buf = jnp.concatenate([conv_buf.astype(F32), glu3], axis=1)[:, -(CONV_W - 1):]
    shp = (batch, t, SB_HEADS, HEAD_DIM)
    if prompt:
        o = _sb_prompt(res[3], res[4], res[5])
        k_out, v_out = _from_transposed(res[1]), _from_transposed(res[2])
    else:
        cache_k, cache_v, page_table = sample_ctx
        k32, v32 = res[1], res[2]
        o = _sb_sample(res[3].reshape(batch, t, SB_DIM), k32.reshape(batch, t, SB_DIM),
                       v32.reshape(batch, t, SB_DIM), cache_k, cache_v, page_table)
        k_out, v_out = k32.reshape(shp), v32.reshape(shp)
    h2 = _proj_res(h2, [y_conv.reshape(batch * t, CONV_CH), o.reshape(batch * t, SB_DIM)], p["w_mix_out"][0])
    return h2, new_buf, k_out, v_out


def _odd_layer(h2, batch, p, sample_ctx):
    t = h2.shape[0] // batch
    prompt = sample_ctx is None
    past = 0 if prompt else sample_ctx["past"]
    tm = _row_tile(h2.shape[0])
    pos = past + jnp.arange(max(t, tm), dtype=jnp.int32) % t
    cos_t, sin_t = _rope_tables(pos, 128)
    res = _odd_in(h2, p["norm_mix"][1][None], p["w_in_odd"], cos_t, sin_t, batch, prompt)
    ck, cv = res[0], res[1]
    nb = -(-(past + t) // CMP_BLK)
    cw = (p["cmp_pe_k"][0], p["cmp_w1_k"][0], p["cmp_w2_k"][0]), (p["cmp_pe_v"][0], p["cmp_w1_v"][0], p["cmp_w2_v"][0])
    width = CMP_BLK * NSA_KV_DIM

    def end_tables(first, count):
        blk_end = (first + jnp.arange(count, dtype=jnp.int32)) * CMP_BLK + (CMP_BLK - 1)
        return tuple(jnp.tile(x, (batch, 1)) for x in _rope_tables(blk_end, 128))

    if prompt:
        kc = _compress(ck.reshape(batch * nb, width), *cw[0], end_tables(0, nb), True)
        vc = _compress(cv.reshape(batch * nb, width), *cw[1], None, True)
    else:
        nbp = past // CMP_BLK
        assert past % CMP_BLK == 0 and nb == nbp + 1
        pt = sample_ctx["page_table"]
        new_block = lambda x: jnp.pad(x.reshape(batch, t, NSA_KV_DIM), ((0, 0), (0, CMP_BLK - t), (0, 0))).reshape(batch, width)
        halves = []
        for cache, new, w, tabs in ((sample_ctx["cmp_k"], ck, cw[0], (end_tables(0, nbp), end_tables(nbp, 1))),
                                    (sample_ctx["cmp_v"], cv, cw[1], (None, None))):
            x_past = _cmp_gather(cache, w[0], pt).reshape(batch * nbp, width)
            c_past = _compress(x_past, *w, tabs[0], False).reshape(batch, nbp, NSA_KV_DIM)
            c_new = _compress(new_block(new), *w, tabs[1], True).reshape(batch, 1, NSA_KV_DIM)
            halves.append(jnp.concatenate([c_past, c_new], axis=1))
        kc, vc = halves
    kc = kc.reshape(batch, nb, NSA_KV_DIM)
    vc = vc.reshape(batch, nb, NSA_KV_DIM)
    padc = lambda x: jnp.pad(x, ((0, 0), (0, NB_PAD - nb), (0, 0))).astype(BF16)
    kvshape = (batch, t, NSA_KVH, HEAD_DIM)
    if prompt:
        hm = lambda x: jnp.transpose(padc(x).reshape(batch, NB_PAD, NSA_KVH, HEAD_DIM), (0, 2, 1, 3))
        t32 = res[2:8]
        q_t, gates_t, sk_hm, sv_t, wk_hm, wv_t = res[8:]
        o = _nsa_prompt(q_t, hm(kc), jnp.swapaxes(hm(vc), 2, 3), sk_hm, sv_t, wk_hm, wv_t, gates_t, nb)
        keep = min(WINDOW, t)
        outs = [_from_transposed(x) for x in t32[:4]] + [_from_transposed(x[..., -keep:]) for x in t32[4:]]
    else:
        sk, sv, wk, wv = res[2:6]
        q, gates = res[6:]
        gates = gates.reshape(batch, t, NSA_KVH, GATE_PAD)[..., :3 * NSA_GQA]
        r3 = lambda x: x.reshape(batch, t, NSA_KV_DIM)
        wkc, wvc = sample_ctx["win_k"], sample_ctx["win_v"]
        wb = wkc.shape[1]
        o = _nsa_sample(q.reshape(batch, t, ODD_Q), padc(kc), padc(vc), gates,
                        r3(sk), r3(sv), r3(wk), r3(wv), _keys_minor(wkc), _keys_minor(wvc),
                        sample_ctx["sel_k"], sample_ctx["sel_v"], sample_ctx["page_table"], past, nb)
        r4 = lambda x: x.reshape(kvshape)
        outs = [r4(ck), r4(cv), r4(sk), r4(sv),
                jnp.concatenate([wkc, r4(wk)], axis=1)[:, -wb:], jnp.concatenate([wvc, r4(wv)], axis=1)[:, -wb:]]
    h2 = _proj_res(h2, [o.reshape(batch * t, ODD_Q)], p["w_mix_out"][1])
    return (h2,) + tuple(outs)


def _trunk(x, p, conv_buf, mem_kv, even_ctx, odd_ctx):
    batch, t, _ = x.shape
    h2 = x.reshape(batch * t, D_MODEL)
    h2, new_buf, sbk, sbv = _even_layer(h2, batch, p, conv_buf, even_ctx)
    h2 = _common_tail(h2, batch, 0, mem_kv[0][0], mem_kv[0][1], p, None)
    h2, ck, cv, sk, sv, wk, wv = _odd_layer(h2, batch, p, odd_ctx)
    h2 = _common_tail(h2, batch, 1, mem_kv[1][0], mem_kv[1][1], p, p["final_norm"][None])
    st = lambda a: a[None]
    return (h2.reshape(batch, t, D_MODEL), st(sbk), st(sbv), st(new_buf),
            st(ck), st(cv), st(sk), st(sv), st(wk), st(wv))


def kernel(x_prompt, x_sample, mem_prompt, cache_sb_k, cache_sb_v, state_conv,
           cache_nsa_cmp_k, cache_nsa_cmp_v, cache_nsa_sel_k, cache_nsa_sel_v,
           cache_nsa_win_k, cache_nsa_win_v, cache_mem_k, cache_mem_v, page_table,
           norm_mix, norm_mem, norm_ffn, final_norm, w_in_even, w_in_odd, w_mix_out,
           conv_w, conv_b, conv_ln_g, conv_ln_b,
           cmp_pe_k, cmp_w1_k, cmp_w2_k, cmp_pe_v, cmp_w1_v, cmp_w2_v,
           w_mem_q, w_mem_k, w_mem_v, w_mem_o, w_ffn_gate, w_ffn_up, w_ffn_down):
    assert norm_mix.shape[0] == 2 and w_in_even.shape[0] == 1 and w_in_odd.shape[0] == 1
    bp = x_prompt.shape[0]
    bs = x_sample.shape[0]
    bf = lambda w: w.astype(BF16)
    p = dict(
        norm_mix=norm_mix, norm_mem=norm_mem, norm_ffn=norm_ffn, final_norm=final_norm,
        w_in_even=bf(w_in_even),
        w_in_odd=_odd_in_weights(w_in_odd[0]),
        w_mix_out=bf(w_mix_out), conv_w=conv_w, conv_b=conv_b, conv_ln_g=conv_ln_g, conv_ln_b=conv_ln_b,
        cmp_pe_k=cmp_pe_k, cmp_w1_k=cmp_w1_k, cmp_w2_k=cmp_w2_k,
        cmp_pe_v=cmp_pe_v, cmp_w1_v=cmp_w1_v, cmp_w2_v=cmp_w2_v,
        w_mem_q=bf(w_mem_q), w_mem_o=bf(w_mem_o),
        w_ffn_gate=bf(w_ffn_gate), w_ffn_up=bf(w_ffn_up), w_ffn_down=bf(w_ffn_down))

    ml = mem_prompt.shape[1]
    mem = _mem_kv(mem_prompt.reshape(bp * ml, D_MODEL),
                  [bf(w_mem_k[0]), bf(w_mem_v[0]), bf(w_mem_k[1]), bf(w_mem_v[1])])
    m3 = lambda a: a.reshape(bp, ml, D_MODEL)
    m4 = lambda a: a.reshape(bp, ml, MEM_HEADS, MEM_HD)
    mem_kv_p = [(m3(mem[1]), m3(mem[3])), (m3(mem[5]), m3(mem[7]))]
    mem_k_p = jnp.stack([m4(mem[0]), m4(mem[4])])
    mem_v_p = jnp.stack([m4(mem[2]), m4(mem[6])])
    zero_buf = jnp.zeros((bp, CONV_W - 1, CONV_CH), F32)
    outs_p = _trunk(x_prompt, p, zero_buf, mem_kv_p, None, None)

    past = page_table.shape[1] * PAGE
    even_ctx = (_keys_minor(cache_sb_k[0]), _keys_minor(cache_sb_v[0]), page_table)
    odd_ctx = dict(past=past, page_table=page_table,
                   cmp_k=_keys_minor(cache_nsa_cmp_k[0]), cmp_v=_keys_minor(cache_nsa_cmp_v[0]),
                   sel_k=_keys_minor(cache_nsa_sel_k[0]), sel_v=_keys_minor(cache_nsa_sel_v[0]),
                   win_k=cache_nsa_win_k[0], win_v=cache_nsa_win_v[0])
    mem_kv_s = [(cache_mem_k, cache_mem_v)] * cache_mem_k.shape[0]
    outs_s = _trunk(x_sample, p, state_conv[0], mem_kv_s, even_ctx, odd_ctx)

    return (outs_p[0], outs_s[0]) + tuple(outs_p[1:]) + (mem_k_p, mem_v_p) + tuple(outs_s[1:])
```

```python
import functools
import math

import jax
import jax.numpy as jnp
from jax import lax
from jax.experimental import pallas as pl
from jax.experimental.pallas import tpu as pltpu

F32 = jnp.float32
BF16 = jnp.bfloat16

D_MODEL = 1024
HEAD_DIM = 64
CONV_CH = 512
CONV_W = 31
SB_HEADS = 8
SB_DIM = 512
NSA_HEADS = 16
NSA_GQA = 4
NSA_KVH = 4
NSA_KV_DIM = 256
CMP_BLK = 64
CMP_HID = 256
SEL_TOPN = 16
FORCE_SCORE = 1.0e4
WINDOW = 512
MEM_HEADS = 4
MEM_HD = 256
D_FF = 2816
ROPE_THETA = 10000.0
NORM_EPS = 1e-6
NEG_INF = -1e30
PAGE = 128

V7X_VMEM_BYTES = 64 * 1024 * 1024
VMEM_LIMIT = V7X_VMEM_BYTES - 8 * 1024 * 1024
ATT_TILE = 256
NSA_TILE = 512


def _cp(sem):
    return pltpu.CompilerParams(dimension_semantics=sem, vmem_limit_bytes=VMEM_LIMIT)


def _nt(a, b):
    return lax.dot_general(a, b, (((1,), (1,)), ((), ())), preferred_element_type=F32)


def _dot(a, b):
    return jnp.dot(a, b, preferred_element_type=F32)


def _rms(x, g):
    y = x * lax.rsqrt(jnp.mean(x * x, axis=-1, keepdims=True) + NORM_EPS)
    return y * g


def _sigmoid(x):
    return 1.0 / (1.0 + jnp.exp(-x))


def _const_spec(shape):
    n = len(shape)
    return pl.BlockSpec(shape, lambda *a: (0,) * n)


def _row_spec(tm, n):
    return pl.BlockSpec((tm, n), lambda i: (i, 0))


def _hm_spec(heads, tm, width, nt):
    return pl.BlockSpec((1, heads, tm, width), lambda i: (i // nt, 0, i % nt, 0))


def _tr_spec(heads, tm, nt):
    return pl.BlockSpec((1, heads, HEAD_DIM, tm), lambda i: (i // nt, 0, 0, i % nt))


def _store_transposed(x, heads, f32_ref, bf16_ref):
    xt = x.T.reshape(heads, HEAD_DIM, x.shape[0])
    if f32_ref is not None:
        f32_ref[0] = xt
    if bf16_ref is not None:
        bf16_ref[0] = xt.astype(BF16)


def _store_head_major(x, heads, bf16_ref):
    for hh in range(heads):
        bf16_ref[0, hh] = x[:, hh * HEAD_DIM:(hh + 1) * HEAD_DIM].astype(BF16)


def _from_transposed(xt):
    return jnp.transpose(xt, (0, 3, 1, 2))


def _row_tile(m):
    return 512 if m >= 4096 else min(256, m)


def _rope_tables(pos, width):
    half = HEAD_DIM // 2
    inv = ROPE_THETA ** (-jnp.arange(half, dtype=F32) / half)
    ang = pos.astype(F32)[:, None] * inv[None, :]
    cos = jnp.cos(ang)
    sin = jnp.sin(ang)
    c = jnp.concatenate([cos, cos], axis=-1)
    s = jnp.concatenate([-sin, sin], axis=-1)
    reps = width // HEAD_DIM
    return jnp.tile(c, (1, reps)), jnp.tile(s, (1, reps))


def _rope128(x, c, s):
    lane = lax.broadcasted_iota(jnp.int32, x.shape, 1)
    first = (lane % HEAD_DIM) < (HEAD_DIM // 2)
    rot = jnp.where(first, pltpu.roll(x, 96, 1), pltpu.roll(x, 32, 1))
    return x * c + rot * s


def _rope_wide(x, c, s):
    n = x.shape[1] // 128
    return jnp.concatenate([_rope128(x[:, i * 128:(i + 1) * 128], c, s) for i in range(n)], axis=1)


def _even_in_body(head_major, h_ref, g_ref, w_ref, glu_ref, k32_ref, v32_ref, q_ref, *hm_refs):
    xn = _rms(h_ref[...], g_ref[...]).astype(BF16)

    def mm(lo, hi):
        return _dot(xn, w_ref[:, lo:hi])

    glu_ref[...] = mm(0, CONV_CH) * _sigmoid(mm(CONV_CH, 2 * CONV_CH))
    base = 2 * CONV_CH
    q = mm(base, base + SB_DIM) * (HEAD_DIM ** -0.5)
    k = mm(base + SB_DIM, base + 2 * SB_DIM)
    v = mm(base + 2 * SB_DIM, base + 3 * SB_DIM)
    if head_major:
        kh_ref, vh_ref = hm_refs
        _store_transposed(q, SB_HEADS, None, q_ref)
        _store_transposed(k, SB_HEADS, k32_ref, None)
        _store_head_major(k, SB_HEADS, kh_ref)
        _store_transposed(v, SB_HEADS, v32_ref, vh_ref)
    else:
        k32_ref[...] = k
        v32_ref[...] = v
        q_ref[...] = q.astype(BF16)


def _even_in(h2, g, w, batch, head_major):
    m = h2.shape[0]
    t = m // batch
    tm = _row_tile(m)
    nt = max(t // tm, 1)
    outs = [jax.ShapeDtypeStruct((m, CONV_CH), F32)]
    specs = [_row_spec(tm, CONV_CH)]
    if head_major:
        kt = lambda dt: jax.ShapeDtypeStruct((batch, SB_HEADS, HEAD_DIM, t), dt)
        outs += [kt(F32), kt(F32), kt(BF16), jax.ShapeDtypeStruct((batch, SB_HEADS, t, HEAD_DIM), BF16), kt(BF16)]
        specs += [_tr_spec(SB_HEADS, tm, nt)] * 3 + [_hm_spec(SB_HEADS, tm, HEAD_DIM, nt), _tr_spec(SB_HEADS, tm, nt)]
    else:
        outs += [jax.ShapeDtypeStruct((m, SB_DIM), F32)] * 2 + [jax.ShapeDtypeStruct((m, SB_DIM), BF16)]
        specs += [_row_spec(tm, SB_DIM)] * 3
    return pl.pallas_call(
        functools.partial(_even_in_body, head_major),
        grid=(m // tm,),
        in_specs=[_row_spec(tm, D_MODEL), _const_spec((1, D_MODEL)), _const_spec(w.shape)],
        out_specs=specs, out_shape=outs,
        compiler_params=_cp(("parallel",)), name="even_in",
    )(h2, g, w)


CONV_PAD = 32


def _conv_body(t, chunk, hp_ref, w_ref, b_ref, lg_ref, lb_ref, o_ref):
    win_rows = chunk + CONV_PAD

    def one_chunk(c, _):
        base = pl.multiple_of(c * chunk, chunk)
        win = hp_ref[0, pl.ds(base, win_rows), :]
        acc = jnp.zeros((chunk, CONV_CH), F32)
        for r in range(8):
            sh = win if r == 0 else pltpu.roll(win, win_rows - r, 0)
            for a in range(4):
                tap = 8 * a + r
                if tap < CONV_W:
                    acc = acc + sh[8 * a:8 * a + chunk] * w_ref[tap:tap + 1, :]
        y = acc + b_ref[...]
        mu = jnp.mean(y, axis=-1, keepdims=True)
        var = jnp.mean(jnp.square(y - mu), axis=-1, keepdims=True)
        y = (y - mu) * lax.rsqrt(var + NORM_EPS) * lg_ref[...] + lb_ref[...]
        o_ref[0, pl.ds(base, chunk), :] = (y * _sigmoid(y)).astype(o_ref.dtype)
        return 0

    lax.fori_loop(0, t // chunk, one_chunk, 0)


def _conformer_conv(glu, buf, w_dw, b_dw, ln_g, ln_b):
    b, t, c = glu.shape
    hp = jnp.concatenate([buf.astype(F32), glu, jnp.zeros((b, CONV_PAD - (CONV_W - 1), c), F32)], axis=1)
    chunk = min(t, 128)
    return pl.pallas_call(
        functools.partial(_conv_body, t, chunk),
        grid=(b,),
        in_specs=[pl.BlockSpec((1, t + CONV_PAD, c), lambda i: (i, 0, 0)),
                  _const_spec((CONV_PAD, c)), _const_spec((1, c)), _const_spec((1, c)), _const_spec((1, c))],
        out_specs=pl.BlockSpec((1, t, c), lambda i: (i, 0, 0)),
        out_shape=jax.ShapeDtypeStruct((b, t, c), BF16),
        compiler_params=_cp(("parallel",)), name="conformer_conv",
    )(hp, jnp.pad(w_dw, ((0, CONV_PAD - CONV_W), (0, 0))), b_dw[None], ln_g[None], ln_b[None])


def _softplus(z):
    return jnp.maximum(z, 0.0) + jnp.log(1.0 + jnp.exp(-jnp.abs(z)))


def _sb_tiles(zs, carries, valid, tri2, keys_axis):
    valids = valid if isinstance(valid, (list, tuple)) else [valid] * len(zs)
    sps = []
    for z, ok in zip(zs, valids):
        sp = _softplus(z)
        sps.append(sp if ok is None else jnp.where(ok, sp, 0.0))
    laters = []
    for sp in sps:
        hi = sp.astype(BF16)
        lo = (sp - hi.astype(F32)).astype(BF16)
        split = jnp.concatenate([hi, lo], axis=keys_axis)
        laters.append(_dot(split, tri2) if keys_axis == 1 else _dot(tri2, split))
    first = (lambda a: a[:, 0:1]) if keys_axis == 1 else (lambda a: a[0:1, :])
    ws, new = [], []
    for n, (z, sp, later, ok) in enumerate(zip(zs, sps, laters, valids)):
        carry = new[n - len(carries)] if n >= len(carries) else carries[n]
        w = jnp.exp(z - sp - later - carry)
        ws.append((w if ok is None else jnp.where(ok, w, 0.0)).astype(BF16))
        new.append(carry + first(later) + first(sp))
    return ws, new


def _tri2(tk, keys_axis):
    s = jnp.arange(tk)[:, None]
    j = jnp.arange(tk)[None, :]
    if keys_axis == 1:
        tri = (s > j).astype(BF16)
        return jnp.concatenate([tri, tri], axis=0)
    tri = (j > s).astype(BF16)
    return jnp.concatenate([tri, tri], axis=1)


def _sb_prompt_body(q_ref, k_ref, v_ref, tri_ref, o_ref):
    i = pl.program_id(2)
    tq = q_ref.shape[3]
    key = lax.broadcasted_iota(jnp.int32, (tq, tq), 0)
    qry = lax.broadcasted_iota(jnp.int32, (tq, tq), 1)
    diag_valid = key < qry
    tri2 = tri_ref[...]
    nh = q_ref.shape[1]

    def tiles(js, st, valid):
        starts = [pl.multiple_of(j * tq, tq) for j in js]
        zs = [_dot(k_ref[0, hh, pl.ds(s0, tq), :], q_ref[0, hh]) for s0 in starts for hh in range(nh)]
        ws, carries = _sb_tiles(zs, [s[0] for s in st], valid, tri2, 0)
        accs = [s[1] for s in st]
        for n, s0 in enumerate(starts):
            for hh in range(nh):
                accs[hh] = accs[hh] + _dot(v_ref[0, hh, :, pl.ds(s0, tq)], ws[n * nh + hh])
        return tuple(zip(carries[-nh:], accs))

    init = tuple((jnp.zeros((1, tq), F32), jnp.zeros((HEAD_DIM, tq), F32)) for _ in range(nh))
    st = tiles([i], init, diag_valid)
    odd = i % 2
    st = lax.cond(odd == 1, lambda s: tiles([i - 1], s, None), lambda s: s, st)
    top = i - 1 - odd
    st = lax.fori_loop(0, i // 2, lambda n, s: tiles([top - 2 * n, top - 2 * n - 1], s, None), st)
    o_ref[0] = jnp.concatenate([s[1] for s in st], axis=0).T.astype(o_ref.dtype)


SB_HEADS_PER_STEP = 4


def _sb_prompt(q_t, k_hm, v_t):
    b, h, d, t = q_t.shape
    tq = min(ATT_TILE, t)
    nh = SB_HEADS_PER_STEP
    return pl.pallas_call(
        _sb_prompt_body,
        grid=(b, h // nh, t // tq),
        in_specs=[pl.BlockSpec((1, nh, d, tq), lambda bi, hp, i: (bi, hp, 0, i)),
                  pl.BlockSpec((1, nh, t, d), lambda bi, hp, i: (bi, hp, 0, 0)),
                  pl.BlockSpec((1, nh, d, t), lambda bi, hp, i: (bi, hp, 0, 0)),
                  _const_spec((tq, 2 * tq))],
        out_specs=pl.BlockSpec((1, tq, nh * d), lambda bi, hp, i: (bi, i, hp)),
        out_shape=jax.ShapeDtypeStruct((b, t, h * d), BF16),
        compiler_params=_cp(("parallel", "parallel", "arbitrary")), name="sb_prompt",
    )(q_t, k_hm, v_t, _tri2(tq, 0))


def _block_diag_rows(q, groups):
    b, t, h, d = q.shape
    per = h // groups
    qg = jnp.transpose(q.reshape(b, t, groups, per, d), (0, 2, 3, 1, 4))
    eye = jnp.eye(groups, dtype=q.dtype)[None, :, None, None, :, None]
    return (qg[:, :, :, :, None, :] * eye).reshape(b, h * t, groups * d)


def _diag_rows_out(o, t, groups):
    b, rows, gd = o.shape
    d = gd // groups
    per = rows // (groups * t)
    o6 = o.reshape(b, groups, per, t, groups, d)
    idx = jnp.arange(groups)
    og = o6[:, idx, :, :, idx, :]
    return jnp.transpose(og, (1, 3, 0, 2, 4)).reshape(b, t, groups * per * d)


def _sb_sample_body(npages, t_new, pt_ref, qbd_ref, kn_ref, vn_ref, tri_ref, *refs):
    k_refs = refs[:npages]
    v_refs = refs[npages:2 * npages]
    o_ref = refs[2 * npages]
    qbd = qbd_ref[0]
    rows = qbd.shape[0]
    tri2 = tri_ref[...]
    width = kn_ref.shape[2]
    pad = jnp.zeros((PAGE - t_new, width), F32)
    kn = jnp.concatenate([kn_ref[0], pad], axis=0).astype(BF16)
    vn = jnp.concatenate([vn_ref[0], pad], axis=0).astype(BF16)
    qoff = lax.broadcasted_iota(jnp.int32, (rows, PAGE), 0) % t_new
    col = lax.broadcasted_iota(jnp.int32, (rows, PAGE), 1)
    order = list(reversed(range(npages)))
    zs = [_nt(qbd, kn)] + [_dot(qbd, k_refs[p][0].reshape(width, PAGE).astype(BF16)) for p in order]
    ws, _ = _sb_tiles(zs, [jnp.zeros((rows, 1), F32)], [col < qoff] + [None] * npages, tri2, 1)
    acc = _dot(ws[0], vn)
    for w, p in zip(ws[1:], order):
        acc = acc + _nt(w, v_refs[p][0].reshape(width, PAGE).astype(BF16))
    o_ref[0] = acc


def _page_specs(npages, heads):
    return [pl.BlockSpec((1, heads, HEAD_DIM, PAGE), lambda b, pt, p=p: (pt[b, p], 0, 0, 0))
            for p in range(npages)]


def _keys_minor(cache):
    nd = cache.ndim
    return jnp.transpose(cache, tuple(range(nd - 3)) + (nd - 2, nd - 1, nd - 3))


def _sb_sample(q, k_new, v_new, cache_k, cache_v, page_table):
    b, t, w = k_new.shape
    npages = page_table.shape[1]
    qbd = _block_diag_rows(q.reshape(b, t, SB_HEADS, HEAD_DIM), SB_HEADS)
    rows = SB_HEADS * t
    bspec = lambda r, c: pl.BlockSpec((1, r, c), lambda bi, pt: (bi, 0, 0))
    grid_spec = pltpu.PrefetchScalarGridSpec(
        num_scalar_prefetch=1, grid=(b,),
        in_specs=[bspec(rows, w), bspec(t, w), bspec(t, w),
                  pl.BlockSpec((2 * PAGE, PAGE), lambda bi, pt: (0, 0))]
                 + _page_specs(npages, SB_HEADS) + _page_specs(npages, SB_HEADS),
        out_specs=bspec(rows, w))
    o = pl.pallas_call(
        functools.partial(_sb_sample_body, npages, t),
        grid_spec=grid_spec,
        out_shape=jax.ShapeDtypeStruct((b, rows, w), F32),
        compiler_params=_cp(("parallel",)), name="sb_sample",
    )(page_table, qbd, k_new, v_new, _tri2(PAGE, 1), *([cache_k] * npages), *([cache_v] * npages))
    return _diag_rows_out(o, t, SB_HEADS)


def _proj_res_body(n_in, *refs):
    h_ref = refs[0]
    x_refs = refs[1:1 + n_in]
    w_ref = refs[1 + n_in]
    o_ref = refs[2 + n_in]
    acc = h_ref[...]
    off = 0
    for x_ref in x_refs:
        k = x_ref.shape[1]
        acc = acc + _dot(x_ref[...].astype(BF16), w_ref[off:off + k, :])
        off += k
    o_ref[...] = acc


def _proj_res(h2, xs, w):
    m = h2.shape[0]
    tm = _row_tile(m)
    return pl.pallas_call(
        functools.partial(_proj_res_body, len(xs)),
        grid=(m // tm,),
        in_specs=[_row_spec(tm, D_MODEL)] + [_row_spec(tm, x.shape[1]) for x in xs] + [_const_spec(w.shape)],
        out_specs=_row_spec(tm, D_MODEL),
        out_shape=jax.ShapeDtypeStruct((m, D_MODEL), F32),
        compiler_params=_cp(("parallel",)), name="proj_res",
    )(h2, *xs, w)


def _norm_proj_body(scale, h_ref, g_ref, w_ref, o_ref):
    xn = _rms(h_ref[...], g_ref[...]).astype(BF16)
    o_ref[...] = (_dot(xn, w_ref[...]) * scale).astype(o_ref.dtype)


def _norm_proj(h2, g, w, scale):
    m = h2.shape[0]
    tm = _row_tile(m)
    return pl.pallas_call(
        functools.partial(_norm_proj_body, scale),
        grid=(m // tm,),
        in_specs=[_row_spec(tm, D_MODEL), _const_spec((1, D_MODEL)), _const_spec(w.shape)],
        out_specs=_row_spec(tm, w.shape[1]),
        out_shape=jax.ShapeDtypeStruct((m, w.shape[1]), BF16),
        compiler_params=_cp(("parallel",)), name="norm_proj",
    )(h2, g, w)


def _mem_kv_body(x_ref, wk0, wv0, wk1, wv1, *o_refs):
    x = x_ref[...].astype(BF16)
    for n, w_ref in enumerate((wk0, wv0, wk1, wv1)):
        y = _dot(x, w_ref[...])
        o_refs[2 * n][...] = y
        o_refs[2 * n + 1][...] = y.astype(BF16)


def _mem_kv(x2, ws):
    m = x2.shape[0]
    tm = _row_tile(m)
    outs, specs = [], []
    for _ in ws:
        outs += [jax.ShapeDtypeStruct((m, D_MODEL), F32), jax.ShapeDtypeStruct((m, D_MODEL), BF16)]
        specs += [_row_spec(tm, D_MODEL)] * 2
    return pl.pallas_call(
        _mem_kv_body, grid=(m // tm,),
        in_specs=[_row_spec(tm, D_MODEL)] + [_const_spec(w.shape) for w in ws],
        out_specs=specs, out_shape=outs,
        compiler_params=_cp(("parallel",)), name="mem_kv",
    )(x2, *ws)


def _mem_block_body(h_ref, g_ref, wq_ref, k_ref, v_ref, wo_ref, o_ref):
    h = h_ref[...]
    q = (_dot(_rms(h, g_ref[...]).astype(BF16), wq_ref[...]) * (MEM_HD ** -0.5)).astype(BF16)
    outs = []
    for hh in range(MEM_HEADS):
        sl = slice(hh * MEM_HD, (hh + 1) * MEM_HD)
        s = _nt(q[:, sl], k_ref[0, :, sl])
        m = jnp.max(s, axis=-1, keepdims=True)
        e = jnp.exp(s - m)
        p = e / jnp.sum(e, axis=-1, keepdims=True)
        outs.append(_dot(p.astype(BF16), v_ref[0, :, sl]).astype(BF16))
    o_ref[...] = h + _dot(jnp.concatenate(outs, axis=1), wo_ref[...])


def _mem_block(h2, g, wq, mk, mv, wo, batch):
    m = h2.shape[0]
    tm = _row_tile(m)
    nt = (m // batch) // tm
    ml = mk.shape[1]
    kv_spec = pl.BlockSpec((1, ml, D_MODEL), lambda i: (i // nt, 0, 0))
    return pl.pallas_call(
        _mem_block_body, grid=(m // tm,),
        in_specs=[_row_spec(tm, D_MODEL), _const_spec((1, D_MODEL)), _const_spec(wq.shape), kv_spec, kv_spec,
                  _const_spec(wo.shape)],
        out_specs=_row_spec(tm, D_MODEL),
        out_shape=jax.ShapeDtypeStruct((m, D_MODEL), F32),
        compiler_params=_cp(("parallel",)), name="mem_block",
    )(h2, g, wq, mk, mv, wo)


def _mem_attn_cached_body(q_ref, k_ref, v_ref, o_ref):
    ml, heads, hd = k_ref.shape[2:]
    k = k_ref[0, 0].reshape(ml * heads, hd).astype(BF16)
    v = v_ref[0, 0].reshape(ml * heads, hd).astype(BF16)
    q = q_ref[0]
    s = _nt(k, q)
    t = q.shape[0] // heads
    same = (lax.broadcasted_iota(jnp.int32, s.shape, 0) % heads) == (lax.broadcasted_iota(jnp.int32, s.shape, 1) // t)
    sm = jnp.where(same, s, NEG_INF)
    m = jnp.max(sm, axis=0, keepdims=True)
    e = jnp.where(same, jnp.exp(sm - m), 0.0)
    p = (e / jnp.sum(e, axis=0, keepdims=True)).astype(BF16)
    o_ref[0] = lax.dot_general(p, v, (((0,), (0,)), ((), ())), preferred_element_type=F32).astype(o_ref.dtype)


def _mem_attn_cached(q, cache_k, cache_v, layer):
    b, t, d = q.shape
    _, _, ml, heads, hd = cache_k.shape
    rows = heads * t
    q_rows = jnp.transpose(q.reshape(b, t, heads, hd), (0, 2, 1, 3)).reshape(b, rows, hd)
    kv_spec = pl.BlockSpec((1, 1, ml, heads, hd), lambda bi: (layer, bi, 0, 0, 0))
    o = pl.pallas_call(
        _mem_attn_cached_body, grid=(b,),
        in_specs=[pl.BlockSpec((1, rows, hd), lambda bi: (bi, 0, 0)), kv_spec, kv_spec],
        out_specs=pl.BlockSpec((1, rows, hd), lambda bi: (bi, 0, 0)),
        out_shape=jax.ShapeDtypeStruct((b, rows, hd), BF16),
        compiler_params=_cp(("parallel",)), name="mem_attn_cached",
    )(q_rows, cache_k, cache_v)
    return jnp.transpose(o.reshape(b, heads, t, hd), (0, 2, 1, 3)).reshape(b, t, d)


FFN_CHUNK = 512


def _ffn_body(final, h_ref, g_ref, wg_ref, wu_ref, wd_ref, *rest):
    if final:
        gf_ref, o_ref = rest
    else:
        (o_ref,) = rest
    h = h_ref[...]
    xn = _rms(h, g_ref[...]).astype(BF16)
    acc = h
    for lo in range(0, D_FF, FFN_CHUNK):
        hi = min(lo + FFN_CHUNK, D_FF)
        gate = _dot(xn, wg_ref[:, lo:hi])
        up = _dot(xn, wu_ref[:, lo:hi])
        act = (gate * _sigmoid(gate) * up).astype(BF16)
        acc = acc + _dot(act, wd_ref[lo:hi, :])
    if final:
        acc = _rms(acc, gf_ref[...])
    o_ref[...] = acc


def _ffn(h2, g, wg, wu, wd, final_g=None):
    m = h2.shape[0]
    tm = _row_tile(m)
    final = final_g is not None
    once = lambda shape: pl.BlockSpec(shape, lambda i: (0, 0), pipeline_mode=pl.Buffered(1))
    in_specs = [_row_spec(tm, D_MODEL), _const_spec((1, D_MODEL)), once(wg.shape), once(wu.shape), once(wd.shape)]
    args = [h2, g, wg, wu, wd]
    if final:
        in_specs.append(_const_spec((1, D_MODEL)))
        args.append(final_g)
    return pl.pallas_call(
        functools.partial(_ffn_body, final), grid=(m // tm,),
        in_specs=in_specs, out_specs=_row_spec(tm, D_MODEL),
        out_shape=jax.ShapeDtypeStruct((m, D_MODEL), F32),
        compiler_params=_cp(("parallel",)), name="ffn",
    )(*args)


ODD_Q = NSA_HEADS * HEAD_DIM
ODD_GATES = 3 * NSA_HEADS
ODD_IN = ODD_Q + 6 * NSA_KV_DIM + ODD_GATES
GATE_PAD = 16
GATE_LANES = NSA_KVH * GATE_PAD
ODD_IN_PAD = ODD_Q + 6 * NSA_KV_DIM + 128


def _odd_in_weights(w_in_odd):
    gates = w_in_odd[:, ODD_Q + 6 * NSA_KV_DIM:].reshape(D_MODEL, NSA_KVH, 3 * NSA_GQA)
    gates = jnp.pad(gates, ((0, 0), (0, 0), (0, GATE_PAD - 3 * NSA_GQA))).reshape(D_MODEL, GATE_LANES)
    return jnp.concatenate([w_in_odd[:, :ODD_Q + 6 * NSA_KV_DIM],
                            jnp.pad(gates, ((0, 0), (0, 128 - GATE_LANES)))], axis=1).astype(BF16)


def _odd_in_body(head_major, h_ref, g_ref, w_ref, c_ref, s_ref, *o_refs):
    xn = _rms(h_ref[...], g_ref[...]).astype(BF16)
    c = c_ref[...]
    s = s_ref[...]

    def mm(lo, hi):
        return _dot(xn, w_ref[:, lo:hi])

    kv = lambda n: mm(ODD_Q + n * NSA_KV_DIM, ODD_Q + (n + 1) * NSA_KV_DIM)
    q = _rope_wide(mm(0, ODD_Q), c, s) * (HEAD_DIM ** -0.5)
    ck, cv = kv(0), kv(1)
    sk, sv = _rope_wide(kv(2), c, s), kv(3)
    wk, wv = _rope_wide(kv(4), c, s), kv(5)
    gates = _sigmoid(mm(ODD_Q + 6 * NSA_KV_DIM, ODD_IN_PAD))[:, :GATE_LANES]
    if head_major:
        ck_rows, cv_rows = o_refs[:2]
        t32 = o_refs[2:8]
        q_ref, g_out, sk_rows, sv_t, wk_rows, wv_t = o_refs[8:14]
        ck_rows[...] = ck
        cv_rows[...] = cv
        for val, r32, r16 in zip((ck, cv, sk, sv, wk, wv), t32, (None, None, None, sv_t, None, wv_t)):
            _store_transposed(val, NSA_KVH, r32, r16)
        _store_transposed(q, NSA_HEADS, None, q_ref)
        _store_head_major(sk, NSA_KVH, sk_rows)
        _store_head_major(wk, NSA_KVH, wk_rows)
        g_out[0] = gates.T.reshape(NSA_KVH, GATE_PAD, gates.shape[0])
    else:
        for ref, val in zip(o_refs[:6], (ck, cv, sk, sv, wk, wv)):
            ref[...] = val
        q_ref, g_out = o_refs[6:]
        q_ref[...] = q.astype(BF16)
        g_out[...] = gates


def _odd_in(h2, g, w, cos_t, sin_t, batch, head_major):
    m = h2.shape[0]
    t = m // batch
    tm = _row_tile(m)
    nt = max(t // tm, 1)
    ntab = cos_t.shape[0] // tm
    tab_spec = pl.BlockSpec((tm, 128), lambda i: (i % ntab, 0))
    kv = jax.ShapeDtypeStruct((m, NSA_KV_DIM), F32)
    if head_major:
        kt = lambda dt: jax.ShapeDtypeStruct((batch, NSA_KVH, HEAD_DIM, t), dt)
        rows16 = jax.ShapeDtypeStruct((batch, NSA_KVH, t, HEAD_DIM), BF16)
        outs = [kv] * 2 + [kt(F32)] * 6
        outs += [jax.ShapeDtypeStruct((batch, NSA_HEADS, HEAD_DIM, t), BF16),
                 jax.ShapeDtypeStruct((batch, NSA_KVH, GATE_PAD, t), F32), rows16, kt(BF16), rows16, kt(BF16)]
        rows_spec = _hm_spec(NSA_KVH, tm, HEAD_DIM, nt)
        specs = [_row_spec(tm, NSA_KV_DIM)] * 2 + [_tr_spec(NSA_KVH, tm, nt)] * 6
        specs += [_tr_spec(NSA_HEADS, tm, nt),
                  pl.BlockSpec((1, NSA_KVH, GATE_PAD, tm), lambda i: (i // nt, 0, 0, i % nt)),
                  rows_spec, _tr_spec(NSA_KVH, tm, nt), rows_spec, _tr_spec(NSA_KVH, tm, nt)]
    else:
        outs = [kv] * 6 + [jax.ShapeDtypeStruct((m, ODD_Q), BF16), jax.ShapeDtypeStruct((m, GATE_LANES), F32)]
        specs = [_row_spec(tm, NSA_KV_DIM)] * 6 + [_row_spec(tm, ODD_Q), _row_spec(tm, GATE_LANES)]
    return pl.pallas_call(
        functools.partial(_odd_in_body, head_major), grid=(m // tm,),
        in_specs=[_row_spec(tm, D_MODEL), _const_spec((1, D_MODEL)), _const_spec(w.shape), tab_spec, tab_spec],
        out_specs=specs, out_shape=outs,
        compiler_params=_cp(("parallel",)), name="odd_in",
    )(h2, g, w, cos_t, sin_t)


def _gelu_tanh(x):
    return 0.5 * x * (1.0 + jnp.tanh(math.sqrt(2.0 / math.pi) * (x + 0.044715 * x * x * x)))


def _compress_body(add_pe, use_rope, x_ref, pe_ref, w1_ref, w2_ref, c_ref, s_ref, o_ref, acc_ref, wbd_ref):
    l = pl.program_id(0)

    @pl.when(l == 0)
    def _():
        acc_ref[...] = jnp.zeros_like(acc_ref)
        wbd_ref[...] = jnp.zeros_like(wbd_ref)

    for gg in range(NSA_KVH):
        wbd_ref[gg * HEAD_DIM:(gg + 1) * HEAD_DIM, gg * CMP_HID:(gg + 1) * CMP_HID] = w1_ref[0]
    x = x_ref[...]
    if add_pe:
        x = (x + pe_ref[0]).astype(BF16)
    acc_ref[...] += _dot(x, wbd_ref[...])

    @pl.when(l == CMP_BLK - 1)
    def _():
        hid = _gelu_tanh(acc_ref[...]).astype(BF16)
        y = _dot(hid, w2_ref[...])
        if use_rope:
            y = _rope_wide(y, c_ref[...], s_ref[...])
        o_ref[...] = y


def _compress(x2, pe, w1, w2, tables, add_pe):
    rows = x2.shape[0]
    pe_t = jnp.tile(pe, (1, NSA_KVH))[:, None, :]
    w2bd = jnp.kron(jnp.eye(NSA_KVH, dtype=F32), w2).astype(BF16)
    use_rope = tables is not None
    if use_rope:
        c_t, s_t = tables
    else:
        c_t = s_t = jnp.zeros((rows, 128), F32)
    return pl.pallas_call(
        functools.partial(_compress_body, add_pe, use_rope), grid=(CMP_BLK,),
        in_specs=[pl.BlockSpec((rows, NSA_KV_DIM), lambda l: (0, l)),
                  pl.BlockSpec((1, 1, NSA_KV_DIM), lambda l: (l, 0, 0)),
                  pl.BlockSpec((1, HEAD_DIM, CMP_HID), lambda l: (l, 0, 0)),
                  _const_spec(w2bd.shape), _const_spec((rows, 128)), _const_spec((rows, 128))],
        out_specs=_const_spec((rows, NSA_KV_DIM)),
        out_shape=jax.ShapeDtypeStruct((rows, NSA_KV_DIM), F32),
        scratch_shapes=[pltpu.VMEM((rows, NSA_KVH * CMP_HID), F32),
                        pltpu.VMEM((NSA_KV_DIM, NSA_KVH * CMP_HID), BF16)],
        compiler_params=_cp(("arbitrary",)), name="compress",
    )(x2, pe_t, w1.astype(BF16), w2bd, c_t, s_t)


NB_PAD = 128
RANK_ROWS = 64


def _cmp_and_select(q, kc, vc, qpos, groups, rows_per_q, nb):
    tq = qpos.shape[0]
    rows = groups * rows_per_q * tq
    s = _nt(q, kc).reshape(groups, rows_per_q, tq, NB_PAD)
    n_idx = lax.broadcasted_iota(jnp.int32, (tq, NB_PAD), 1)
    valid = ((n_idx * CMP_BLK + (CMP_BLK - 1)) <= qpos)[None, None]
    sm = jnp.where(valid, s, NEG_INF)
    m = jnp.max(sm, axis=-1, keepdims=True)
    p = jnp.where(valid, jnp.exp(sm - m), 0.0)
    p = p / jnp.maximum(jnp.sum(p, axis=-1, keepdims=True), 1e-30)
    o_cmp = _dot(p.reshape(rows, NB_PAD).astype(BF16), vc)
    cur = qpos // CMP_BLK
    forced = ((n_idx == 0) | (n_idx == cur) | (n_idx == cur - 1))[None]
    score = jnp.where((n_idx > cur)[None], -1.0, jnp.where(forced, FORCE_SCORE, jnp.sum(p, axis=1)))
    nq = groups * tq
    nqp = -(-nq // 128) * 128
    score = score.reshape(nq, NB_PAD)
    if nqp > nq:
        score = jnp.concatenate([score, jnp.zeros((nqp - nq, NB_PAD), F32)], axis=0)
    sel_t = _top_n_mask(score.T[:RANK_ROWS], nb)
    sel = jnp.concatenate([sel_t, jnp.zeros((NB_PAD - RANK_ROWS, nqp), F32)], axis=0).T[:nq]
    sel = jnp.broadcast_to(sel.reshape(groups, 1, tq, NB_PAD), (groups, rows_per_q, tq, NB_PAD))
    return o_cmp, sel.reshape(rows, NB_PAD)


def _top_n_mask(st, nb):
    groups = RANK_ROWS // 8
    blocks = [st[8 * g:8 * g + 8] for g in range(groups)]
    cnt = [jnp.zeros_like(b) for b in blocks]
    row = lax.broadcasted_iota(jnp.int32, blocks[0].shape, 0)
    for i in range(nb):
        r = st[i:i + 1]
        for g in range(groups):
            gt = jnp.where(r > blocks[g], 1.0, 0.0)
            if 8 * g + 7 <= i:
                inc = gt
            else:
                ge = jnp.where(r >= blocks[g], 1.0, 0.0)
                inc = ge if 8 * g > i else jnp.where(row + 8 * g > i, ge, gt)
            cnt[g] = cnt[g] + inc
    return jnp.concatenate([jnp.where(c < float(SEL_TOPN), 1.0, 0.0) for c in cnt], axis=0)


def _osm(z, st, v, keys_axis, v_keys_minor):
    m, l, acc = st
    m_new = jnp.maximum(m, jnp.max(z, axis=keys_axis, keepdims=True))
    alpha = jnp.exp(m - m_new)
    p = jnp.exp(z - m_new)
    l = alpha * l + jnp.sum(p, axis=keys_axis, keepdims=True)
    pb = p.astype(BF16)
    if keys_axis == 1:
        pv = _nt(pb, v) if v_keys_minor else _dot(pb, v)
    else:
        pv = _dot(v, pb)
    return m_new, l, alpha * acc + pv


def _softmax_tiles(zs, vs):
    m = functools.reduce(jnp.maximum, [jnp.max(z, axis=1, keepdims=True) for z in zs])
    ps = [jnp.exp(z - m) for z in zs]
    l = functools.reduce(lambda a, b: a + b, [jnp.sum(p, axis=1, keepdims=True) for p in ps])
    acc = None
    for p, (v, keys_minor) in zip(ps, vs):
        pv = _nt(p.astype(BF16), v) if keys_minor else _dot(p.astype(BF16), v)
        acc = pv if acc is None else acc + pv
    return acc / jnp.maximum(l, 1e-30)


def _osm_groups_t(q_ts, k_rows, v_t, bias, sts):
    n = len(q_ts)
    z_next = _dot(k_rows, q_ts[0]) + bias
    out = []
    for r in range(n):
        z = z_next
        if r + 1 < n:
            z_next = _dot(k_rows, q_ts[r + 1]) + bias
        out.append(_osm(z, sts[r], v_t, 0, True))
    return tuple(out)


def _osm_init(rows, width, keys_axis=1):
    if keys_axis == 1:
        return (jnp.full((rows, 1), NEG_INF, F32), jnp.zeros((rows, 1), F32), jnp.zeros((rows, width), F32))
    return (jnp.full((1, rows), NEG_INF, F32), jnp.zeros((1, rows), F32), jnp.zeros((width, rows), F32))


def _osm_out(st):
    return st[2] / jnp.maximum(st[1], 1e-30)


def _nsa_prompt_body(nb, q_ref, kc_ref, vc_ref, sk_ref, sv_ref, wk_ref, wv_ref, g_ref, e_ref, o_ref):
    i = pl.program_id(2)
    tq = q_ref.shape[3]
    tk = tq
    rq = NSA_GQA
    q_ts = [q_ref[0, r] for r in range(rq)]
    qpos = i * tq + lax.broadcasted_iota(jnp.int32, (1, tq), 1)

    n_idx = lax.broadcasted_iota(jnp.int32, (NB_PAD, tq), 0)
    valid = (n_idx * CMP_BLK + (CMP_BLK - 1)) <= qpos
    kc = kc_ref[0, 0]
    vc_t = vc_ref[0, 0]
    o_cmp, score = [], None
    for r in range(rq):
        sm = jnp.where(valid, _dot(kc, q_ts[r]), NEG_INF)
        m = jnp.max(sm, axis=0, keepdims=True)
        p = jnp.where(valid, jnp.exp(sm - m), 0.0)
        p = p / jnp.maximum(jnp.sum(p, axis=0, keepdims=True), 1e-30)
        o_cmp.append(_dot(vc_t, p.astype(BF16)))
        score = p if score is None else score + p
    cur = qpos // CMP_BLK
    forced = (n_idx == 0) | (n_idx == cur) | (n_idx == cur - 1)
    score = jnp.where(n_idx > cur, -1.0, jnp.where(forced, FORCE_SCORE, score))
    sel_t = _top_n_mask(score[:RANK_ROWS], nb)
    selb = jnp.concatenate([sel_t, jnp.zeros((NB_PAD - RANK_ROWS, tq), F32)], axis=0).astype(BF16)

    key = lax.broadcasted_iota(jnp.int32, (tk, tq), 0)
    qry = lax.broadcasted_iota(jnp.int32, (tk, tq), 1)
    init = tuple(_osm_init(tq, HEAD_DIM, 0) for _ in range(rq))

    def sel_tile(j, sts, diag):
        start = pl.multiple_of(j * tk, tk)
        chosen = _dot(e_ref[pl.ds(start, tk), :], selb)
        bias = (chosen - 1.0) * 1e30
        if diag:
            bias = jnp.where(key <= qry, bias, NEG_INF)
        return _osm_groups_t(q_ts, sk_ref[0, 0, pl.ds(start, tk), :], sv_ref[0, 0, :, pl.ds(start, tk)], bias, sts)

    sts = lax.fori_loop(0, i, lambda j, s: sel_tile(j, s, False), init)
    o_sel = [_osm_out(st) for st in sel_tile(i, sts, True)]

    def win_tile(back, sts):
        j_raw = i - back
        start = pl.multiple_of(jnp.maximum(j_raw, 0) * tk, tk)
        if back == 0:
            ok = key <= qry
        elif back < WINDOW // tk:
            ok = key >= 0
        else:
            ok = key > qry
        bias = jnp.where(ok & (j_raw >= 0), 0.0, NEG_INF)
        return _osm_groups_t(q_ts, wk_ref[0, 0, pl.ds(start, tk), :], wv_ref[0, 0, :, pl.ds(start, tk)], bias, sts)

    sts = init
    for back in range(WINDOW // tk + 1):
        sts = win_tile(back, sts)
    o_win = [_osm_out(st) for st in sts]

    gates = g_ref[0, 0]
    outs = []
    for r in range(rq):
        gc, gs, gw = (gates[3 * r + n:3 * r + n + 1, :] for n in range(3))
        outs.append(gc * o_cmp[r] + gs * o_sel[r] + gw * o_win[r])
    o_ref[0] = jnp.concatenate(outs, axis=0).T.astype(o_ref.dtype)


def _nsa_prompt(q_t, kc_hm, vc_t, sk_hm, sv_t, wk_hm, wv_t, gates_t, nb):
    b, _, d, t = q_t.shape
    tq = min(NSA_TILE, t)
    assert WINDOW % tq == 0 and nb <= RANK_ROWS and tq % 128 == 0
    expand = ((jnp.arange(t)[:, None] // CMP_BLK) == jnp.arange(NB_PAD)[None, :]).astype(BF16)
    per_group = lambda shape: pl.BlockSpec((1, 1) + shape, lambda bi, g, i: (bi, g, 0, 0))
    return pl.pallas_call(
        functools.partial(_nsa_prompt_body, nb),
        grid=(b, NSA_KVH, t // tq),
        in_specs=[pl.BlockSpec((1, NSA_GQA, d, tq), lambda bi, g, i: (bi, g, 0, i)),
                  per_group((NB_PAD, d)), per_group((d, NB_PAD)),
                  per_group((t, d)), per_group((d, t)), per_group((t, d)), per_group((d, t)),
                  pl.BlockSpec((1, 1, GATE_PAD, tq), lambda bi, g, i: (bi, g, 0, i)),
                  _const_spec((t, NB_PAD))],
        out_specs=pl.BlockSpec((1, tq, NSA_GQA * d), lambda bi, g, i: (bi, i, g)),
        out_shape=jax.ShapeDtypeStruct((b, t, NSA_HEADS * d), BF16),
        compiler_params=_cp(("parallel", "parallel", "arbitrary")), name="nsa_prompt",
    )(q_t, kc_hm, vc_t, sk_hm, sv_t, wk_hm, wv_t, gates_t, expand)


def _cmp_gather_body(npages, pt_ref, pe_ref, *refs):
    page_refs = refs[:npages]
    o_ref = refs[npages]
    tok_ref = refs[npages + 1]
    pe2 = pe_ref[...]
    lane_tiles = NSA_KV_DIM // 128
    for p in range(npages):
        tok = page_refs[p][0].reshape(NSA_KV_DIM, PAGE).T + pe2
        for c in range(lane_tiles):
            tok_ref[c, p * PAGE:(p + 1) * PAGE, :] = tok[:, c * 128:(c + 1) * 128]
    nblk = npages * PAGE // CMP_BLK
    def regroup(l, _):
        for c in range(lane_tiles):
            lo = pl.multiple_of(l * NSA_KV_DIM + c * 128, 128)
            o_ref[0, :, pl.ds(lo, 128)] = tok_ref[c, pl.ds(l, nblk, stride=CMP_BLK), :].astype(BF16)
        return 0

    lax.fori_loop(0, CMP_BLK, regroup, 0, unroll=8)


def _cmp_gather(cache, pe, page_table):
    b, npages = page_table.shape
    nblk = npages * PAGE // CMP_BLK
    width = CMP_BLK * NSA_KV_DIM
    pe2 = jnp.tile(pe, (PAGE // CMP_BLK, NSA_KVH))
    grid_spec = pltpu.PrefetchScalarGridSpec(
        num_scalar_prefetch=1, grid=(b,),
        in_specs=[pl.BlockSpec((PAGE, NSA_KV_DIM), lambda bi, pt: (0, 0))] + _page_specs(npages, NSA_KVH),
        out_specs=pl.BlockSpec((1, nblk, width), lambda bi, pt: (bi, 0, 0)),
        scratch_shapes=[pltpu.VMEM((NSA_KV_DIM // 128, npages * PAGE, 128), F32)])
    return pl.pallas_call(
        functools.partial(_cmp_gather_body, npages), grid_spec=grid_spec,
        out_shape=jax.ShapeDtypeStruct((b, nblk, width), BF16),
        compiler_params=_cp(("parallel",)), name="cmp_gather",
    )(page_table, pe2, *([cache] * npages))


def _nsa_sample_body(npages, t_new, past, nb, pt_ref, qbd_ref, kc_ref, vc_ref, g_ref,
                     skn_ref, svn_ref, wkn_ref, wvn_ref, wkc_ref, wvc_ref, *refs):
    sk_refs = refs[:npages]
    sv_refs = refs[npages:2 * npages]
    o_ref = refs[2 * npages]
    qbd = qbd_ref[0]
    rows = qbd.shape[0]
    w = qbd.shape[1]
    rq = rows // t_new
    t_idx = lax.broadcasted_iota(jnp.int32, (t_new, 1), 0)
    qpos = past + t_idx

    o_cmp, sel = _cmp_and_select(qbd, kc_ref[0], vc_ref[0], qpos, NSA_KVH, NSA_GQA, nb)
    sel_bias = (sel - 1.0) * 1e30

    qoff = lax.broadcasted_iota(jnp.int32, (rows, PAGE), 0) % t_new
    col = lax.broadcasted_iota(jnp.int32, (rows, PAGE), 1)
    pad = jnp.zeros((PAGE - t_new, w), F32)
    new_bias = jnp.where(col <= qoff, 0.0, NEG_INF)

    def padded(ref):
        return jnp.concatenate([ref[0], pad], axis=0).astype(BF16)

    half = col < CMP_BLK
    zs, vs = [], []
    for p in range(npages):
        b0 = sel_bias[:, 2 * p:2 * p + 1]
        b1 = sel_bias[:, 2 * p + 1:2 * p + 2]
        zs.append(_dot(qbd, sk_refs[p][0].reshape(w, PAGE).astype(BF16)) + jnp.where(half, b0, b1))
        vs.append((sv_refs[p][0].reshape(w, PAGE).astype(BF16), True))
    last = (past // CMP_BLK)
    zs.append(_nt(qbd, padded(skn_ref)) + new_bias + sel_bias[:, last:last + 1])
    vs.append((padded(svn_ref), False))
    o_sel = _softmax_tiles(zs, vs)

    zs, vs = [_nt(qbd, padded(wkn_ref)) + new_bias], [(padded(wvn_ref), False)]
    wb = wkc_ref.shape[3]
    for c in range(wb // PAGE):
        kpos = (past - wb) + c * PAGE + col
        diff = (past + qoff) - kpos
        bias = jnp.where((diff < WINDOW) & (kpos >= 0), 0.0, NEG_INF)
        chunk = lambda ref: ref[0, :, :, c * PAGE:(c + 1) * PAGE].reshape(w, PAGE).astype(BF16)
        zs.append(_dot(qbd, chunk(wkc_ref)) + bias)
        vs.append((chunk(wvc_ref), True))
    o_win = _softmax_tiles(zs, vs)

    gates = g_ref[0]
    o_ref[0] = gates[:, 0:1] * o_cmp + gates[:, 1:2] * o_sel + gates[:, 2:3] * o_win


def _nsa_sample(q, kc, vc, gates, sk_new, sv_new, wk_new, wv_new, win_k, win_v,
                cache_sk, cache_sv, page_table, past, nb):
    b, t, w = sk_new.shape
    npages = page_table.shape[1]
    assert nb <= RANK_ROWS and (past // CMP_BLK) < nb
    qbd = _block_diag_rows(q.reshape(b, t, NSA_HEADS, HEAD_DIM), NSA_KVH)
    rows = NSA_HEADS * t
    g_rows = jnp.transpose(gates.reshape(b, t, NSA_KVH, NSA_GQA, 3), (0, 2, 3, 1, 4)).reshape(b, rows, 3)
    bspec = lambda r, c: pl.BlockSpec((1, r, c), lambda bi, pt: (bi, 0, 0))
    wb = win_k.shape[3]
    win_spec = pl.BlockSpec((1, NSA_KVH, HEAD_DIM, wb), lambda bi, pt: (bi, 0, 0, 0))
    grid_spec = pltpu.PrefetchScalarGridSpec(
        num_scalar_prefetch=1, grid=(b,),
        in_specs=[bspec(rows, w), bspec(NB_PAD, w), bspec(NB_PAD, w), bspec(rows, 3),
                  bspec(t, w), bspec(t, w), bspec(t, w), bspec(t, w), win_spec, win_spec]
                 + _page_specs(npages, NSA_KVH) + _page_specs(npages, NSA_KVH),
        out_specs=bspec(rows, w))
    o = pl.pallas_call(
        functools.partial(_nsa_sample_body, npages, t, past, nb), grid_spec=grid_spec,
        out_shape=jax.ShapeDtypeStruct((b, rows, w), F32),
        compiler_params=_cp(("parallel",)), name="nsa_sample",
    )(page_table, qbd, kc, vc, g_rows, sk_new, sv_new, wk_new, wv_new, win_k, win_v,
      *([cache_sk] * npages), *([cache_sv] * npages))
    return _diag_rows_out(o, t, NSA_KVH)


def _common_tail(h2, batch, l, mem_k, mem_v, p, final_g):
    t = h2.shape[0] // batch
    if mem_k.ndim == 5:
        q = _norm_proj(h2, p["norm_mem"][l][None], p["w_mem_q"][l], MEM_HD ** -0.5).reshape(batch, t, D_MODEL)
        o = _mem_attn_cached(q, mem_k, mem_v, l)
        h2 = _proj_res(h2, [o.reshape(batch * t, D_MODEL)], p["w_mem_o"][l])
    else:
        h2 = _mem_block(h2, p["norm_mem"][l][None], p["w_mem_q"][l], mem_k, mem_v, p["w_mem_o"][l], batch)
    return _ffn(h2, p["norm_ffn"][l][None], p["w_ffn_gate"][l], p["w_ffn_up"][l], p["w_ffn_down"][l], final_g)


def _even_layer(h2, batch, p, conv_buf, sample_ctx):
    t = h2.shape[0] // batch
    prompt = sample_ctx is None
    res = _even_in(h2, p["norm_mix"][0][None], p["w_in_even"][0], batch, prompt)
    glu3 = res[0].reshape(batch, t, CONV_CH)
    y_conv = _conformer_conv(glu3, conv_buf, p["conv_w"][0], p["conv_b"][0], p["conv_ln_g"][0], p["conv_ln_b"][0])
    new_buf = jnp.concatenate([conv_buf.astype(F32), glu3], axis=1)[:, -(CONV_W - 1):]
    shp = (batch, t, SB_HEADS, HEAD_DIM)
    if prompt:
        o = _sb_prompt(res[3], res[4], res[5])
        k_out, v_out = _from_transposed(res[1]), _from_transposed(res[2])
    else:
        cache_k, cache_v, page_table = sample_ctx
        k32, v32 = res[1], res[2]
        o = _sb_sample(res[3].reshape(batch, t, SB_DIM), k32.reshape(batch, t, SB_DIM),
                       v32.reshape(batch, t, SB_DIM), cache_k, cache_v, page_table)
        k_out, v_out = k32.reshape(shp), v32.reshape(shp)
    h2 = _proj_res(h2, [y_conv.reshape(batch * t, CONV_CH), o.reshape(batch * t, SB_DIM)], p["w_mix_out"][0])
    return h2, new_buf, k_out, v_out


def _odd_layer(h2, batch, p, sample_ctx):
    t = h2.shape[0] // batch
    prompt = sample_ctx is None
    past = 0 if prompt else sample_ctx["past"]
    tm = _row_tile(h2.shape[0])
    pos = past + jnp.arange(max(t, tm), dtype=jnp.int32) % t
    cos_t, sin_t = _rope_tables(pos, 128)
    res = _odd_in(h2, p["norm_mix"][1][None], p["w_in_odd"], cos_t, sin_t, batch, prompt)
    ck, cv = res[0], res[1]
    nb = -(-(past + t) // CMP_BLK)
    cw = (p["cmp_pe_k"][0], p["cmp_w1_k"][0], p["cmp_w2_k"][0]), (p["cmp_pe_v"][0], p["cmp_w1_v"][0], p["cmp_w2_v"][0])
    width = CMP_BLK * NSA_KV_DIM

    def end_tables(first, count):
        blk_end = (first + jnp.arange(count, dtype=jnp.int32)) * CMP_BLK + (CMP_BLK - 1)
        return tuple(jnp.tile(x, (batch, 1)) for x in _rope_tables(blk_end, 128))

    if prompt:
        kc = _compress(ck.reshape(batch * nb, width), *cw[0], end_tables(0, nb), True)
        vc = _compress(cv.reshape(batch * nb, width), *cw[1], None, True)
    else:
        nbp = past // CMP_BLK
        assert past % CMP_BLK == 0 and nb == nbp + 1
        pt = sample_ctx["page_table"]
        new_block = lambda x: jnp.pad(x.reshape(batch, t, NSA_KV_DIM), ((0, 0), (0, CMP_BLK - t), (0, 0))).reshape(batch, width)
        halves = []
        for cache, new, w, tabs in ((sample_ctx["cmp_k"], ck, cw[0], (end_tables(0, nbp), end_tables(nbp, 1))),
                                    (sample_ctx["cmp_v"], cv, cw[1], (None, None))):
            x_past = _cmp_gather(cache, w[0], pt).reshape(batch * nbp, width)
            c_past = _compress(x_past, *w, tabs[0], False).reshape(batch, nbp, NSA_KV_DIM)
            c_new = _compress(new_block(new), *w, tabs[1], True).reshape(batch, 1, NSA_KV_DIM)
            halves.append(jnp.concatenate([c_past, c_new], axis=1))
        kc, vc = halves
    kc = kc.reshape(batch, nb, NSA_KV_DIM)
    vc = vc.reshape(batch, nb, NSA_KV_DIM)
    padc = lambda x: jnp.pad(x, ((0, 0), (0, NB_PAD - nb), (0, 0))).astype(BF16)
    kvshape = (batch, t, NSA_KVH, HEAD_DIM)
    if prompt:
        hm = lambda x: jnp.transpose(padc(x).reshape(batch, NB_PAD, NSA_KVH, HEAD_DIM), (0, 2, 1, 3))
        t32 = res[2:8]
        q_t, gates_t, sk_hm, sv_t, wk_hm, wv_t = res[8:]
        o = _nsa_prompt(q_t, hm(kc), jnp.swapaxes(hm(vc), 2, 3), sk_hm, sv_t, wk_hm, wv_t, gates_t, nb)
        keep = min(WINDOW, t)
        outs = [_from_transposed(x) for x in t32[:4]] + [_from_transposed(x[..., -keep:]) for x in t32[4:]]
    else:
        sk, sv, wk, wv = res[2:6]
        q, gates = res[6:]
        gates = gates.reshape(batch, t, NSA_KVH, GATE_PAD)[..., :3 * NSA_GQA]
        r3 = lambda x: x.reshape(batch, t, NSA_KV_DIM)
        wkc, wvc = sample_ctx["win_k"], sample_ctx["win_v"]
        wb = wkc.shape[1]
        o = _nsa_sample(q.reshape(batch, t, ODD_Q), padc(kc), padc(vc), gates,
                        r3(sk), r3(sv), r3(wk), r3(wv), _keys_minor(wkc), _keys_minor(wvc),
                        sample_ctx["sel_k"], sample_ctx["sel_v"], sample_ctx["page_table"], past, nb)
        r4 = lambda x: x.reshape(kvshape)
        outs = [r4(ck), r4(cv), r4(sk), r4(sv),
                jnp.concatenate([wkc, r4(wk)], axis=1)[:, -wb:], jnp.concatenate([wvc, r4(wv)], axis=1)[:, -wb:]]
    h2 = _proj_res(h2, [o.reshape(batch * t, ODD_Q)], p["w_mix_out"][1])
    return (h2,) + tuple(outs)


def _trunk(x, p, conv_buf, mem_kv, even_ctx, odd_ctx):
    batch, t, _ = x.shape
    h2 = x.reshape(batch * t, D_MODEL)
    h2, new_buf, sbk, sbv = _even_layer(h2, batch, p, conv_buf, even_ctx)
    h2 = _common_tail(h2, batch, 0, mem_kv[0][0], mem_kv[0][1], p, None)
    h2, ck, cv, sk, sv, wk, wv = _odd_layer(h2, batch, p, odd_ctx)
    h2 = _common_tail(h2, batch, 1, mem_kv[1][0], mem_kv[1][1], p, p["final_norm"][None])
    st = lambda a: a[None]
    return (h2.reshape(batch, t, D_MODEL), st(sbk), st(sbv), st(new_buf),
            st(ck), st(cv), st(sk), st(sv), st(wk), st(wv))


def kernel(x_prompt, x_sample, mem_prompt, cache_sb_k, cache_sb_v, state_conv,
           cache_nsa_cmp_k, cache_nsa_cmp_v, cache_nsa_sel_k, cache_nsa_sel_v,
           cache_nsa_win_k, cache_nsa_win_v, cache_mem_k, cache_mem_v, page_table,
           norm_mix, norm_mem, norm_ffn, final_norm, w_in_even, w_in_odd, w_mix_out,
           conv_w, conv_b, conv_ln_g, conv_ln_b,
           cmp_pe_k, cmp_w1_k, cmp_w2_k, cmp_pe_v, cmp_w1_v, cmp_w2_v,
           w_mem_q, w_mem_k, w_mem_v, w_mem_o, w_ffn_gate, w_ffn_up, w_ffn_down):
    assert norm_mix.shape[0] == 2 and w_in_even.shape[0] == 1 and w_in_odd.shape[0] == 1
    bp = x_prompt.shape[0]
    bs = x_sample.shape[0]
    bf = lambda w: w.astype(BF16)
    p = dict(
        norm_mix=norm_mix, norm_mem=norm_mem, norm_ffn=norm_ffn, final_norm=final_norm,
        w_in_even=bf(w_in_even),
        w_in_odd=_odd_in_weights(w_in_odd[0]),
        w_mix_out=bf(w_mix_out), conv_w=conv_w, conv_b=conv_b, conv_ln_g=conv_ln_g, conv_ln_b=conv_ln_b,
        cmp_pe_k=cmp_pe_k, cmp_w1_k=cmp_w1_k, cmp_w2_k=cmp_w2_k,
        cmp_pe_v=cmp_pe_v, cmp_w1_v=cmp_w1_v, cmp_w2_v=cmp_w2_v,
        w_mem_q=bf(w_mem_q), w_mem_o=bf(w_mem_o),
        w_ffn_gate=bf(w_ffn_gate), w_ffn_up=bf(w_ffn_up), w_ffn_down=bf(w_ffn_down))

    ml = mem_prompt.shape[1]
    mem = _mem_kv(mem_prompt.reshape(bp * ml, D_MODEL),
                  [bf(w_mem_k[0]), bf(w_mem_v[0]), bf(w_mem_k[1]), bf(w_mem_v[1])])
    m3 = lambda a: a.reshape(bp, ml, D_MODEL)
    m4 = lambda a: a.reshape(bp, ml, MEM_HEADS, MEM_HD)
    mem_kv_p = [(m3(mem[1]), m3(mem[3])), (m3(mem[5]), m3(mem[7]))]
    mem_k_p = jnp.stack([m4(mem[0]), m4(mem[4])])
    mem_v_p = jnp.stack([m4(mem[2]), m4(mem[6])])
    zero_buf = jnp.zeros((bp, CONV_W - 1, CONV_CH), F32)
    outs_p = _trunk(x_prompt, p, zero_buf, mem_kv_p, None, None)

    past = page_table.shape[1] * PAGE
    even_ctx = (_keys_minor(cache_sb_k[0]), _keys_minor(cache_sb_v[0]), page_table)
    odd_ctx = dict(past=past, page_table=page_table,
                   cmp_k=_keys_minor(cache_nsa_cmp_k[0]), cmp_v=_keys_minor(cache_nsa_cmp_v[0]),
                   sel_k=_keys_minor(cache_nsa_sel_k[0]), sel_v=_keys_minor(cache_nsa_sel_v[0]),
                   win_k=cache_nsa_win_k[0], win_v=cache_nsa_win_v[0])
    mem_kv_s = [(cache_mem_k, cache_mem_v)] * cache_mem_k.shape[0]
    outs_s = _trunk(x_sample, p, state_conv[0], mem_kv_s, even_ctx, odd_ctx)

    return (outs_p[0], outs_s[0]) + tuple(outs_p[1:]) + (mem_k_p, mem_v_p) + tuple(outs_s[1:])
```

```python
import functools
import math

import jax
import jax.numpy as jnp
from jax import lax
from jax.experimental import pallas as pl
from jax.experimental.pallas import tpu as pltpu

F32 = jnp.float32
BF16 = jnp.bfloat16

D_MODEL = 1024
HEAD_DIM = 64
CONV_CH = 512
CONV_W = 31
SB_HEADS = 8
SB_DIM = 512
NSA_HEADS = 16
NSA_GQA = 4
NSA_KVH = 4
NSA_KV_DIM = 256
CMP_BLK = 64
CMP_HID = 256
SEL_TOPN = 16
FORCE_SCORE = 1.0e4
WINDOW = 512
MEM_HEADS = 4
MEM_HD = 256
D_FF = 2816
ROPE_THETA = 10000.0
NORM_EPS = 1e-6
NEG_INF = -1e30
PAGE = 128

V7X_VMEM_BYTES = 64 * 1024 * 1024
VMEM_LIMIT = V7X_VMEM_BYTES - 8 * 1024 * 1024
ATT_TILE = 256
NSA_TILE = 512


def _cp(sem):
    return pltpu.CompilerParams(dimension_semantics=sem, vmem_limit_bytes=VMEM_LIMIT)


def _nt(a, b):
    return lax.dot_general(a, b, (((1,), (1,)), ((), ())), preferred_element_type=F32)


def _dot(a, b):
    return jnp.dot(a, b, preferred_element_type=F32)


def _rms(x, g):
    y = x * lax.rsqrt(jnp.mean(x * x, axis=-1, keepdims=True) + NORM_EPS)
    return y * g


def _sigmoid(x):
    return 1.0 / (1.0 + jnp.exp(-x))


def _const_spec(shape):
    n = len(shape)
    return pl.BlockSpec(shape, lambda *a: (0,) * n)


def _row_spec(tm, n):
    return pl.BlockSpec((tm, n), lambda i: (i, 0))


def _hm_spec(heads, tm, width, nt):
    return pl.BlockSpec((1, heads, tm, width), lambda i: (i // nt, 0, i % nt, 0))


def _tr_spec(heads, tm, nt):
    return pl.BlockSpec((1, heads, HEAD_DIM, tm), lambda i: (i // nt, 0, 0, i % nt))


def _store_transposed(x, heads, f32_ref, bf16_ref):
    xt = x.T.reshape(heads, HEAD_DIM, x.shape[0])
    if f32_ref is not None:
        f32_ref[0] = xt
    if bf16_ref is not None:
        bf16_ref[0] = xt.astype(BF16)


def _store_head_major(x, heads, bf16_ref):
    for hh in range(heads):
        bf16_ref[0, hh] = x[:, hh * HEAD_DIM:(hh + 1) * HEAD_DIM].astype(BF16)


def _from_transposed(xt):
    return jnp.transpose(xt, (0, 3, 1, 2))


def _row_tile(m):
    return 512 if m >= 4096 else min(256, m)


def _rope_tables(pos, width):
    half = HEAD_DIM // 2
    inv = ROPE_THETA ** (-jnp.arange(half, dtype=F32) / half)
    ang = pos.astype(F32)[:, None] * inv[None, :]
    cos = jnp.cos(ang)
    sin = jnp.sin(ang)
    c = jnp.concatenate([cos, cos], axis=-1)
    s = jnp.concatenate([-sin, sin], axis=-1)
    reps = width // HEAD_DIM
    return jnp.tile(c, (1, reps)), jnp.tile(s, (1, reps))


def _rope128(x, c, s):
    lane = lax.broadcasted_iota(jnp.int32, x.shape, 1)
    first = (lane % HEAD_DIM) < (HEAD_DIM // 2)
    rot = jnp.where(first, pltpu.roll(x, 96, 1), pltpu.roll(x, 32, 1))
    return x * c + rot * s


def _rope_wide(x, c, s):
    n = x.shape[1] // 128
    return jnp.concatenate([_rope128(x[:, i * 128:(i + 1) * 128], c, s) for i in range(n)], axis=1)


def _even_in_body(head_major, h_ref, g_ref, w_ref, glu_ref, k32_ref, v32_ref, q_ref, *hm_refs):
    xn = _rms(h_ref[...], g_ref[...]).astype(BF16)

    def mm(lo, hi):
        return _dot(xn, w_ref[:, lo:hi])

    glu_ref[...] = mm(0, CONV_CH) * _sigmoid(mm(CONV_CH, 2 * CONV_CH))
    base = 2 * CONV_CH
    q = mm(base, base + SB_DIM) * (HEAD_DIM ** -0.5)
    k = mm(base + SB_DIM, base + 2 * SB_DIM)
    v = mm(base + 2 * SB_DIM, base + 3 * SB_DIM)
    if head_major:
        kh_ref, vh_ref = hm_refs
        _store_transposed(q, SB_HEADS, None, q_ref)
        _store_transposed(k, SB_HEADS, k32_ref, None)
        _store_head_major(k, SB_HEADS, kh_ref)
        _store_transposed(v, SB_HEADS, v32_ref, vh_ref)
    else:
        k32_ref[...] = k
        v32_ref[...] = v
        q_ref[...] = q.astype(BF16)


def _even_in(h2, g, w, batch, head_major):
    m = h2.shape[0]
    t = m // batch
    tm = _row_tile(m)
    nt = max(t // tm, 1)
    outs = [jax.ShapeDtypeStruct((m, CONV_CH), F32)]
    specs = [_row_spec(tm, CONV_CH)]
    if head_major:
        kt = lambda dt: jax.ShapeDtypeStruct((batch, SB_HEADS, HEAD_DIM, t), dt)
        outs += [kt(F32), kt(F32), kt(BF16), jax.ShapeDtypeStruct((batch, SB_HEADS, t, HEAD_DIM), BF16), kt(BF16)]
        specs += [_tr_spec(SB_HEADS, tm, nt)] * 3 + [_hm_spec(SB_HEADS, tm, HEAD_DIM, nt), _tr_spec(SB_HEADS, tm, nt)]
    else:
        outs += [jax.ShapeDtypeStruct((m, SB_DIM), F32)] * 2 + [jax.ShapeDtypeStruct((m, SB_DIM), BF16)]
        specs += [_row_spec(tm, SB_DIM)] * 3
    return pl.pallas_call(
        functools.partial(_even_in_body, head_major),
        grid=(m // tm,),
        in_specs=[_row_spec(tm, D_MODEL), _const_spec((1, D_MODEL)), _const_spec(w.shape)],
        out_specs=specs, out_shape=outs,
        compiler_params=_cp(("parallel",)), name="even_in",
    )(h2, g, w)


CONV_PAD = 32


def _conv_body(t, chunk, hp_ref, w_ref, b_ref, lg_ref, lb_ref, o_ref):
    win_rows = chunk + CONV_PAD

    def one_chunk(c, _):
        base = pl.multiple_of(c * chunk, chunk)
        win = hp_ref[0, pl.ds(base, win_rows), :]
        acc = jnp.zeros((chunk, CONV_CH), F32)
        for r in range(8):
            sh = win if r == 0 else pltpu.roll(win, win_rows - r, 0)
            for a in range(4):
                tap = 8 * a + r
                if tap < CONV_W:
                    acc = acc + sh[8 * a:8 * a + chunk] * w_ref[tap:tap + 1, :]
        y = acc + b_ref[...]
        mu = jnp.mean(y, axis=-1, keepdims=True)
        var = jnp.mean(jnp.square(y - mu), axis=-1, keepdims=True)
        y = (y - mu) * lax.rsqrt(var + NORM_EPS) * lg_ref[...] + lb_ref[...]
        o_ref[0, pl.ds(base, chunk), :] = (y * _sigmoid(y)).astype(o_ref.dtype)
        return 0

    lax.fori_loop(0, t // chunk, one_chunk, 0)


def _conformer_conv(glu, buf, w_dw, b_dw, ln_g, ln_b):
    b, t, c = glu.shape
    hp = jnp.concatenate([buf.astype(F32), glu, jnp.zeros((b, CONV_PAD - (CONV_W - 1), c), F32)], axis=1)
    chunk = min(t, 128)
    return pl.pallas_call(
        functools.partial(_conv_body, t, chunk),
        grid=(b,),
        in_specs=[pl.BlockSpec((1, t + CONV_PAD, c), lambda i: (i, 0, 0)),
                  _const_spec((CONV_PAD, c)), _const_spec((1, c)), _const_spec((1, c)), _const_spec((1, c))],
        out_specs=pl.BlockSpec((1, t, c), lambda i: (i, 0, 0)),
        out_shape=jax.ShapeDtypeStruct((b, t, c), BF16),
        compiler_params=_cp(("parallel",)), name="conformer_conv",
    )(hp, jnp.pad(w_dw, ((0, CONV_PAD - CONV_W), (0, 0))), b_dw[None], ln_g[None], ln_b[None])


def _softplus(z):
    return jnp.maximum(z, 0.0) + jnp.log(1.0 + jnp.exp(-jnp.abs(z)))


def _sb_tiles(zs, carries, valid, tri2, keys_axis):
    valids = valid if isinstance(valid, (list, tuple)) else [valid] * len(zs)
    sps = []
    for z, ok in zip(zs, valids):
        sp = _softplus(z)
        sps.append(sp if ok is None else jnp.where(ok, sp, 0.0))
    laters = []
    for sp in sps:
        hi = sp.astype(BF16)
        lo = (sp - hi.astype(F32)).astype(BF16)
        split = jnp.concatenate([hi, lo], axis=keys_axis)
        laters.append(_dot(split, tri2) if keys_axis == 1 else _dot(tri2, split))
    first = (lambda a: a[:, 0:1]) if keys_axis == 1 else (lambda a: a[0:1, :])
    ws, new = [], []
    for n, (z, sp, later, ok) in enumerate(zip(zs, sps, laters, valids)):
        carry = new[n - len(carries)] if n >= len(carries) else carries[n]
        w = jnp.exp(z - sp - later - carry)
        ws.append((w if ok is None else jnp.where(ok, w, 0.0)).astype(BF16))
        new.append(carry + first(later) + first(sp))
    return ws, new


def _tri2(tk, keys_axis):
    s = jnp.arange(tk)[:, None]
    j = jnp.arange(tk)[None, :]
    if keys_axis == 1:
        tri = (s > j).astype(BF16)
        return jnp.concatenate([tri, tri], axis=0)
    tri = (j > s).astype(BF16)
    return jnp.concatenate([tri, tri], axis=1)


def _sb_prompt_body(q_ref, k_ref, v_ref, tri_ref, o_ref):
    i = pl.program_id(2)
    tq = q_ref.shape[3]
    key = lax.broadcasted_iota(jnp.int32, (tq, tq), 0)
    qry = lax.broadcasted_iota(jnp.int32, (tq, tq), 1)
    diag_valid = key < qry
    tri2 = tri_ref[...]
    nh = q_ref.shape[1]

    def tiles(js, st, valid):
        starts = [pl.multiple_of(j * tq, tq) for j in js]
        zs = [_dot(k_ref[0, hh, pl.ds(s0, tq), :], q_ref[0, hh]) for s0 in starts for hh in range(nh)]
        ws, carries = _sb_tiles(zs, [s[0] for s in st], valid, tri2, 0)
        accs = [s[1] for s in st]
        for n, s0 in enumerate(starts):
            for hh in range(nh):
                accs[hh] = accs[hh] + _dot(v_ref[0, hh, :, pl.ds(s0, tq)], ws[n * nh + hh])
        return tuple(zip(carries[-nh:], accs))

    init = tuple((jnp.zeros((1, tq), F32), jnp.zeros((HEAD_DIM, tq), F32)) for _ in range(nh))
    st = tiles([i], init, diag_valid)
    odd = i % 2
    st = lax.cond(odd == 1, lambda s: tiles([i - 1], s, None), lambda s: s, st)
    top = i - 1 - odd
    st = lax.fori_loop(0, i // 2, lambda n, s: tiles([top - 2 * n, top - 2 * n - 1], s, None), st)
    o_ref[0] = jnp.concatenate([s[1] for s in st], axis=0).T.astype(o_ref.dtype)


SB_HEADS_PER_STEP = 4


def _sb_prompt(q_t, k_hm, v_t):
    b, h, d, t = q_t.shape
    tq = min(ATT_TILE, t)
    nh = SB_HEADS_PER_STEP
    return pl.pallas_call(
        _sb_prompt_body,
        grid=(b, h // nh, t // tq),
        in_specs=[pl.BlockSpec((1, nh, d, tq), lambda bi, hp, i: (bi, hp, 0, i)),
                  pl.BlockSpec((1, nh, t, d), lambda bi, hp, i: (bi, hp, 0, 0)),
                  pl.BlockSpec((1, nh, d, t), lambda bi, hp, i: (bi, hp, 0, 0)),
                  _const_spec((tq, 2 * tq))],
        out_specs=pl.BlockSpec((1, tq, nh * d), lambda bi, hp, i: (bi, i, hp)),
        out_shape=jax.ShapeDtypeStruct((b, t, h * d), BF16),
        compiler_params=_cp(("parallel", "parallel", "arbitrary")), name="sb_prompt",
    )(q_t, k_hm, v_t, _tri2(tq, 0))


def _block_diag_rows(q, groups):
    b, t, h, d = q.shape
    per = h // groups
    qg = jnp.transpose(q.reshape(b, t, groups, per, d), (0, 2, 3, 1, 4))
    eye = jnp.eye(groups, dtype=q.dtype)[None, :, None, None, :, None]
    return (qg[:, :, :, :, None, :] * eye).reshape(b, h * t, groups * d)


def _diag_rows_out(o, t, groups):
    b, rows, gd = o.shape
    d = gd // groups
    per = rows // (groups * t)
    o6 = o.reshape(b, groups, per, t, groups, d)
    idx = jnp.arange(groups)
    og = o6[:, idx, :, :, idx, :]
    return jnp.transpose(og, (1, 3, 0, 2, 4)).reshape(b, t, groups * per * d)


def _sb_sample_body(npages, t_new, pt_ref, qbd_ref, kn_ref, vn_ref, tri_ref, *refs):
    k_refs = refs[:npages]
    v_refs = refs[npages:2 * npages]
    o_ref = refs[2 * npages]
    qbd = qbd_ref[0]
    rows = qbd.shape[0]
    tri2 = tri_ref[...]
    width = kn_ref.shape[2]
    pad = jnp.zeros((PAGE - t_new, width), F32)
    kn = jnp.concatenate([kn_ref[0], pad], axis=0).astype(BF16)
    vn = jnp.concatenate([vn_ref[0], pad], axis=0).astype(BF16)
    qoff = lax.broadcasted_iota(jnp.int32, (rows, PAGE), 0) % t_new
    col = lax.broadcasted_iota(jnp.int32, (rows, PAGE), 1)
    order = list(reversed(range(npages)))
    zs = [_nt(qbd, kn)] + [_dot(qbd, k_refs[p][0].reshape(width, PAGE).astype(BF16)) for p in order]
    ws, _ = _sb_tiles(zs, [jnp.zeros((rows, 1), F32)], [col < qoff] + [None] * npages, tri2, 1)
    acc = _dot(ws[0], vn)
    for w, p in zip(ws[1:], order):
        acc = acc + _nt(w, v_refs[p][0].reshape(width, PAGE).astype(BF16))
    o_ref[0] = acc


def _page_specs(npages, heads):
    return [pl.BlockSpec((1, heads, HEAD_DIM, PAGE), lambda b, pt, p=p: (pt[b, p], 0, 0, 0))
            for p in range(npages)]


def _keys_minor(cache):
    nd = cache.ndim
    return jnp.transpose(cache, tuple(range(nd - 3)) + (nd - 2, nd - 1, nd - 3))


def _sb_sample(q, k_new, v_new, cache_k, cache_v, page_table):
    b, t, w = k_new.shape
    npages = page_table.shape[1]
    qbd = _block_diag_rows(q.reshape(b, t, SB_HEADS, HEAD_DIM), SB_HEADS)
    rows = SB_HEADS * t
    bspec = lambda r, c: pl.BlockSpec((1, r, c), lambda bi, pt: (bi, 0, 0))
    grid_spec = pltpu.PrefetchScalarGridSpec(
        num_scalar_prefetch=1, grid=(b,),
        in_specs=[bspec(rows, w), bspec(t, w), bspec(t, w),
                  pl.BlockSpec((2 * PAGE, PAGE), lambda bi, pt: (0, 0))]
                 + _page_specs(npages, SB_HEADS) + _page_specs(npages, SB_HEADS),
        out_specs=bspec(rows, w))
    o = pl.pallas_call(
        functools.partial(_sb_sample_body, npages, t),
        grid_spec=grid_spec,
        out_shape=jax.ShapeDtypeStruct((b, rows, w), F32),
        compiler_params=_cp(("parallel",)), name="sb_sample",
    )(page_table, qbd, k_new, v_new, _tri2(PAGE, 1), *([cache_k] * npages), *([cache_v] * npages))
    return _diag_rows_out(o, t, SB_HEADS)


def _proj_res_body(n_in, *refs):
    h_ref = refs[0]
    x_refs = refs[1:1 + n_in]
    w_ref = refs[1 + n_in]
    o_ref = refs[2 + n_in]
    acc = h_ref[...]
    off = 0
    for x_ref in x_refs:
        k = x_ref.shape[1]
        acc = acc + _dot(x_ref[...].astype(BF16), w_ref[off:off + k, :])
        off += k
    o_ref[...] = acc


def _proj_res(h2, xs, w):
    m = h2.shape[0]
    tm = _row_tile(m)
    return pl.pallas_call(
        functools.partial(_proj_res_body, len(xs)),
        grid=(m // tm,),
        in_specs=[_row_spec(tm, D_MODEL)] + [_row_spec(tm, x.shape[1]) for x in xs] + [_const_spec(w.shape)],
        out_specs=_row_spec(tm, D_MODEL),
        out_shape=jax.ShapeDtypeStruct((m, D_MODEL), F32),
        compiler_params=_cp(("parallel",)), name="proj_res",
    )(h2, *xs, w)


def _norm_proj_body(scale, h_ref, g_ref, w_ref, o_ref):
    xn = _rms(h_ref[...], g_ref[...]).astype(BF16)
    o_ref[...] = (_dot(xn, w_ref[...]) * scale).astype(o_ref.dtype)


def _norm_proj(h2, g, w, scale):
    m = h2.shape[0]
    tm = _row_tile(m)
    return pl.pallas_call(
        functools.partial(_norm_proj_body, scale),
        grid=(m // tm,),
        in_specs=[_row_spec(tm, D_MODEL), _const_spec((1, D_MODEL)), _const_spec(w.shape)],
        out_specs=_row_spec(tm, w.shape[1]),
        out_shape=jax.ShapeDtypeStruct((m, w.shape[1]), BF16),
        compiler_params=_cp(("parallel",)), name="norm_proj",
    )(h2, g, w)


def _mem_kv_body(x_ref, wk0, wv0, wk1, wv1, *o_refs):
    x = x_ref[...].astype(BF16)
    for n, w_ref in enumerate((wk0, wv0, wk1, wv1)):
        y = _dot(x, w_ref[...])
        o_refs[2 * n][...] = y
        o_refs[2 * n + 1][...] = y.astype(BF16)


def _mem_kv(x2, ws):
    m = x2.shape[0]
    tm = _row_tile(m)
    outs, specs = [], []
    for _ in ws:
        outs += [jax.ShapeDtypeStruct((m, D_MODEL), F32), jax.ShapeDtypeStruct((m, D_MODEL), BF16)]
        specs += [_row_spec(tm, D_MODEL)] * 2
    return pl.pallas_call(
        _mem_kv_body, grid=(m // tm,),
        in_specs=[_row_spec(tm, D_MODEL)] + [_const_spec(w.shape) for w in ws],
        out_specs=specs, out_shape=outs,
        compiler_params=_cp(("parallel",)), name="mem_kv",
    )(x2, *ws)


def _mem_block_body(h_ref, g_ref, wq_ref, k_ref, v_ref, wo_ref, o_ref):
    h = h_ref[...]
    q = (_dot(_rms(h, g_ref[...]).astype(BF16), wq_ref[...]) * (MEM_HD ** -0.5)).astype(BF16)
    outs = []
    for hh in range(MEM_HEADS):
        sl = slice(hh * MEM_HD, (hh + 1) * MEM_HD)
        s = _nt(q[:, sl], k_ref[0, :, sl])
        m = jnp.max(s, axis=-1, keepdims=True)
        e = jnp.exp(s - m)
        p = e / jnp.sum(e, axis=-1, keepdims=True)
        outs.append(_dot(p.astype(BF16), v_ref[0, :, sl]).astype(BF16))
    o_ref[...] = h + _dot(jnp.concatenate(outs, axis=1), wo_ref[...])


def _mem_block(h2, g, wq, mk, mv, wo, batch):
    m = h2.shape[0]
    tm = _row_tile(m)
    nt = (m // batch) // tm
    ml = mk.shape[1]
    kv_spec = pl.BlockSpec((1, ml, D_MODEL), lambda i: (i // nt, 0, 0))
    return pl.pallas_call(
        _mem_block_body, grid=(m // tm,),
        in_specs=[_row_spec(tm, D_MODEL), _const_spec((1, D_MODEL)), _const_spec(wq.shape), kv_spec, kv_spec,
                  _const_spec(wo.shape)],
        out_specs=_row_spec(tm, D_MODEL),
        out_shape=jax.ShapeDtypeStruct((m, D_MODEL), F32),
        compiler_params=_cp(("parallel",)), name="mem_block",
    )(h2, g, wq, mk, mv, wo)


def _mem_attn_cached_body(q_ref, k_ref, v_ref, o_ref):
    ml, heads, hd = k_ref.shape[2:]
    t = q_ref.shape[1] // heads
    shape = (ml * heads, heads * t)
    same = (lax.broadcasted_iota(jnp.int32, shape, 0) % heads) == (lax.broadcasted_iota(jnp.int32, shape, 1) // t)
    reqs = range(q_ref.shape[0])
    rows = lambda ref, n: ref[0, n].reshape(ml * heads, hd).astype(BF16)
    sms = [jnp.where(same, _nt(rows(k_ref, n), q_ref[n]), NEG_INF) for n in reqs]
    ps = []
    for sm in sms:
        m = jnp.max(sm, axis=0, keepdims=True)
        e = jnp.where(same, jnp.exp(sm - m), 0.0)
        ps.append((e / jnp.sum(e, axis=0, keepdims=True)).astype(BF16))
    for n in reqs:
        o_ref[n] = lax.dot_general(ps[n], rows(v_ref, n), (((0,), (0,)), ((), ())),
                                   preferred_element_type=F32).astype(o_ref.dtype)


MEM_REQS_PER_STEP = 2


def _mem_attn_cached(q, cache_k, cache_v, layer):
    b, t, d = q.shape
    _, _, ml, heads, hd = cache_k.shape
    rows = heads * t
    q_rows = jnp.transpose(q.reshape(b, t, heads, hd), (0, 2, 1, 3)).reshape(b, rows, hd)
    per = MEM_REQS_PER_STEP if b % MEM_REQS_PER_STEP == 0 else 1
    kv_spec = pl.BlockSpec((1, per, ml, heads, hd), lambda bi: (layer, bi, 0, 0, 0))
    o = pl.pallas_call(
        _mem_attn_cached_body, grid=(b // per,),
        in_specs=[pl.BlockSpec((per, rows, hd), lambda bi: (bi, 0, 0)), kv_spec, kv_spec],
        out_specs=pl.BlockSpec((per, rows, hd), lambda bi: (bi, 0, 0)),
        out_shape=jax.ShapeDtypeStruct((b, rows, hd), BF16),
        compiler_params=_cp(("parallel",)), name="mem_attn_cached",
    )(q_rows, cache_k, cache_v)
    return jnp.transpose(o.reshape(b, heads, t, hd), (0, 2, 1, 3)).reshape(b, t, d)


FFN_CHUNK = 512


def _ffn_body(final, h_ref, g_ref, wg_ref, wu_ref, wd_ref, *rest):
    if final:
        gf_ref, o_ref = rest
    else:
        (o_ref,) = rest
    h = h_ref[...]
    xn = _rms(h, g_ref[...]).astype(BF16)
    acc = h
    for lo in range(0, D_FF, FFN_CHUNK):
        hi = min(lo + FFN_CHUNK, D_FF)
        gate = _dot(xn, wg_ref[:, lo:hi])
        up = _dot(xn, wu_ref[:, lo:hi])
        act = (gate * _sigmoid(gate) * up).astype(BF16)
        acc = acc + _dot(act, wd_ref[lo:hi, :])
    if final:
        acc = _rms(acc, gf_ref[...])
    o_ref[...] = acc


def _ffn(h2, g, wg, wu, wd, final_g=None):
    m = h2.shape[0]
    tm = _row_tile(m)
    final = final_g is not None
    once = lambda shape: pl.BlockSpec(shape, lambda i: (0, 0), pipeline_mode=pl.Buffered(1))
    in_specs = [_row_spec(tm, D_MODEL), _const_spec((1, D_MODEL)), once(wg.shape), once(wu.shape), once(wd.shape)]
    args = [h2, g, wg, wu, wd]
    if final:
        in_specs.append(_const_spec((1, D_MODEL)))
        args.append(final_g)
    return pl.pallas_call(
        functools.partial(_ffn_body, final), grid=(m // tm,),
        in_specs=in_specs, out_specs=_row_spec(tm, D_MODEL),
        out_shape=jax.ShapeDtypeStruct((m, D_MODEL), F32),
        compiler_params=_cp(("parallel",)), name="ffn",
    )(*args)


ODD_Q = NSA_HEADS * HEAD_DIM
ODD_GATES = 3 * NSA_HEADS
ODD_IN = ODD_Q + 6 * NSA_KV_DIM + ODD_GATES
GATE_PAD = 16
GATE_LANES = NSA_KVH * GATE_PAD
ODD_IN_PAD = ODD_Q + 6 * NSA_KV_DIM + 128


def _odd_in_weights(w_in_odd):
    gates = w_in_odd[:, ODD_Q + 6 * NSA_KV_DIM:].reshape(D_MODEL, NSA_KVH, 3 * NSA_GQA)
    gates = jnp.pad(gates, ((0, 0), (0, 0), (0, GATE_PAD - 3 * NSA_GQA))).reshape(D_MODEL, GATE_LANES)
    return jnp.concatenate([w_in_odd[:, :ODD_Q + 6 * NSA_KV_DIM],
                            jnp.pad(gates, ((0, 0), (0, 128 - GATE_LANES)))], axis=1).astype(BF16)


def _odd_in_body(head_major, h_ref, g_ref, w_ref, c_ref, s_ref, *o_refs):
    xn = _rms(h_ref[...], g_ref[...]).astype(BF16)
    c = c_ref[...]
    s = s_ref[...]

    def mm(lo, hi):
        return _dot(xn, w_ref[:, lo:hi])

    kv = lambda n: mm(ODD_Q + n * NSA_KV_DIM, ODD_Q + (n + 1) * NSA_KV_DIM)
    q = _rope_wide(mm(0, ODD_Q), c, s) * (HEAD_DIM ** -0.5)
    ck, cv = kv(0), kv(1)
    sk, sv = _rope_wide(kv(2), c, s), kv(3)
    wk, wv = _rope_wide(kv(4), c, s), kv(5)
    gates = _sigmoid(mm(ODD_Q + 6 * NSA_KV_DIM, ODD_IN_PAD))[:, :GATE_LANES]
    if head_major:
        ck_rows, cv_rows = o_refs[:2]
        t32 = o_refs[2:8]
        q_ref, g_out, sk_rows, sv_t, wk_rows, wv_t = o_refs[8:14]
        ck_rows[...] = ck
        cv_rows[...] = cv
        for val, r32, r16 in zip((ck, cv, sk, sv, wk, wv), t32, (None, None, None, sv_t, None, wv_t)):
            _store_transposed(val, NSA_KVH, r32, r16)
        _store_transposed(q, NSA_HEADS, None, q_ref)
        _store_head_major(sk, NSA_KVH, sk_rows)
        _store_head_major(wk, NSA_KVH, wk_rows)
        g_out[0] = gates.T.reshape(NSA_KVH, GATE_PAD, gates.shape[0])
    else:
        for ref, val in zip(o_refs[:6], (ck, cv, sk, sv, wk, wv)):
            ref[...] = val
        q_ref, g_out = o_refs[6:]
        q_ref[...] = q.astype(BF16)
        g_out[...] = gates


def _odd_in(h2, g, w, cos_t, sin_t, batch, head_major):
    m = h2.shape[0]
    t = m // batch
    tm = _row_tile(m)
    nt = max(t // tm, 1)
    ntab = cos_t.shape[0] // tm
    tab_spec = pl.BlockSpec((tm, 128), lambda i: (i % ntab, 0))
    kv = jax.ShapeDtypeStruct((m, NSA_KV_DIM), F32)
    if head_major:
        kt = lambda dt: jax.ShapeDtypeStruct((batch, NSA_KVH, HEAD_DIM, t), dt)
        rows16 = jax.ShapeDtypeStruct((batch, NSA_KVH, t, HEAD_DIM), BF16)
        outs = [kv] * 2 + [kt(F32)] * 6
        outs += [jax.ShapeDtypeStruct((batch, NSA_HEADS, HEAD_DIM, t), BF16),
                 jax.ShapeDtypeStruct((batch, NSA_KVH, GATE_PAD, t), F32), rows16, kt(BF16), rows16, kt(BF16)]
        rows_spec = _hm_spec(NSA_KVH, tm, HEAD_DIM, nt)
        specs = [_row_spec(tm, NSA_KV_DIM)] * 2 + [_tr_spec(NSA_KVH, tm, nt)] * 6
        specs += [_tr_spec(NSA_HEADS, tm, nt),
                  pl.BlockSpec((1, NSA_KVH, GATE_PAD, tm), lambda i: (i // nt, 0, 0, i % nt)),
                  rows_spec, _tr_spec(NSA_KVH, tm, nt), rows_spec, _tr_spec(NSA_KVH, tm, nt)]
    else:
        outs = [kv] * 6 + [jax.ShapeDtypeStruct((m, ODD_Q), BF16), jax.ShapeDtypeStruct((m, GATE_LANES), F32)]
        specs = [_row_spec(tm, NSA_KV_DIM)] * 6 + [_row_spec(tm, ODD_Q), _row_spec(tm, GATE_LANES)]
    return pl.pallas_call(
        functools.partial(_odd_in_body, head_major), grid=(m // tm,),
        in_specs=[_row_spec(tm, D_MODEL), _const_spec((1, D_MODEL)), _const_spec(w.shape), tab_spec, tab_spec],
        out_specs=specs, out_shape=outs,
        compiler_params=_cp(("parallel",)), name="odd_in",
    )(h2, g, w, cos_t, sin_t)


def _gelu_tanh(x):
    return 0.5 * x * (1.0 + jnp.tanh(math.sqrt(2.0 / math.pi) * (x + 0.044715 * x * x * x)))


def _compress_body(add_pe, use_rope, x_ref, pe_ref, w1_ref, w2_ref, c_ref, s_ref, o_ref, acc_ref, wbd_ref):
    l = pl.program_id(0)

    @pl.when(l == 0)
    def _():
        acc_ref[...] = jnp.zeros_like(acc_ref)
        wbd_ref[...] = jnp.zeros_like(wbd_ref)

    for gg in range(NSA_KVH):
        wbd_ref[gg * HEAD_DIM:(gg + 1) * HEAD_DIM, gg * CMP_HID:(gg + 1) * CMP_HID] = w1_ref[0]
    x = x_ref[...]
    if add_pe:
        x = (x + pe_ref[0]).astype(BF16)
    acc_ref[...] += _dot(x, wbd_ref[...])

    @pl.when(l == CMP_BLK - 1)
    def _():
        hid = _gelu_tanh(acc_ref[...]).astype(BF16)
        y = _dot(hid, w2_ref[...])
        if use_rope:
            y = _rope_wide(y, c_ref[...], s_ref[...])
        o_ref[...] = y


def _compress(x2, pe, w1, w2, tables, add_pe):
    rows = x2.shape[0]
    pe_t = jnp.tile(pe, (1, NSA_KVH))[:, None, :]
    w2bd = jnp.kron(jnp.eye(NSA_KVH, dtype=F32), w2).astype(BF16)
    use_rope = tables is not None
    if use_rope:
        c_t, s_t = tables
    else:
        c_t = s_t = jnp.zeros((rows, 128), F32)
    return pl.pallas_call(
        functools.partial(_compress_body, add_pe, use_rope), grid=(CMP_BLK,),
        in_specs=[pl.BlockSpec((rows, NSA_KV_DIM), lambda l: (0, l)),
                  pl.BlockSpec((1, 1, NSA_KV_DIM), lambda l: (l, 0, 0)),
                  pl.BlockSpec((1, HEAD_DIM, CMP_HID), lambda l: (l, 0, 0)),
                  _const_spec(w2bd.shape), _const_spec((rows, 128)), _const_spec((rows, 128))],
        out_specs=_const_spec((rows, NSA_KV_DIM)),
        out_shape=jax.ShapeDtypeStruct((rows, NSA_KV_DIM), F32),
        scratch_shapes=[pltpu.VMEM((rows, NSA_KVH * CMP_HID), F32),
                        pltpu.VMEM((NSA_KV_DIM, NSA_KVH * CMP_HID), BF16)],
        compiler_params=_cp(("arbitrary",)), name="compress",
    )(x2, pe_t, w1.astype(BF16), w2bd, c_t, s_t)


NB_PAD = 128
RANK_ROWS = 64


def _cmp_and_select(q, kc, vc, qpos, groups, rows_per_q, nb):
    tq = qpos.shape[0]
    rows = groups * rows_per_q * tq
    s = _nt(q, kc).reshape(groups, rows_per_q, tq, NB_PAD)
    n_idx = lax.broadcasted_iota(jnp.int32, (tq, NB_PAD), 1)
    valid = ((n_idx * CMP_BLK + (CMP_BLK - 1)) <= qpos)[None, None]
    sm = jnp.where(valid, s, NEG_INF)
    m = jnp.max(sm, axis=-1, keepdims=True)
    p = jnp.where(valid, jnp.exp(sm - m), 0.0)
    p = p / jnp.maximum(jnp.sum(p, axis=-1, keepdims=True), 1e-30)
    o_cmp = _dot(p.reshape(rows, NB_PAD).astype(BF16), vc)
    cur = qpos // CMP_BLK
    forced = ((n_idx == 0) | (n_idx == cur) | (n_idx == cur - 1))[None]
    score = jnp.where((n_idx > cur)[None], -1.0, jnp.where(forced, FORCE_SCORE, jnp.sum(p, axis=1)))
    nq = groups * tq
    nqp = -(-nq // 128) * 128
    score = score.reshape(nq, NB_PAD)
    if nqp > nq:
        score = jnp.concatenate([score, jnp.zeros((nqp - nq, NB_PAD), F32)], axis=0)
    sel_t = _top_n_mask(score.T[:RANK_ROWS], nb)
    sel = jnp.concatenate([sel_t, jnp.zeros((NB_PAD - RANK_ROWS, nqp), F32)], axis=0).T[:nq]
    sel = jnp.broadcast_to(sel.reshape(groups, 1, tq, NB_PAD), (groups, rows_per_q, tq, NB_PAD))
    return o_cmp, sel.reshape(rows, NB_PAD)


def _top_n_mask(st, nb):
    groups = RANK_ROWS // 8
    blocks = [st[8 * g:8 * g + 8] for g in range(groups)]
    cnt = [jnp.zeros_like(b) for b in blocks]
    row = lax.broadcasted_iota(jnp.int32, blocks[0].shape, 0)
    for i in range(nb):
        r = st[i:i + 1]
        for g in range(groups):
            gt = jnp.where(r > blocks[g], 1.0, 0.0)
            if 8 * g + 7 <= i:
                inc = gt
            else:
                ge = jnp.where(r >= blocks[g], 1.0, 0.0)
                inc = ge if 8 * g > i else jnp.where(row + 8 * g > i, ge, gt)
            cnt[g] = cnt[g] + inc
    return jnp.concatenate([jnp.where(c < float(SEL_TOPN), 1.0, 0.0) for c in cnt], axis=0)


def _osm(z, st, v, keys_axis, v_keys_minor):
    m, l, acc = st
    m_new = jnp.maximum(m, jnp.max(z, axis=keys_axis, keepdims=True))
    alpha = jnp.exp(m - m_new)
    p = jnp.exp(z - m_new)
    l = alpha * l + jnp.sum(p, axis=keys_axis, keepdims=True)
    pb = p.astype(BF16)
    if keys_axis == 1:
        pv = _nt(pb, v) if v_keys_minor else _dot(pb, v)
    else:
        pv = _dot(v, pb)
    return m_new, l, alpha * acc + pv


def _softmax_tiles(zs, vs):
    m = functools.reduce(jnp.maximum, [jnp.max(z, axis=1, keepdims=True) for z in zs])
    ps = [jnp.exp(z - m) for z in zs]
    l = functools.reduce(lambda a, b: a + b, [jnp.sum(p, axis=1, keepdims=True) for p in ps])
    acc = None
    for p, (v, keys_minor) in zip(ps, vs):
        pv = _nt(p.astype(BF16), v) if keys_minor else _dot(p.astype(BF16), v)
        acc = pv if acc is None else acc + pv
    return acc / jnp.maximum(l, 1e-30)


def _osm_groups_t(q_ts, k_rows, v_t, bias, sts):
    n = len(q_ts)
    z_next = _dot(k_rows, q_ts[0]) + bias
    out = []
    for r in range(n):
        z = z_next
        if r + 1 < n:
            z_next = _dot(k_rows, q_ts[r + 1]) + bias
        out.append(_osm(z, sts[r], v_t, 0, True))
    return tuple(out)


def _osm_init(rows, width, keys_axis=1):
    if keys_axis == 1:
        return (jnp.full((rows, 1), NEG_INF, F32), jnp.zeros((rows, 1), F32), jnp.zeros((rows, width), F32))
    return (jnp.full((1, rows), NEG_INF, F32), jnp.zeros((1, rows), F32), jnp.zeros((width, rows), F32))


def _osm_out(st):
    return st[2] / jnp.maximum(st[1], 1e-30)


def _nsa_prompt_body(nb, q_ref, kc_ref, vc_ref, sk_ref, sv_ref, wk_ref, wv_ref, g_ref, o_ref, sb_ref):
    i = pl.program_id(2)
    tq = q_ref.shape[3]
    tk = tq
    rq = NSA_GQA
    q_ts = [q_ref[0, r] for r in range(rq)]
    qpos = i * tq + lax.broadcasted_iota(jnp.int32, (1, tq), 1)

    n_idx = lax.broadcasted_iota(jnp.int32, (NB_PAD, tq), 0)
    valid = (n_idx * CMP_BLK + (CMP_BLK - 1)) <= qpos
    kc = kc_ref[0, 0]
    vc_t = vc_ref[0, 0]
    o_cmp, score = [], None
    for r in range(rq):
        sm = jnp.where(valid, _dot(kc, q_ts[r]), NEG_INF)
        m = jnp.max(sm, axis=0, keepdims=True)
        p = jnp.where(valid, jnp.exp(sm - m), 0.0)
        p = p / jnp.maximum(jnp.sum(p, axis=0, keepdims=True), 1e-30)
        o_cmp.append(_dot(vc_t, p.astype(BF16)))
        score = p if score is None else score + p
    cur = qpos // CMP_BLK
    forced = (n_idx == 0) | (n_idx == cur) | (n_idx == cur - 1)
    score = jnp.where(n_idx > cur, -1.0, jnp.where(forced, FORCE_SCORE, score))
    sel_t = _top_n_mask(score[:RANK_ROWS], nb)
    sb_ref[...] = (sel_t - 1.0) * 1e30

    key = lax.broadcasted_iota(jnp.int32, (tk, tq), 0)
    qry = lax.broadcasted_iota(jnp.int32, (tk, tq), 1)
    init = tuple(_osm_init(tq, HEAD_DIM, 0) for _ in range(rq))

    def sel_tile(j, sts, diag):
        start = pl.multiple_of(j * tk, tk)
        per = tk // CMP_BLK
        bias = jnp.concatenate([jnp.broadcast_to(sb_ref[pl.ds(j * per + b, 1), :], (CMP_BLK, tq))
                                for b in range(per)], axis=0)
        if diag:
            bias = jnp.where(key <= qry, bias, NEG_INF)
        return _osm_groups_t(q_ts, sk_ref[0, 0, pl.ds(start, tk), :], sv_ref[0, 0, :, pl.ds(start, tk)], bias, sts)

    sts = lax.fori_loop(0, i, lambda j, s: sel_tile(j, s, False), init)
    o_sel = [_osm_out(st) for st in sel_tile(i, sts, True)]

    def win_tile(back, sts):
        j_raw = i - back
        start = pl.multiple_of(jnp.maximum(j_raw, 0) * tk, tk)
        if back == 0:
            ok = key <= qry
        elif back < WINDOW // tk:
            ok = key >= 0
        else:
            ok = key > qry
        bias = jnp.where(ok & (j_raw >= 0), 0.0, NEG_INF)
        return _osm_groups_t(q_ts, wk_ref[0, 0, pl.ds(start, tk), :], wv_ref[0, 0, :, pl.ds(start, tk)], bias, sts)

    sts = init
    for back in range(WINDOW // tk + 1):
        sts = win_tile(back, sts)
    o_win = [_osm_out(st) for st in sts]

    gates = g_ref[0, 0]
    outs = []
    for r in range(rq):
        gc, gs, gw = (gates[3 * r + n:3 * r + n + 1, :] for n in range(3))
        outs.append(gc * o_cmp[r] + gs * o_sel[r] + gw * o_win[r])
    o_ref[0] = jnp.concatenate(outs, axis=0).T.astype(o_ref.dtype)


def _nsa_prompt(q_t, kc_hm, vc_t, sk_hm, sv_t, wk_hm, wv_t, gates_t, nb):
    b, _, d, t = q_t.shape
    tq = min(NSA_TILE, t)
    assert WINDOW % tq == 0 and nb <= RANK_ROWS and tq % 128 == 0 and tq % CMP_BLK == 0
    per_group = lambda shape: pl.BlockSpec((1, 1) + shape, lambda bi, g, i: (bi, g, 0, 0))
    return pl.pallas_call(
        functools.partial(_nsa_prompt_body, nb),
        grid=(b, NSA_KVH, t // tq),
        in_specs=[pl.BlockSpec((1, NSA_GQA, d, tq), lambda bi, g, i: (bi, g, 0, i)),
                  per_group((NB_PAD, d)), per_group((d, NB_PAD)),
                  per_group((t, d)), per_group((d, t)), per_group((t, d)), per_group((d, t)),
                  pl.BlockSpec((1, 1, GATE_PAD, tq), lambda bi, g, i: (bi, g, 0, i))],
        out_specs=pl.BlockSpec((1, tq, NSA_GQA * d), lambda bi, g, i: (bi, i, g)),
        out_shape=jax.ShapeDtypeStruct((b, t, NSA_HEADS * d), BF16),
        scratch_shapes=[pltpu.VMEM((RANK_ROWS, tq), F32)],
        compiler_params=_cp(("parallel", "parallel", "arbitrary")), name="nsa_prompt",
    )(q_t, kc_hm, vc_t, sk_hm, sv_t, wk_hm, wv_t, gates_t)


def _cmp_gather_body(npages, pt_ref, pe_ref, *refs):
    page_refs = refs[:npages]
    o_ref = refs[npages]
    tok_ref = refs[npages + 1]
    pe2 = pe_ref[...]
    lane_tiles = NSA_KV_DIM // 128
    for p in range(npages):
        tok = page_refs[p][0].reshape(NSA_KV_DIM, PAGE).T + pe2
        for c in range(lane_tiles):
            tok_ref[c, p * PAGE:(p + 1) * PAGE, :] = tok[:, c * 128:(c + 1) * 128]
    nblk = npages * PAGE // CMP_BLK
    mid_ref = refs[npages + 2]
    for lo in range(8):
        for c in range(lane_tiles):
            mid_ref[c, lo] = tok_ref[c, pl.ds(lo, npages * PAGE // 8, stride=8), :]

    for l in range(CMP_BLK):
        for c in range(lane_tiles):
            lanes = l * NSA_KV_DIM + c * 128
            o_ref[0, :, lanes:lanes + 128] = mid_ref[c, l % 8, pl.ds(l // 8, nblk, stride=8), :].astype(BF16)


def _cmp_gather(cache, pe, page_table):
    b, npages = page_table.shape
    nblk = npages * PAGE // CMP_BLK
    width = CMP_BLK * NSA_KV_DIM
    pe2 = jnp.tile(pe, (PAGE // CMP_BLK, NSA_KVH))
    grid_spec = pltpu.PrefetchScalarGridSpec(
        num_scalar_prefetch=1, grid=(b,),
        in_specs=[pl.BlockSpec((PAGE, NSA_KV_DIM), lambda bi, pt: (0, 0))] + _page_specs(npages, NSA_KVH),
        out_specs=pl.BlockSpec((1, nblk, width), lambda bi, pt: (bi, 0, 0)),
        scratch_shapes=[pltpu.VMEM((NSA_KV_DIM // 128, npages * PAGE, 128), F32),
                        pltpu.VMEM((NSA_KV_DIM // 128, 8, npages * PAGE // 8, 128), F32)])
    return pl.pallas_call(
        functools.partial(_cmp_gather_body, npages), grid_spec=grid_spec,
        out_shape=jax.ShapeDtypeStruct((b, nblk, width), BF16),
        compiler_params=_cp(("parallel",)), name="cmp_gather",
    )(page_table, pe2, *([cache] * npages))


def _nsa_sample_body(npages, t_new, past, nb, pt_ref, qbd_ref, kc_ref, vc_ref, g_ref,
                     skn_ref, svn_ref, wkn_ref, wvn_ref, wkc_ref, wvc_ref, *refs):
    sk_refs = refs[:npages]
    sv_refs = refs[npages:2 * npages]
    o_ref = refs[2 * npages]
    qbd = qbd_ref[0]
    rows = qbd.shape[0]
    w = qbd.shape[1]
    rq = rows // t_new
    t_idx = lax.broadcasted_iota(jnp.int32, (t_new, 1), 0)
    qpos = past + t_idx

    o_cmp, sel = _cmp_and_select(qbd, kc_ref[0], vc_ref[0], qpos, NSA_KVH, NSA_GQA, nb)
    sel_bias = (sel - 1.0) * 1e30

    qoff = lax.broadcasted_iota(jnp.int32, (rows, PAGE), 0) % t_new
    col = lax.broadcasted_iota(jnp.int32, (rows, PAGE), 1)
    pad = jnp.zeros((PAGE - t_new, w), F32)
    new_bias = jnp.where(col <= qoff, 0.0, NEG_INF)

    def padded(ref):
        return jnp.concatenate([ref[0], pad], axis=0).astype(BF16)

    half = col < CMP_BLK
    zs, vs = [], []
    for p in range(npages):
        b0 = sel_bias[:, 2 * p:2 * p + 1]
        b1 = sel_bias[:, 2 * p + 1:2 * p + 2]
        zs.append(_dot(qbd, sk_refs[p][0].reshape(w, PAGE).astype(BF16)) + jnp.where(half, b0, b1))
        vs.append((sv_refs[p][0].reshape(w, PAGE).astype(BF16), True))
    last = (past // CMP_BLK)
    zs.append(_nt(qbd, padded(skn_ref)) + new_bias + sel_bias[:, last:last + 1])
    vs.append((padded(svn_ref), False))
    o_sel = _softmax_tiles(zs, vs)

    zs, vs = [_nt(qbd, padded(wkn_ref)) + new_bias], [(padded(wvn_ref), False)]
    wb = wkc_ref.shape[3]
    for c in range(wb // PAGE):
        kpos = (past - wb) + c * PAGE + col
        diff = (past + qoff) - kpos
        bias = jnp.where((diff < WINDOW) & (kpos >= 0), 0.0, NEG_INF)
        chunk = lambda ref: ref[0, :, :, c * PAGE:(c + 1) * PAGE].reshape(w, PAGE).astype(BF16)
        zs.append(_dot(qbd, chunk(wkc_ref)) + bias)
        vs.append((chunk(wvc_ref), True))
    o_win = _softmax_tiles(zs, vs)

    gates = g_ref[0]
    o_ref[0] = gates[:, 0:1] * o_cmp + gates[:, 1:2] * o_sel + gates[:, 2:3] * o_win


def _nsa_sample(q, kc, vc, gates, sk_new, sv_new, wk_new, wv_new, win_k, win_v,
                cache_sk, cache_sv, page_table, past, nb):
    b, t, w = sk_new.shape
    npages = page_table.shape[1]
    assert nb <= RANK_ROWS and (past // CMP_BLK) < nb
    qbd = _block_diag_rows(q.reshape(b, t, NSA_HEADS, HEAD_DIM), NSA_KVH)
    rows = NSA_HEADS * t
    g_rows = jnp.transpose(gates.reshape(b, t, NSA_KVH, NSA_GQA, 3), (0, 2, 3, 1, 4)).reshape(b, rows, 3)
    bspec = lambda r, c: pl.BlockSpec((1, r, c), lambda bi, pt: (bi, 0, 0))
    wb = win_k.shape[3]
    win_spec = pl.BlockSpec((1, NSA_KVH, HEAD_DIM, wb), lambda bi, pt: (bi, 0, 0, 0))
    grid_spec = pltpu.PrefetchScalarGridSpec(
        num_scalar_prefetch=1, grid=(b,),
        in_specs=[bspec(rows, w), bspec(NB_PAD, w), bspec(NB_PAD, w), bspec(rows, 3),
                  bspec(t, w), bspec(t, w), bspec(t, w), bspec(t, w), win_spec, win_spec]
                 + _page_specs(npages, NSA_KVH) + _page_specs(npages, NSA_KVH),
        out_specs=bspec(rows, w))
    o = pl.pallas_call(
        functools.partial(_nsa_sample_body, npages, t, past, nb), grid_spec=grid_spec,
        out_shape=jax.ShapeDtypeStruct((b, rows, w), F32),
        compiler_params=_cp(("parallel",)), name="nsa_sample",
    )(page_table, qbd, kc, vc, g_rows, sk_new, sv_new, wk_new, wv_new, win_k, win_v,
      *([cache_sk] * npages), *([cache_sv] * npages))
    return _diag_rows_out(o, t, NSA_KVH)


def _common_tail(h2, batch, l, mem_k, mem_v, p, final_g):
    t = h2.shape[0] // batch
    if mem_k.ndim == 5:
        q = _norm_proj(h2, p["norm_mem"][l][None], p["w_mem_q"][l], MEM_HD ** -0.5).reshape(batch, t, D_MODEL)
        o = _mem_attn_cached(q, mem_k, mem_v, l)
        h2 = _proj_res(h2, [o.reshape(batch * t, D_MODEL)], p["w_mem_o"][l])
    else:
        h2 = _mem_block(h2, p["norm_mem"][l][None], p["w_mem_q"][l], mem_k, mem_v, p["w_mem_o"][l], batch)
    return _ffn(h2, p["norm_ffn"][l][None], p["w_ffn_gate"][l], p["w_ffn_up"][l], p["w_ffn_down"][l], final_g)


def _even_layer(h2, batch, p, conv_buf, sample_ctx):
    t = h2.shape[0] // batch
    prompt = sample_ctx is None
    res = _even_in(h2, p["norm_mix"][0][None], p["w_in_even"][0], batch, prompt)
    glu3 = res[0].reshape(batch, t, CONV_CH)
    y_conv = _conformer_conv(glu3, conv_buf, p["conv_w"][0], p["conv_b"][0], p["conv_ln_g"][0], p["conv_ln_b"][0])
    new_buf = jnp.concatenate([conv_buf.astype(F32), glu3], axis=1)[:, -(CONV_W - 1):]
    shp = (batch, t, SB_HEADS, HEAD_DIM)
    if prompt:
        o = _sb_prompt(res[3], res[4], res[5])
        k_out, v_out = _from_transposed(res[1]), _from_transposed(res[2])
    else:
        cache_k, cache_v, page_table = sample_ctx
        k32, v32 = res[1], res[2]
        o = _sb_sample(res[3].reshape(batch, t, SB_DIM), k32.reshape(batch, t, SB_DIM),
                       v32.reshape(batch, t, SB_DIM), cache_k, cache_v, page_table)
        k_out, v_out = k32.reshape(shp), v32.reshape(shp)
    h2 = _proj_res(h2, [y_conv.reshape(batch * t, CONV_CH), o.reshape(batch * t, SB_DIM)], p["w_mix_out"][0])
    return h2, new_buf, k_out, v_out


def _odd_layer(h2, batch, p, sample_ctx):
    t = h2.shape[0] // batch
    prompt = sample_ctx is None
    past = 0 if prompt else sample_ctx["past"]
    tm = _row_tile(h2.shape[0])
    pos = past + jnp.arange(max(t, tm), dtype=jnp.int32) % t
    cos_t, sin_t = _rope_tables(pos, 128)
    res = _odd_in(h2, p["norm_mix"][1][None], p["w_in_odd"], cos_t, sin_t, batch, prompt)
    ck, cv = res[0], res[1]
    nb = -(-(past + t) // CMP_BLK)
    cw = (p["cmp_pe_k"][0], p["cmp_w1_k"][0], p["cmp_w2_k"][0]), (p["cmp_pe_v"][0], p["cmp_w1_v"][0], p["cmp_w2_v"][0])
    width = CMP_BLK * NSA_KV_DIM

    def end_tables(first, count):
        blk_end = (first + jnp.arange(count, dtype=jnp.int32)) * CMP_BLK + (CMP_BLK - 1)
        return tuple(jnp.tile(x, (batch, 1)) for x in _rope_tables(blk_end, 128))

    if prompt:
        kc = _compress(ck.reshape(batch * nb, width), *cw[0], end_tables(0, nb), True)
        vc = _compress(cv.reshape(batch * nb, width), *cw[1], None, True)
    else:
        nbp = past // CMP_BLK
        assert past % CMP_BLK == 0 and nb == nbp + 1
        pt = sample_ctx["page_table"]
        new_block = lambda x: jnp.pad(x.reshape(batch, t, NSA_KV_DIM), ((0, 0), (0, CMP_BLK - t), (0, 0))).reshape(batch, width)
        halves = []
        for cache, new, w, tabs in ((sample_ctx["cmp_k"], ck, cw[0], (end_tables(0, nbp), end_tables(nbp, 1))),
                                    (sample_ctx["cmp_v"], cv, cw[1], (None, None))):
            x_past = _cmp_gather(cache, w[0], pt).reshape(batch * nbp, width)
            c_past = _compress(x_past, *w, tabs[0], False).reshape(batch, nbp, NSA_KV_DIM)
            c_new = _compress(new_block(new), *w, tabs[1], True).reshape(batch, 1, NSA_KV_DIM)
            halves.append(jnp.concatenate([c_past, c_new], axis=1))
        kc, vc = halves
    kc = kc.reshape(batch, nb, NSA_KV_DIM)
    vc = vc.reshape(batch, nb, NSA_KV_DIM)
    padc = lambda x: jnp.pad(x, ((0, 0), (0, NB_PAD - nb), (0, 0))).astype(BF16)
    kvshape = (batch, t, NSA_KVH, HEAD_DIM)
    if prompt:
        hm = lambda x: jnp.transpose(padc(x).reshape(batch, NB_PAD, NSA_KVH, HEAD_DIM), (0, 2, 1, 3))
        t32 = res[2:8]
        q_t, gates_t, sk_hm, sv_t, wk_hm, wv_t = res[8:]
        o = _nsa_prompt(q_t, hm(kc), jnp.swapaxes(hm(vc), 2, 3), sk_hm, sv_t, wk_hm, wv_t, gates_t, nb)
        keep = min(WINDOW, t)
        outs = [_from_transposed(x) for x in t32[:4]] + [_from_transposed(x[..., -keep:]) for x in t32[4:]]
    else:
        sk, sv, wk, wv = res[2:6]
        q, gates = res[6:]
        gates = gates.reshape(batch, t, NSA_KVH, GATE_PAD)[..., :3 * NSA_GQA]
        r3 = lambda x: x.reshape(batch, t, NSA_KV_DIM)
        wkc, wvc = sample_ctx["win_k"], sample_ctx["win_v"]
        wb = wkc.shape[1]
        o = _nsa_sample(q.reshape(batch, t, ODD_Q), padc(kc), padc(vc), gates,
                        r3(sk), r3(sv), r3(wk), r3(wv), _keys_minor(wkc), _keys_minor(wvc),
                        sample_ctx["sel_k"], sample_ctx["sel_v"], sample_ctx["page_table"], past, nb)
        r4 = lambda x: x.reshape(kvshape)
        outs = [r4(ck), r4(cv), r4(sk), r4(sv),
                jnp.concatenate([wkc, r4(wk)], axis=1)[:, -wb:], jnp.concatenate([wvc, r4(wv)], axis=1)[:, -wb:]]
    h2 = _proj_res(h2, [o.reshape(batch * t, ODD_Q)], p["w_mix_out"][1])
    return (h2,) + tuple(outs)


def _trunk(x, p, conv_buf, mem_kv, even_ctx, odd_ctx):
    batch, t, _ = x.shape
    h2 = x.reshape(batch * t, D_MODEL)
    h2, new_buf, sbk, sbv = _even_layer(h2, batch, p, conv_buf, even_ctx)
    h2 = _common_tail(h2, batch, 0, mem_kv[0][0], mem_kv[0][1], p, None)
    h2, ck, cv, sk, sv, wk, wv = _odd_layer(h2, batch, p, odd_ctx)
    h2 = _common_tail(h2, batch, 1, mem_kv[1][0], mem_kv[1][1], p, p["final_norm"][None])
    st = lambda a: a[None]
    return (h2.reshape(batch, t, D_MODEL), st(sbk), st(sbv), st(new_buf),
            st(ck), st(cv), st(sk), st(sv), st(wk), st(wv))


def kernel(x_prompt, x_sample, mem_prompt, cache_sb_k, cache_sb_v, state_conv,
           cache_nsa_cmp_k, cache_nsa_cmp_v, cache_nsa_sel_k, cache_nsa_sel_v,
           cache_nsa_win_k, cache_nsa_win_v, cache_mem_k, cache_mem_v, page_table,
           norm_mix, norm_mem, norm_ffn, final_norm, w_in_even, w_in_odd, w_mix_out,
           conv_w, conv_b, conv_ln_g, conv_ln_b,
           cmp_pe_k, cmp_w1_k, cmp_w2_k, cmp_pe_v, cmp_w1_v, cmp_w2_v,
           w_mem_q, w_mem_k, w_mem_v, w_mem_o, w_ffn_gate, w_ffn_up, w_ffn_down):
    assert norm_mix.shape[0] == 2 and w_in_even.shape[0] == 1 and w_in_odd.shape[0] == 1
    bp = x_prompt.shape[0]
    bs = x_sample.shape[0]
    bf = lambda w: w.astype(BF16)
    p = dict(
        norm_mix=norm_mix, norm_mem=norm_mem, norm_ffn=norm_ffn, final_norm=final_norm,
        w_in_even=bf(w_in_even),
        w_in_odd=_odd_in_weights(w_in_odd[0]),
        w_mix_out=bf(w_mix_out), conv_w=conv_w, conv_b=conv_b, conv_ln_g=conv_ln_g, conv_ln_b=conv_ln_b,
        cmp_pe_k=cmp_pe_k, cmp_w1_k=cmp_w1_k, cmp_w2_k=cmp_w2_k,
        cmp_pe_v=cmp_pe_v, cmp_w1_v=cmp_w1_v, cmp_w2_v=cmp_w2_v,
        w_mem_q=bf(w_mem_q), w_mem_o=bf(w_mem_o),
        w_ffn_gate=bf(w_ffn_gate), w_ffn_up=bf(w_ffn_up), w_ffn_down=bf(w_ffn_down))

    ml = mem_prompt.shape[1]
    mem = _mem_kv(mem_prompt.reshape(bp * ml, D_MODEL),
                  [bf(w_mem_k[0]), bf(w_mem_v[0]), bf(w_mem_k[1]), bf(w_mem_v[1])])
    m3 = lambda a: a.reshape(bp, ml, D_MODEL)
    m4 = lambda a: a.reshape(bp, ml, MEM_HEADS, MEM_HD)
    mem_kv_p = [(m3(mem[1]), m3(mem[3])), (m3(mem[5]), m3(mem[7]))]
    mem_k_p = jnp.stack([m4(mem[0]), m4(mem[4])])
    mem_v_p = jnp.stack([m4(mem[2]), m4(mem[6])])
    zero_buf = jnp.zeros((bp, CONV_W - 1, CONV_CH), F32)
    outs_p = _trunk(x_prompt, p, zero_buf, mem_kv_p, None, None)

    past = page_table.shape[1] * PAGE
    even_ctx = (_keys_minor(cache_sb_k[0]), _keys_minor(cache_sb_v[0]), page_table)
    odd_ctx = dict(past=past, page_table=page_table,
                   cmp_k=_keys_minor(cache_nsa_cmp_k[0]), cmp_v=_keys_minor(cache_nsa_cmp_v[0]),
                   sel_k=_keys_minor(cache_nsa_sel_k[0]), sel_v=_keys_minor(cache_nsa_sel_v[0]),
                   win_k=cache_nsa_win_k[0], win_v=cache_nsa_win_v[0])
    mem_kv_s = [(cache_mem_k, cache_mem_v)] * cache_mem_k.shape[0]
    outs_s = _trunk(x_sample, p, state_conv[0], mem_kv_s, even_ctx, odd_ctx)

    return (outs_p[0], outs_s[0]) + tuple(outs_p[1:]) + (mem_k_p, mem_v_p) + tuple(outs_s[1:])
```

```python
import functools
import math

import jax
import jax.numpy as jnp
from jax import lax
from jax.experimental import pallas as pl
from jax.experimental.pallas import tpu as pltpu

F32 = jnp.float32
BF16 = jnp.bfloat16

D_MODEL = 1024
HEAD_DIM = 64
CONV_CH = 512
CONV_W = 31
SB_HEADS = 8
SB_DIM = 512
NSA_HEADS = 16
NSA_GQA = 4
NSA_KVH = 4
NSA_KV_DIM = 256
CMP_BLK = 64
CMP_HID = 256
SEL_TOPN = 16
FORCE_SCORE = 1.0e4
WINDOW = 512
MEM_HEADS = 4
MEM_HD = 256
D_FF = 2816
ROPE_THETA = 10000.0
NORM_EPS = 1e-6
NEG_INF = -1e30
PAGE = 128

V7X_VMEM_BYTES = 64 * 1024 * 1024
VMEM_LIMIT = V7X_VMEM_BYTES - 8 * 1024 * 1024
ATT_TILE = 256
NSA_TILE = 512


def _cp(sem):
    return pltpu.CompilerParams(dimension_semantics=sem, vmem_limit_bytes=VMEM_LIMIT)


def _nt(a, b):
    return lax.dot_general(a, b, (((1,), (1,)), ((), ())), preferred_element_type=F32)


def _dot(a, b):
    return jnp.dot(a, b, preferred_element_type=F32)


def _rms(x, g):
    y = x * lax.rsqrt(jnp.mean(x * x, axis=-1, keepdims=True) + NORM_EPS)
    return y * g


def _sigmoid(x):
    return 1.0 / (1.0 + jnp.exp(-x))


def _const_spec(shape):
    n = len(shape)
    return pl.BlockSpec(shape, lambda *a: (0,) * n)


def _row_spec(tm, n):
    return pl.BlockSpec((tm, n), lambda i: (i, 0))


def _hm_spec(heads, tm, width, nt):
    return pl.BlockSpec((1, heads, tm, width), lambda i: (i // nt, 0, i % nt, 0))


def _tr_spec(heads, tm, nt):
    return pl.BlockSpec((1, heads, HEAD_DIM, tm), lambda i: (i // nt, 0, 0, i % nt))


def _store_transposed(x, heads, f32_ref, bf16_ref):
    xt = x.T.reshape(heads, HEAD_DIM, x.shape[0])
    if f32_ref is not None:
        f32_ref[0] = xt
    if bf16_ref is not None:
        bf16_ref[0] = xt.astype(BF16)


def _store_head_major(x, heads, bf16_ref):
    for hh in range(heads):
        bf16_ref[0, hh] = x[:, hh * HEAD_DIM:(hh + 1) * HEAD_DIM].astype(BF16)


def _from_transposed(xt):
    return jnp.transpose(xt, (0, 3, 1, 2))


def _row_tile(m):
    return 512 if m >= 4096 else min(256, m)


def _rope_tables(pos, width):
    half = HEAD_DIM // 2
    inv = ROPE_THETA ** (-jnp.arange(half, dtype=F32) / half)
    ang = pos.astype(F32)[:, None] * inv[None, :]
    cos = jnp.cos(ang)
    sin = jnp.sin(ang)
    c = jnp.concatenate([cos, cos], axis=-1)
    s = jnp.concatenate([-sin, sin], axis=-1)
    reps = width // HEAD_DIM
    return jnp.tile(c, (1, reps)), jnp.tile(s, (1, reps))


def _rope128(x, c, s):
    lane = lax.broadcasted_iota(jnp.int32, x.shape, 1)
    first = (lane % HEAD_DIM) < (HEAD_DIM // 2)
    rot = jnp.where(first, pltpu.roll(x, 96, 1), pltpu.roll(x, 32, 1))
    return x * c + rot * s


def _rope_wide(x, c, s):
    n = x.shape[1] // 128
    return jnp.concatenate([_rope128(x[:, i * 128:(i + 1) * 128], c, s) for i in range(n)], axis=1)


def _even_in_body(head_major, h_ref, g_ref, w_ref, glu_ref, k32_ref, v32_ref, q_ref, *hm_refs):
    xn = _rms(h_ref[...], g_ref[...]).astype(BF16)

    def mm(lo, hi):
        return _dot(xn, w_ref[:, lo:hi])

    glu_ref[...] = mm(0, CONV_CH) * _sigmoid(mm(CONV_CH, 2 * CONV_CH))
    base = 2 * CONV_CH
    q = mm(base, base + SB_DIM) * (HEAD_DIM ** -0.5)
    k = mm(base + SB_DIM, base + 2 * SB_DIM)
    v = mm(base + 2 * SB_DIM, base + 3 * SB_DIM)
    if head_major:
        kh_ref, vh_ref = hm_refs
        _store_transposed(q, SB_HEADS, None, q_ref)
        _store_transposed(k, SB_HEADS, k32_ref, None)
        _store_head_major(k, SB_HEADS, kh_ref)
        _store_transposed(v, SB_HEADS, v32_ref, vh_ref)
    else:
        k32_ref[...] = k
        v32_ref[...] = v
        q_ref[...] = q.astype(BF16)


def _even_in(h2, g, w, batch, head_major):
    m = h2.shape[0]
    t = m // batch
    tm = _row_tile(m)
    nt = max(t // tm, 1)
    outs = [jax.ShapeDtypeStruct((m, CONV_CH), F32)]
    specs = [_row_spec(tm, CONV_CH)]
    if head_major:
        kt = lambda dt: jax.ShapeDtypeStruct((batch, SB_HEADS, HEAD_DIM, t), dt)
        outs += [kt(F32), kt(F32), kt(BF16), jax.ShapeDtypeStruct((batch, SB_HEADS, t, HEAD_DIM), BF16), kt(BF16)]
        specs += [_tr_spec(SB_HEADS, tm, nt)] * 3 + [_hm_spec(SB_HEADS, tm, HEAD_DIM, nt), _tr_spec(SB_HEADS, tm, nt)]
    else:
        outs += [jax.ShapeDtypeStruct((m, SB_DIM), F32)] * 2 + [jax.ShapeDtypeStruct((m, SB_DIM), BF16)]
        specs += [_row_spec(tm, SB_DIM)] * 3
    return pl.pallas_call(
        functools.partial(_even_in_body, head_major),
        grid=(m // tm,),
        in_specs=[_row_spec(tm, D_MODEL), _const_spec((1, D_MODEL)), _const_spec(w.shape)],
        out_specs=specs, out_shape=outs,
        compiler_params=_cp(("parallel",)), name="even_in",
    )(h2, g, w)


CONV_PAD = 32


def _conv_body(t, chunk, hp_ref, w_ref, b_ref, lg_ref, lb_ref, o_ref):
    win_rows = chunk + CONV_PAD

    def one_chunk(c, _):
        base = pl.multiple_of(c * chunk, chunk)
        win = hp_ref[0, pl.ds(base, win_rows), :]
        acc = jnp.zeros((chunk, CONV_CH), F32)
        for r in range(8):
            sh = win if r == 0 else pltpu.roll(win, win_rows - r, 0)
            for a in range(4):
                tap = 8 * a + r
                if tap < CONV_W:
                    acc = acc + sh[8 * a:8 * a + chunk] * w_ref[tap:tap + 1, :]
        y = acc + b_ref[...]
        mu = jnp.mean(y, axis=-1, keepdims=True)
        var = jnp.mean(jnp.square(y - mu), axis=-1, keepdims=True)
        y = (y - mu) * lax.rsqrt(var + NORM_EPS) * lg_ref[...] + lb_ref[...]
        o_ref[0, pl.ds(base, chunk), :] = (y * _sigmoid(y)).astype(o_ref.dtype)
        return 0

    lax.fori_loop(0, t // chunk, one_chunk, 0)


def _conformer_conv(glu, buf, w_dw, b_dw, ln_g, ln_b):
    b, t, c = glu.shape
    hp = jnp.concatenate([buf.astype(F32), glu, jnp.zeros((b, CONV_PAD - (CONV_W - 1), c), F32)], axis=1)
    chunk = min(t, 128)
    return pl.pallas_call(
        functools.partial(_conv_body, t, chunk),
        grid=(b,),
        in_specs=[pl.BlockSpec((1, t + CONV_PAD, c), lambda i: (i, 0, 0)),
                  _const_spec((CONV_PAD, c)), _const_spec((1, c)), _const_spec((1, c)), _const_spec((1, c))],
        out_specs=pl.BlockSpec((1, t, c), lambda i: (i, 0, 0)),
        out_shape=jax.ShapeDtypeStruct((b, t, c), BF16),
        compiler_params=_cp(("parallel",)), name="conformer_conv",
    )(hp, jnp.pad(w_dw, ((0, CONV_PAD - CONV_W), (0, 0))), b_dw[None], ln_g[None], ln_b[None])


def _softplus(z):
    return jnp.maximum(z, 0.0) + jnp.log(1.0 + jnp.exp(-jnp.abs(z)))


def _sb_tiles(zs, carries, valid, tri2, keys_axis):
    valids = valid if isinstance(valid, (list, tuple)) else [valid] * len(zs)
    sps = []
    for z, ok in zip(zs, valids):
        sp = _softplus(z)
        sps.append(sp if ok is None else jnp.where(ok, sp, 0.0))
    laters = []
    for sp in sps:
        hi = sp.astype(BF16)
        lo = (sp - hi.astype(F32)).astype(BF16)
        split = jnp.concatenate([hi, lo], axis=keys_axis)
        laters.append(_dot(split, tri2) if keys_axis == 1 else _dot(tri2, split))
    first = (lambda a: a[:, 0:1]) if keys_axis == 1 else (lambda a: a[0:1, :])
    ws, new = [], []
    for n, (z, sp, later, ok) in enumerate(zip(zs, sps, laters, valids)):
        carry = new[n - len(carries)] if n >= len(carries) else carries[n]
        w = jnp.exp(z - sp - later - carry)
        ws.append((w if ok is None else jnp.where(ok, w, 0.0)).astype(BF16))
        new.append(carry + first(later) + first(sp))
    return ws, new


def _tri2(tk, keys_axis):
    s = jnp.arange(tk)[:, None]
    j = jnp.arange(tk)[None, :]
    if keys_axis == 1:
        tri = (s > j).astype(BF16)
        return jnp.concatenate([tri, tri], axis=0)
    tri = (j > s).astype(BF16)
    return jnp.concatenate([tri, tri], axis=1)


def _sb_prompt_body(q_ref, k_ref, v_ref, tri_ref, o_ref):
    i = pl.program_id(2)
    tq = q_ref.shape[3]
    key = lax.broadcasted_iota(jnp.int32, (tq, tq), 0)
    qry = lax.broadcasted_iota(jnp.int32, (tq, tq), 1)
    diag_valid = key < qry
    tri2 = tri_ref[...]
    nh = q_ref.shape[1]

    def tiles(js, st, valid):
        starts = [pl.multiple_of(j * tq, tq) for j in js]
        zs = [_dot(k_ref[0, hh, pl.ds(s0, tq), :], q_ref[0, hh]) for s0 in starts for hh in range(nh)]
        ws, carries = _sb_tiles(zs, [s[0] for s in st], valid, tri2, 0)
        accs = [s[1] for s in st]
        for n, s0 in enumerate(starts):
            for hh in range(nh):
                accs[hh] = accs[hh] + _dot(v_ref[0, hh, :, pl.ds(s0, tq)], ws[n * nh + hh])
        return tuple(zip(carries[-nh:], accs))

    init = tuple((jnp.zeros((1, tq), F32), jnp.zeros((HEAD_DIM, tq), F32)) for _ in range(nh))
    st = tiles([i], init, diag_valid)
    odd = i % 2
    st = lax.cond(odd == 1, lambda s: tiles([i - 1], s, None), lambda s: s, st)
    top = i - 1 - odd
    st = lax.fori_loop(0, i // 2, lambda n, s: tiles([top - 2 * n, top - 2 * n - 1], s, None), st)
    o_ref[0] = jnp.concatenate([s[1] for s in st], axis=0).T.astype(o_ref.dtype)


SB_HEADS_PER_STEP = 4


def _sb_prompt(q_t, k_hm, v_t):
    b, h, d, t = q_t.shape
    tq = min(ATT_TILE, t)
    nh = SB_HEADS_PER_STEP
    return pl.pallas_call(
        _sb_prompt_body,
        grid=(b, h // nh, t // tq),
        in_specs=[pl.BlockSpec((1, nh, d, tq), lambda bi, hp, i: (bi, hp, 0, i)),
                  pl.BlockSpec((1, nh, t, d), lambda bi, hp, i: (bi, hp, 0, 0)),
                  pl.BlockSpec((1, nh, d, t), lambda bi, hp, i: (bi, hp, 0, 0)),
                  _const_spec((tq, 2 * tq))],
        out_specs=pl.BlockSpec((1, tq, nh * d), lambda bi, hp, i: (bi, i, hp)),
        out_shape=jax.ShapeDtypeStruct((b, t, h * d), BF16),
        compiler_params=_cp(("parallel", "parallel", "arbitrary")), name="sb_prompt",
    )(q_t, k_hm, v_t, _tri2(tq, 0))


def _block_diag_rows(q, groups):
    b, t, h, d = q.shape
    per = h // groups
    qg = jnp.transpose(q.reshape(b, t, groups, per, d), (0, 2, 3, 1, 4))
    eye = jnp.eye(groups, dtype=q.dtype)[None, :, None, None, :, None]
    return (qg[:, :, :, :, None, :] * eye).reshape(b, h * t, groups * d)


def _diag_rows_out(o, t, groups):
    b, rows, gd = o.shape
    d = gd // groups
    per = rows // (groups * t)
    o6 = o.reshape(b, groups, per, t, groups, d)
    idx = jnp.arange(groups)
    og = o6[:, idx, :, :, idx, :]
    return jnp.transpose(og, (1, 3, 0, 2, 4)).reshape(b, t, groups * per * d)


def _sb_sample_body(npages, t_new, pt_ref, qbd_ref, kn_ref, vn_ref, tri_ref, *refs):
    k_refs = refs[:npages]
    v_refs = refs[npages:2 * npages]
    o_ref = refs[2 * npages]
    qbd = qbd_ref[0]
    rows = qbd.shape[0]
    tri2 = tri_ref[...]
    width = kn_ref.shape[2]
    pad = jnp.zeros((PAGE - t_new, width), F32)
    kn = jnp.concatenate([kn_ref[0], pad], axis=0).astype(BF16)
    vn = jnp.concatenate([vn_ref[0], pad], axis=0).astype(BF16)
    qoff = lax.broadcasted_iota(jnp.int32, (rows, PAGE), 0) % t_new
    col = lax.broadcasted_iota(jnp.int32, (rows, PAGE), 1)
    order = list(reversed(range(npages)))
    zs = [_nt(qbd, kn)] + [_dot(qbd, k_refs[p][0].reshape(width, PAGE).astype(BF16)) for p in order]
    ws, _ = _sb_tiles(zs, [jnp.zeros((rows, 1), F32)], [col < qoff] + [None] * npages, tri2, 1)
    acc = _dot(ws[0], vn)
    for w, p in zip(ws[1:], order):
        acc = acc + _nt(w, v_refs[p][0].reshape(width, PAGE).astype(BF16))
    o_ref[0] = acc


def _page_specs(npages, heads):
    return [pl.BlockSpec((1, heads, HEAD_DIM, PAGE), lambda b, pt, p=p: (pt[b, p], 0, 0, 0))
            for p in range(npages)]


def _keys_minor(cache):
    nd = cache.ndim
    return jnp.transpose(cache, tuple(range(nd - 3)) + (nd - 2, nd - 1, nd - 3))


def _sb_sample(q, k_new, v_new, cache_k, cache_v, page_table):
    b, t, w = k_new.shape
    npages = page_table.shape[1]
    qbd = _block_diag_rows(q.reshape(b, t, SB_HEADS, HEAD_DIM), SB_HEADS)
    rows = SB_HEADS * t
    bspec = lambda r, c: pl.BlockSpec((1, r, c), lambda bi, pt: (bi, 0, 0))
    grid_spec = pltpu.PrefetchScalarGridSpec(
        num_scalar_prefetch=1, grid=(b,),
        in_specs=[bspec(rows, w), bspec(t, w), bspec(t, w),
                  pl.BlockSpec((2 * PAGE, PAGE), lambda bi, pt: (0, 0))]
                 + _page_specs(npages, SB_HEADS) + _page_specs(npages, SB_HEADS),
        out_specs=bspec(rows, w))
    o = pl.pallas_call(
        functools.partial(_sb_sample_body, npages, t),
        grid_spec=grid_spec,
        out_shape=jax.ShapeDtypeStruct((b, rows, w), F32),
        compiler_params=_cp(("parallel",)), name="sb_sample",
    )(page_table, qbd, k_new, v_new, _tri2(PAGE, 1), *([cache_k] * npages), *([cache_v] * npages))
    return _diag_rows_out(o, t, SB_HEADS)


def _proj_res_body(n_in, *refs):
    h_ref = refs[0]
    x_refs = refs[1:1 + n_in]
    w_ref = refs[1 + n_in]
    o_ref = refs[2 + n_in]
    acc = h_ref[...]
    off = 0
    for x_ref in x_refs:
        k = x_ref.shape[1]
        acc = acc + _dot(x_ref[...].astype(BF16), w_ref[off:off + k, :])
        off += k
    o_ref[...] = acc


def _proj_res(h2, xs, w):
    m = h2.shape[0]
    tm = _row_tile(m)
    return pl.pallas_call(
        functools.partial(_proj_res_body, len(xs)),
        grid=(m // tm,),
        in_specs=[_row_spec(tm, D_MODEL)] + [_row_spec(tm, x.shape[1]) for x in xs] + [_const_spec(w.shape)],
        out_specs=_row_spec(tm, D_MODEL),
        out_shape=jax.ShapeDtypeStruct((m, D_MODEL), F32),
        compiler_params=_cp(("parallel",)), name="proj_res",
    )(h2, *xs, w)


def _norm_proj_body(scale, h_ref, g_ref, w_ref, o_ref):
    xn = _rms(h_ref[...], g_ref[...]).astype(BF16)
    o_ref[...] = (_dot(xn, w_ref[...]) * scale).astype(o_ref.dtype)


def _norm_proj(h2, g, w, scale):
    m = h2.shape[0]
    tm = _row_tile(m)
    return pl.pallas_call(
        functools.partial(_norm_proj_body, scale),
        grid=(m // tm,),
        in_specs=[_row_spec(tm, D_MODEL), _const_spec((1, D_MODEL)), _const_spec(w.shape)],
        out_specs=_row_spec(tm, w.shape[1]),
        out_shape=jax.ShapeDtypeStruct((m, w.shape[1]), BF16),
        compiler_params=_cp(("parallel",)), name="norm_proj",
    )(h2, g, w)


def _mem_kv_body(x_ref, wk0, wv0, wk1, wv1, *o_refs):
    x = x_ref[...].astype(BF16)
    for n, w_ref in enumerate((wk0, wv0, wk1, wv1)):
        y = _dot(x, w_ref[...])
        o_refs[2 * n][...] = y
        o_refs[2 * n + 1][...] = y.astype(BF16)


def _mem_kv(x2, ws):
    m = x2.shape[0]
    tm = _row_tile(m)
    outs, specs = [], []
    for _ in ws:
        outs += [jax.ShapeDtypeStruct((m, D_MODEL), F32), jax.ShapeDtypeStruct((m, D_MODEL), BF16)]
        specs += [_row_spec(tm, D_MODEL)] * 2
    return pl.pallas_call(
        _mem_kv_body, grid=(m // tm,),
        in_specs=[_row_spec(tm, D_MODEL)] + [_const_spec(w.shape) for w in ws],
        out_specs=specs, out_shape=outs,
        compiler_params=_cp(("parallel",)), name="mem_kv",
    )(x2, *ws)


def _mem_block_body(h_ref, g_ref, wq_ref, k_ref, v_ref, wo_ref, o_ref):
    h = h_ref[...]
    q = (_dot(_rms(h, g_ref[...]).astype(BF16), wq_ref[...]) * (MEM_HD ** -0.5)).astype(BF16)
    outs = []
    for hh in range(MEM_HEADS):
        sl = slice(hh * MEM_HD, (hh + 1) * MEM_HD)
        s = _nt(q[:, sl], k_ref[0, :, sl])
        m = jnp.max(s, axis=-1, keepdims=True)
        e = jnp.exp(s - m)
        p = e / jnp.sum(e, axis=-1, keepdims=True)
        outs.append(_dot(p.astype(BF16), v_ref[0, :, sl]).astype(BF16))
    o_ref[...] = h + _dot(jnp.concatenate(outs, axis=1), wo_ref[...])


def _mem_block(h2, g, wq, mk, mv, wo, batch):
    m = h2.shape[0]
    tm = _row_tile(m)
    nt = (m // batch) // tm
    ml = mk.shape[1]
    kv_spec = pl.BlockSpec((1, ml, D_MODEL), lambda i: (i // nt, 0, 0))
    return pl.pallas_call(
        _mem_block_body, grid=(m // tm,),
        in_specs=[_row_spec(tm, D_MODEL), _const_spec((1, D_MODEL)), _const_spec(wq.shape), kv_spec, kv_spec,
                  _const_spec(wo.shape)],
        out_specs=_row_spec(tm, D_MODEL),
        out_shape=jax.ShapeDtypeStruct((m, D_MODEL), F32),
        compiler_params=_cp(("parallel",)), name="mem_block",
    )(h2, g, wq, mk, mv, wo)


def _mem_attn_cached_body(q_ref, k_ref, v_ref, o_ref):
    ml, heads, hd = k_ref.shape[2:]
    t = q_ref.shape[1] // heads
    shape = (ml * heads, heads * t)
    same = (lax.broadcasted_iota(jnp.int32, shape, 0) % heads) == (lax.broadcasted_iota(jnp.int32, shape, 1) // t)
    reqs = range(q_ref.shape[0])
    rows = lambda ref, n: ref[0, n].reshape(ml * heads, hd).astype(BF16)
    sms = [jnp.where(same, _nt(rows(k_ref, n), q_ref[n]), NEG_INF) for n in reqs]
    ps = []
    for sm in sms:
        m = jnp.max(sm, axis=0, keepdims=True)
        e = jnp.where(same, jnp.exp(sm - m), 0.0)
        ps.append((e / jnp.sum(e, axis=0, keepdims=True)).astype(BF16))
    for n in reqs:
        o_ref[n] = lax.dot_general(ps[n], rows(v_ref, n), (((0,), (0,)), ((), ())),
                                   preferred_element_type=F32).astype(o_ref.dtype)


MEM_REQS_PER_STEP = 2


def _mem_attn_cached(q, cache_k, cache_v, layer):
    b, t, d = q.shape
    _, _, ml, heads, hd = cache_k.shape
    rows = heads * t
    q_rows = jnp.transpose(q.reshape(b, t, heads, hd), (0, 2, 1, 3)).reshape(b, rows, hd)
    per = MEM_REQS_PER_STEP if b % MEM_REQS_PER_STEP == 0 else 1
    kv_spec = pl.BlockSpec((1, per, ml, heads, hd), lambda bi: (layer, bi, 0, 0, 0))
    o = pl.pallas_call(
        _mem_attn_cached_body, grid=(b // per,),
        in_specs=[pl.BlockSpec((per, rows, hd), lambda bi: (bi, 0, 0)), kv_spec, kv_spec],
        out_specs=pl.BlockSpec((per, rows, hd), lambda bi: (bi, 0, 0)),
        out_shape=jax.ShapeDtypeStruct((b, rows, hd), BF16),
        compiler_params=_cp(("parallel",)), name="mem_attn_cached",
    )(q_rows, cache_k, cache_v)
    return jnp.transpose(o.reshape(b, heads, t, hd), (0, 2, 1, 3)).reshape(b, t, d)


FFN_CHUNK = 512


def _ffn_body(final, h_ref, g_ref, wg_ref, wu_ref, wd_ref, *rest):
    if final:
        gf_ref, o_ref = rest
    else:
        (o_ref,) = rest
    h = h_ref[...]
    xn = _rms(h, g_ref[...]).astype(BF16)
    acc = h
    for lo in range(0, D_FF, FFN_CHUNK):
        hi = min(lo + FFN_CHUNK, D_FF)
        gate = _dot(xn, wg_ref[:, lo:hi])
        up = _dot(xn, wu_ref[:, lo:hi])
        act = (gate * _sigmoid(gate) * up).astype(BF16)
        acc = acc + _dot(act, wd_ref[lo:hi, :])
    if final:
        acc = _rms(acc, gf_ref[...])
    o_ref[...] = acc


def _ffn(h2, g, wg, wu, wd, final_g=None):
    m = h2.shape[0]
    tm = _row_tile(m)
    final = final_g is not None
    once = lambda shape: pl.BlockSpec(shape, lambda i: (0, 0), pipeline_mode=pl.Buffered(1))
    in_specs = [_row_spec(tm, D_MODEL), _const_spec((1, D_MODEL)), once(wg.shape), once(wu.shape), once(wd.shape)]
    args = [h2, g, wg, wu, wd]
    if final:
        in_specs.append(_const_spec((1, D_MODEL)))
        args.append(final_g)
    return pl.pallas_call(
        functools.partial(_ffn_body, final), grid=(m // tm,),
        in_specs=in_specs, out_specs=_row_spec(tm, D_MODEL),
        out_shape=jax.ShapeDtypeStruct((m, D_MODEL), F32),
        compiler_params=_cp(("parallel",)), name="ffn",
    )(*args)


ODD_Q = NSA_HEADS * HEAD_DIM
ODD_GATES = 3 * NSA_HEADS
ODD_IN = ODD_Q + 6 * NSA_KV_DIM + ODD_GATES
GATE_PAD = 16
GATE_LANES = NSA_KVH * GATE_PAD
ODD_IN_PAD = ODD_Q + 6 * NSA_KV_DIM + 128


def _odd_in_weights(w_in_odd):
    gates = w_in_odd[:, ODD_Q + 6 * NSA_KV_DIM:].reshape(D_MODEL, NSA_KVH, 3 * NSA_GQA)
    gates = jnp.pad(gates, ((0, 0), (0, 0), (0, GATE_PAD - 3 * NSA_GQA))).reshape(D_MODEL, GATE_LANES)
    return jnp.concatenate([w_in_odd[:, :ODD_Q + 6 * NSA_KV_DIM],
                            jnp.pad(gates, ((0, 0), (0, 128 - GATE_LANES)))], axis=1).astype(BF16)


def _odd_in_body(head_major, h_ref, g_ref, w_ref, c_ref, s_ref, *o_refs):
    xn = _rms(h_ref[...], g_ref[...]).astype(BF16)
    c = c_ref[...]
    s = s_ref[...]

    def mm(lo, hi):
        return _dot(xn, w_ref[:, lo:hi])

    kv = lambda n: mm(ODD_Q + n * NSA_KV_DIM, ODD_Q + (n + 1) * NSA_KV_DIM)
    q = _rope_wide(mm(0, ODD_Q), c, s) * (HEAD_DIM ** -0.5)
    ck, cv = kv(0), kv(1)
    sk, sv = _rope_wide(kv(2), c, s), kv(3)
    wk, wv = _rope_wide(kv(4), c, s), kv(5)
    gates = _sigmoid(mm(ODD_Q + 6 * NSA_KV_DIM, ODD_IN_PAD))[:, :GATE_LANES]
    if head_major:
        ck_rows, cv_rows = o_refs[:2]
        t32 = o_refs[2:8]
        q_ref, g_out, sk_rows, sv_t, wk_rows, wv_t = o_refs[8:14]
        ck_rows[...] = ck
        cv_rows[...] = cv
        for val, r32, r16 in zip((ck, cv, sk, sv, wk, wv), t32, (None, None, None, sv_t, None, wv_t)):
            _store_transposed(val, NSA_KVH, r32, r16)
        _store_transposed(q, NSA_HEADS, None, q_ref)
        _store_head_major(sk, NSA_KVH, sk_rows)
        _store_head_major(wk, NSA_KVH, wk_rows)
        g_out[0] = gates.T.reshape(NSA_KVH, GATE_PAD, gates.shape[0])
    else:
        for ref, val in zip(o_refs[:6], (ck, cv, sk, sv, wk, wv)):
            ref[...] = val
        q_ref, g_out = o_refs[6:]
        q_ref[...] = q.astype(BF16)
        g_out[...] = gates


def _odd_in(h2, g, w, cos_t, sin_t, batch, head_major):
    m = h2.shape[0]
    t = m // batch
    tm = _row_tile(m)
    nt = max(t // tm, 1)
    ntab = cos_t.shape[0] // tm
    tab_spec = pl.BlockSpec((tm, 128), lambda i: (i % ntab, 0))
    kv = jax.ShapeDtypeStruct((m, NSA_KV_DIM), F32)
    if head_major:
        kt = lambda dt: jax.ShapeDtypeStruct((batch, NSA_KVH, HEAD_DIM, t), dt)
        rows16 = jax.ShapeDtypeStruct((batch, NSA_KVH, t, HEAD_DIM), BF16)
        outs = [kv] * 2 + [kt(F32)] * 6
        outs += [jax.ShapeDtypeStruct((batch, NSA_HEADS, HEAD_DIM, t), BF16),
                 jax.ShapeDtypeStruct((batch, NSA_KVH, GATE_PAD, t), F32), rows16, kt(BF16), rows16, kt(BF16)]
        rows_spec = _hm_spec(NSA_KVH, tm, HEAD_DIM, nt)
        specs = [_row_spec(tm, NSA_KV_DIM)] * 2 + [_tr_spec(NSA_KVH, tm, nt)] * 6
        specs += [_tr_spec(NSA_HEADS, tm, nt),
                  pl.BlockSpec((1, NSA_KVH, GATE_PAD, tm), lambda i: (i // nt, 0, 0, i % nt)),
                  rows_spec, _tr_spec(NSA_KVH, tm, nt), rows_spec, _tr_spec(NSA_KVH, tm, nt)]
    else:
        outs = [kv] * 6 + [jax.ShapeDtypeStruct((m, ODD_Q), BF16), jax.ShapeDtypeStruct((m, GATE_LANES), F32)]
        specs = [_row_spec(tm, NSA_KV_DIM)] * 6 + [_row_spec(tm, ODD_Q), _row_spec(tm, GATE_LANES)]
    return pl.pallas_call(
        functools.partial(_odd_in_body, head_major), grid=(m // tm,),
        in_specs=[_row_spec(tm, D_MODEL), _const_spec((1, D_MODEL)), _const_spec(w.shape), tab_spec, tab_spec],
        out_specs=specs, out_shape=outs,
        compiler_params=_cp(("parallel",)), name="odd_in",
    )(h2, g, w, cos_t, sin_t)


def _gelu_tanh(x):
    return 0.5 * x * (1.0 + jnp.tanh(math.sqrt(2.0 / math.pi) * (x + 0.044715 * x * x * x)))


def _compress_body(add_pes, use_rope, *refs):
    n = len(add_pes)
    x_refs = refs[:n]
    pe_ref, w1_ref, w2_ref, c_ref, s_ref, o_ref, acc_ref, wbd_ref = refs[n:]
    l = pl.program_id(0)

    @pl.when(l == 0)
    def _():
        acc_ref[...] = jnp.zeros_like(acc_ref)
        wbd_ref[...] = jnp.zeros_like(wbd_ref)

    for gg in range(NSA_KVH):
        wbd_ref[gg * HEAD_DIM:(gg + 1) * HEAD_DIM, gg * CMP_HID:(gg + 1) * CMP_HID] = w1_ref[0]
    off = 0
    for x_ref, add_pe in zip(x_refs, add_pes):
        x = x_ref[...]
        if add_pe:
            x = (x + pe_ref[0]).astype(BF16)
        acc_ref[off:off + x.shape[0], :] += _dot(x, wbd_ref[...])
        off += x.shape[0]

    @pl.when(l == CMP_BLK - 1)
    def _():
        hid = _gelu_tanh(acc_ref[...]).astype(BF16)
        y = _dot(hid, w2_ref[...])
        if use_rope:
            y = _rope_wide(y, c_ref[...], s_ref[...])
        o_ref[...] = y


def _compress(xs, pe, w1, w2, tables, add_pes):
    rows = sum(x.shape[0] for x in xs)
    pe_t = jnp.tile(pe, (1, NSA_KVH))[:, None, :]
    w2bd = jnp.kron(jnp.eye(NSA_KVH, dtype=F32), w2).astype(BF16)
    use_rope = tables is not None
    if use_rope:
        c_t, s_t = tables
    else:
        c_t = s_t = jnp.zeros((rows, 128), F32)
    return pl.pallas_call(
        functools.partial(_compress_body, tuple(add_pes), use_rope), grid=(CMP_BLK,),
        in_specs=[pl.BlockSpec((x.shape[0], NSA_KV_DIM), lambda l: (0, l)) for x in xs]
                 + [pl.BlockSpec((1, 1, NSA_KV_DIM), lambda l: (l, 0, 0)),
                  pl.BlockSpec((1, HEAD_DIM, CMP_HID), lambda l: (l, 0, 0)),
                  _const_spec(w2bd.shape), _const_spec((rows, 128)), _const_spec((rows, 128))],
        out_specs=_const_spec((rows, NSA_KV_DIM)),
        out_shape=jax.ShapeDtypeStruct((rows, NSA_KV_DIM), F32),
        scratch_shapes=[pltpu.VMEM((rows, NSA_KVH * CMP_HID), F32),
                        pltpu.VMEM((NSA_KV_DIM, NSA_KVH * CMP_HID), BF16)],
        compiler_params=_cp(("arbitrary",)), name="compress",
    )(*xs, pe_t, w1.astype(BF16), w2bd, c_t, s_t)


NB_PAD = 128
RANK_ROWS = 64


def _cmp_and_select(q, kc, vc, qpos, groups, rows_per_q, nb):
    tq = qpos.shape[0]
    rows = groups * rows_per_q * tq
    s = _nt(q, kc).reshape(groups, rows_per_q, tq, NB_PAD)
    n_idx = lax.broadcasted_iota(jnp.int32, (tq, NB_PAD), 1)
    valid = ((n_idx * CMP_BLK + (CMP_BLK - 1)) <= qpos)[None, None]
    sm = jnp.where(valid, s, NEG_INF)
    m = jnp.max(sm, axis=-1, keepdims=True)
    p = jnp.where(valid, jnp.exp(sm - m), 0.0)
    p = p / jnp.maximum(jnp.sum(p, axis=-1, keepdims=True), 1e-30)
    o_cmp = _dot(p.reshape(rows, NB_PAD).astype(BF16), vc)
    cur = qpos // CMP_BLK
    forced = ((n_idx == 0) | (n_idx == cur) | (n_idx == cur - 1))[None]
    score = jnp.where((n_idx > cur)[None], -1.0, jnp.where(forced, FORCE_SCORE, jnp.sum(p, axis=1)))
    nq = groups * tq
    nqp = -(-nq // 128) * 128
    score = score.reshape(nq, NB_PAD)
    if nqp > nq:
        score = jnp.concatenate([score, jnp.zeros((nqp - nq, NB_PAD), F32)], axis=0)
    sel_t = _top_n_mask(score.T[:RANK_ROWS], nb)
    sel = jnp.concatenate([sel_t, jnp.zeros((NB_PAD - RANK_ROWS, nqp), F32)], axis=0).T[:nq]
    sel = jnp.broadcast_to(sel.reshape(groups, 1, tq, NB_PAD), (groups, rows_per_q, tq, NB_PAD))
    return o_cmp, sel.reshape(rows, NB_PAD)


def _top_n_mask(st, nb):
    groups = RANK_ROWS // 8
    blocks = [st[8 * g:8 * g + 8] for g in range(groups)]
    cnt = [jnp.zeros_like(b) for b in blocks]
    row = lax.broadcasted_iota(jnp.int32, blocks[0].shape, 0)
    for i in range(nb):
        r = st[i:i + 1]
        for g in range(groups):
            gt = jnp.where(r > blocks[g], 1.0, 0.0)
            if 8 * g + 7 <= i:
                inc = gt
            else:
                ge = jnp.where(r >= blocks[g], 1.0, 0.0)
                inc = ge if 8 * g > i else jnp.where(row + 8 * g > i, ge, gt)
            cnt[g] = cnt[g] + inc
    return jnp.concatenate([jnp.where(c < float(SEL_TOPN), 1.0, 0.0) for c in cnt], axis=0)


def _osm(z, st, v, keys_axis, v_keys_minor):
    m, l, acc = st
    m_new = jnp.maximum(m, jnp.max(z, axis=keys_axis, keepdims=True))
    alpha = jnp.exp(m - m_new)
    p = jnp.exp(z - m_new)
    l = alpha * l + jnp.sum(p, axis=keys_axis, keepdims=True)
    pb = p.astype(BF16)
    if keys_axis == 1:
        pv = _nt(pb, v) if v_keys_minor else _dot(pb, v)
    else:
        pv = _dot(v, pb)
    return m_new, l, alpha * acc + pv


def _softmax_tiles(zs, vs):
    m = functools.reduce(jnp.maximum, [jnp.max(z, axis=1, keepdims=True) for z in zs])
    ps = [jnp.exp(z - m) for z in zs]
    l = functools.reduce(lambda a, b: a + b, [jnp.sum(p, axis=1, keepdims=True) for p in ps])
    acc = None
    for p, (v, keys_minor) in zip(ps, vs):
        pv = _nt(p.astype(BF16), v) if keys_minor else _dot(p.astype(BF16), v)
        acc = pv if acc is None else acc + pv
    return acc / jnp.maximum(l, 1e-30)


def _osm_groups_t(q_ts, k_rows, v_t, bias, sts):
    n = len(q_ts)
    z_next = _dot(k_rows, q_ts[0]) + bias
    out = []
    for r in range(n):
        z = z_next
        if r + 1 < n:
            z_next = _dot(k_rows, q_ts[r + 1]) + bias
        out.append(_osm(z, sts[r], v_t, 0, True))
    return tuple(out)


def _osm_init(rows, width, keys_axis=1):
    if keys_axis == 1:
        return (jnp.full((rows, 1), NEG_INF, F32), jnp.zeros((rows, 1), F32), jnp.zeros((rows, width), F32))
    return (jnp.full((1, rows), NEG_INF, F32), jnp.zeros((1, rows), F32), jnp.zeros((width, rows), F32))


def _osm_out(st):
    return st[2] / jnp.maximum(st[1], 1e-30)


def _nsa_prompt_body(nb, q_ref, kc_ref, vc_ref, sk_ref, sv_ref, wk_ref, wv_ref, g_ref, o_ref, sb_ref):
    i = pl.program_id(2)
    tq = q_ref.shape[3]
    tk = tq
    rq = NSA_GQA
    q_ts = [q_ref[0, r] for r in range(rq)]
    qpos = i * tq + lax.broadcasted_iota(jnp.int32, (1, tq), 1)

    n_idx = lax.broadcasted_iota(jnp.int32, (NB_PAD, tq), 0)
    valid = (n_idx * CMP_BLK + (CMP_BLK - 1)) <= qpos
    kc = kc_ref[0, 0]
    vc_t = vc_ref[0, 0]
    o_cmp, score = [], None
    for r in range(rq):
        sm = jnp.where(valid, _dot(kc, q_ts[r]), NEG_INF)
        m = jnp.max(sm, axis=0, keepdims=True)
        p = jnp.where(valid, jnp.exp(sm - m), 0.0)
        p = p / jnp.maximum(jnp.sum(p, axis=0, keepdims=True), 1e-30)
        o_cmp.append(_dot(vc_t, p.astype(BF16)))
        score = p if score is None else score + p
    cur = qpos // CMP_BLK
    forced = (n_idx == 0) | (n_idx == cur) | (n_idx == cur - 1)
    score = jnp.where(n_idx > cur, -1.0, jnp.where(forced, FORCE_SCORE, score))
    sel_t = _top_n_mask(score[:RANK_ROWS], nb)
    sb_ref[...] = (sel_t - 1.0) * 1e30

    key = lax.broadcasted_iota(jnp.int32, (tk, tq), 0)
    qry = lax.broadcasted_iota(jnp.int32, (tk, tq), 1)
    init = tuple(_osm_init(tq, HEAD_DIM, 0) for _ in range(rq))

    def sel_tile(j, sts, diag):
        start = pl.multiple_of(j * tk, tk)
        per = tk // CMP_BLK
        bias = jnp.concatenate([jnp.broadcast_to(sb_ref[pl.ds(j * per + b, 1), :], (CMP_BLK, tq))
                                for b in range(per)], axis=0)
        if diag:
            bias = jnp.where(key <= qry, bias, NEG_INF)
        return _osm_groups_t(q_ts, sk_ref[0, 0, pl.ds(start, tk), :], sv_ref[0, 0, :, pl.ds(start, tk)], bias, sts)

    sts = lax.fori_loop(0, i, lambda j, s: sel_tile(j, s, False), init)
    o_sel = [_osm_out(st) for st in sel_tile(i, sts, True)]

    def win_tile(back, sts):
        j_raw = i - back
        start = pl.multiple_of(jnp.maximum(j_raw, 0) * tk, tk)
        if back == 0:
            ok = key <= qry
        elif back < WINDOW // tk:
            ok = key >= 0
        else:
            ok = key > qry
        bias = jnp.where(ok & (j_raw >= 0), 0.0, NEG_INF)
        return _osm_groups_t(q_ts, wk_ref[0, 0, pl.ds(start, tk), :], wv_ref[0, 0, :, pl.ds(start, tk)], bias, sts)

    sts = init
    for back in range(WINDOW // tk + 1):
        sts = win_tile(back, sts)
    o_win = [_osm_out(st) for st in sts]

    gates = g_ref[0, 0]
    outs = []
    for r in range(rq):
        gc, gs, gw = (gates[3 * r + n:3 * r + n + 1, :] for n in range(3))
        outs.append(gc * o_cmp[r] + gs * o_sel[r] + gw * o_win[r])
    o_ref[0] = jnp.concatenate(outs, axis=0).T.astype(o_ref.dtype)


def _nsa_prompt(q_t, kc_hm, vc_t, sk_hm, sv_t, wk_hm, wv_t, gates_t, nb):
    b, _, d, t = q_t.shape
    tq = min(NSA_TILE, t)
    assert WINDOW % tq == 0 and nb <= RANK_ROWS and tq % 128 == 0 and tq % CMP_BLK == 0
    per_group = lambda shape: pl.BlockSpec((1, 1) + shape, lambda bi, g, i: (bi, g, 0, 0))
    return pl.pallas_call(
        functools.partial(_nsa_prompt_body, nb),
        grid=(b, NSA_KVH, t // tq),
        in_specs=[pl.BlockSpec((1, NSA_GQA, d, tq), lambda bi, g, i: (bi, g, 0, i)),
                  per_group((NB_PAD, d)), per_group((d, NB_PAD)),
                  per_group((t, d)), per_group((d, t)), per_group((t, d)), per_group((d, t)),
                  pl.BlockSpec((1, 1, GATE_PAD, tq), lambda bi, g, i: (bi, g, 0, i))],
        out_specs=pl.BlockSpec((1, tq, NSA_GQA * d), lambda bi, g, i: (bi, i, g)),
        out_shape=jax.ShapeDtypeStruct((b, t, NSA_HEADS * d), BF16),
        scratch_shapes=[pltpu.VMEM((RANK_ROWS, tq), F32)],
        compiler_params=_cp(("parallel", "parallel", "arbitrary")), name="nsa_prompt",
    )(q_t, kc_hm, vc_t, sk_hm, sv_t, wk_hm, wv_t, gates_t)


def _cmp_gather_body(npages, pt_ref, pe_ref, *refs):
    page_refs = refs[:npages]
    o_ref = refs[npages]
    tok_ref = refs[npages + 1]
    pe2 = pe_ref[...]
    lane_tiles = NSA_KV_DIM // 128
    for p in range(npages):
        tok = page_refs[p][0].reshape(NSA_KV_DIM, PAGE).T + pe2
        for c in range(lane_tiles):
            tok_ref[c, p * PAGE:(p + 1) * PAGE, :] = tok[:, c * 128:(c + 1) * 128]
    nblk = npages * PAGE // CMP_BLK
    mid_ref = refs[npages + 2]
    for lo in range(8):
        for c in range(lane_tiles):
            mid_ref[c, lo] = tok_ref[c, pl.ds(lo, npages * PAGE // 8, stride=8), :]

    for l in range(CMP_BLK):
        for c in range(lane_tiles):
            lanes = l * NSA_KV_DIM + c * 128
            o_ref[0, :, lanes:lanes + 128] = mid_ref[c, l % 8, pl.ds(l // 8, nblk, stride=8), :].astype(BF16)


def _cmp_gather(cache, pe, page_table):
    b, npages = page_table.shape
    nblk = npages * PAGE // CMP_BLK
    width = CMP_BLK * NSA_KV_DIM
    pe2 = jnp.tile(pe, (PAGE // CMP_BLK, NSA_KVH))
    grid_spec = pltpu.PrefetchScalarGridSpec(
        num_scalar_prefetch=1, grid=(b,),
        in_specs=[pl.BlockSpec((PAGE, NSA_KV_DIM), lambda bi, pt: (0, 0))] + _page_specs(npages, NSA_KVH),
        out_specs=pl.BlockSpec((1, nblk, width), lambda bi, pt: (bi, 0, 0)),
        scratch_shapes=[pltpu.VMEM((NSA_KV_DIM // 128, npages * PAGE, 128), F32),
                        pltpu.VMEM((NSA_KV_DIM // 128, 8, npages * PAGE // 8, 128), F32)])
    return pl.pallas_call(
        functools.partial(_cmp_gather_body, npages), grid_spec=grid_spec,
        out_shape=jax.ShapeDtypeStruct((b, nblk, width), BF16),
        compiler_params=_cp(("parallel",)), name="cmp_gather",
    )(page_table, pe2, *([cache] * npages))


def _nsa_sample_body(npages, t_new, past, nb, pt_ref, qbd_ref, kc_ref, vc_ref, g_ref,
                     skn_ref, svn_ref, wkn_ref, wvn_ref, wkc_ref, wvc_ref, *refs):
    sk_refs = refs[:npages]
    sv_refs = refs[npages:2 * npages]
    o_ref = refs[2 * npages]
    qbd = qbd_ref[0]
    rows = qbd.shape[0]
    w = qbd.shape[1]
    rq = rows // t_new
    t_idx = lax.broadcasted_iota(jnp.int32, (t_new, 1), 0)
    qpos = past + t_idx

    o_cmp, sel = _cmp_and_select(qbd, kc_ref[0], vc_ref[0], qpos, NSA_KVH, NSA_GQA, nb)
    sel_bias = (sel - 1.0) * 1e30

    qoff = lax.broadcasted_iota(jnp.int32, (rows, PAGE), 0) % t_new
    col = lax.broadcasted_iota(jnp.int32, (rows, PAGE), 1)
    pad = jnp.zeros((PAGE - t_new, w), F32)
    new_bias = jnp.where(col <= qoff, 0.0, NEG_INF)

    def padded(ref):
        return jnp.concatenate([ref[0], pad], axis=0).astype(BF16)

    half = col < CMP_BLK
    zs, vs = [], []
    for p in range(npages):
        b0 = sel_bias[:, 2 * p:2 * p + 1]
        b1 = sel_bias[:, 2 * p + 1:2 * p + 2]
        zs.append(_dot(qbd, sk_refs[p][0].reshape(w, PAGE).astype(BF16)) + jnp.where(half, b0, b1))
        vs.append((sv_refs[p][0].reshape(w, PAGE).astype(BF16), True))
    last = (past // CMP_BLK)
    zs.append(_nt(qbd, padded(skn_ref)) + new_bias + sel_bias[:, last:last + 1])
    vs.append((padded(svn_ref), False))
    o_sel = _softmax_tiles(zs, vs)

    zs, vs = [_nt(qbd, padded(wkn_ref)) + new_bias], [(padded(wvn_ref), False)]
    wb = wkc_ref.shape[3]
    for c in range(wb // PAGE):
        kpos = (past - wb) + c * PAGE + col
        diff = (past + qoff) - kpos
        bias = jnp.where((diff < WINDOW) & (kpos >= 0), 0.0, NEG_INF)
        chunk = lambda ref: ref[0, :, :, c * PAGE:(c + 1) * PAGE].reshape(w, PAGE).astype(BF16)
        zs.append(_dot(qbd, chunk(wkc_ref)) + bias)
        vs.append((chunk(wvc_ref), True))
    o_win = _softmax_tiles(zs, vs)

    gates = g_ref[0]
    o_ref[0] = gates[:, 0:1] * o_cmp + gates[:, 1:2] * o_sel + gates[:, 2:3] * o_win


def _nsa_sample(q, kc, vc, gates, sk_new, sv_new, wk_new, wv_new, win_k, win_v,
                cache_sk, cache_sv, page_table, past, nb):
    b, t, w = sk_new.shape
    npages = page_table.shape[1]
    assert nb <= RANK_ROWS and (past // CMP_BLK) < nb
    qbd = _block_diag_rows(q.reshape(b, t, NSA_HEADS, HEAD_DIM), NSA_KVH)
    rows = NSA_HEADS * t
    g_rows = jnp.transpose(gates.reshape(b, t, NSA_KVH, NSA_GQA, 3), (0, 2, 3, 1, 4)).reshape(b, rows, 3)
    bspec = lambda r, c: pl.BlockSpec((1, r, c), lambda bi, pt: (bi, 0, 0))
    wb = win_k.shape[3]
    win_spec = pl.BlockSpec((1, NSA_KVH, HEAD_DIM, wb), lambda bi, pt: (bi, 0, 0, 0))
    grid_spec = pltpu.PrefetchScalarGridSpec(
        num_scalar_prefetch=1, grid=(b,),
        in_specs=[bspec(rows, w), bspec(NB_PAD, w), bspec(NB_PAD, w), bspec(rows, 3),
                  bspec(t, w), bspec(t, w), bspec(t, w), bspec(t, w), win_spec, win_spec]
                 + _page_specs(npages, NSA_KVH) + _page_specs(npages, NSA_KVH),
        out_specs=bspec(rows, w))
    o = pl.pallas_call(
        functools.partial(_nsa_sample_body, npages, t, past, nb), grid_spec=grid_spec,
        out_shape=jax.ShapeDtypeStruct((b, rows, w), F32),
        compiler_params=_cp(("parallel",)), name="nsa_sample",
    )(page_table, qbd, kc, vc, g_rows, sk_new, sv_new, wk_new, wv_new, win_k, win_v,
      *([cache_sk] * npages), *([cache_sv] * npages))
    return _diag_rows_out(o, t, NSA_KVH)


def _common_tail(h2, batch, l, mem_k, mem_v, p, final_g):
    t = h2.shape[0] // batch
    if mem_k.ndim == 5:
        q = _norm_proj(h2, p["norm_mem"][l][None], p["w_mem_q"][l], MEM_HD ** -0.5).reshape(batch, t, D_MODEL)
        o = _mem_attn_cached(q, mem_k, mem_v, l)
        h2 = _proj_res(h2, [o.reshape(batch * t, D_MODEL)], p["w_mem_o"][l])
    else:
        h2 = _mem_block(h2, p["norm_mem"][l][None], p["w_mem_q"][l], mem_k, mem_v, p["w_mem_o"][l], batch)
    return _ffn(h2, p["norm_ffn"][l][None], p["w_ffn_gate"][l], p["w_ffn_up"][l], p["w_ffn_down"][l], final_g)


def _even_layer(h2, batch, p, conv_buf, sample_ctx):
    t = h2.shape[0] // batch
    prompt = sample_ctx is None
    res = _even_in(h2, p["norm_mix"][0][None], p["w_in_even"][0], batch, prompt)
    glu3 = res[0].reshape(batch, t, CONV_CH)
    y_conv = _conformer_conv(glu3, conv_buf, p["conv_w"][0], p["conv_b"][0], p["conv_ln_g"][0], p["conv_ln_b"][0])
    new_buf = jnp.concatenate([conv_buf.astype(F32), glu3], axis=1)[:, -(CONV_W - 1):]
    shp = (batch, t, SB_HEADS, HEAD_DIM)
    if prompt:
        o = _sb_prompt(res[3], res[4], res[5])
        k_out, v_out = _from_transposed(res[1]), _from_transposed(res[2])
    else:
        cache_k, cache_v, page_table = sample_ctx
        k32, v32 = res[1], res[2]
        o = _sb_sample(res[3].reshape(batch, t, SB_DIM), k32.reshape(batch, t, SB_DIM),
                       v32.reshape(batch, t, SB_DIM), cache_k, cache_v, page_table)
        k_out, v_out = k32.reshape(shp), v32.reshape(shp)
    h2 = _proj_res(h2, [y_conv.reshape(batch * t, CONV_CH), o.reshape(batch * t, SB_DIM)], p["w_mix_out"][0])
    return h2, new_buf, k_out, v_out


def _odd_layer(h2, batch, p, sample_ctx):
    t = h2.shape[0] // batch
    prompt = sample_ctx is None
    past = 0 if prompt else sample_ctx["past"]
    tm = _row_tile(h2.shape[0])
    pos = past + jnp.arange(max(t, tm), dtype=jnp.int32) % t
    cos_t, sin_t = _rope_tables(pos, 128)
    res = _odd_in(h2, p["norm_mix"][1][None], p["w_in_odd"], cos_t, sin_t, batch, prompt)
    ck, cv = res[0], res[1]
    nb = -(-(past + t) // CMP_BLK)
    cw = (p["cmp_pe_k"][0], p["cmp_w1_k"][0], p["cmp_w2_k"][0]), (p["cmp_pe_v"][0], p["cmp_w1_v"][0], p["cmp_w2_v"][0])
    width = CMP_BLK * NSA_KV_DIM

    def end_tables(first, count):
        blk_end = (first + jnp.arange(count, dtype=jnp.int32)) * CMP_BLK + (CMP_BLK - 1)
        return tuple(jnp.tile(x, (batch, 1)) for x in _rope_tables(blk_end, 128))

    if prompt:
        kc = _compress([ck.reshape(batch * nb, width)], *cw[0], end_tables(0, nb), [True])
        vc = _compress([cv.reshape(batch * nb, width)], *cw[1], None, [True])
    else:
        nbp = past // CMP_BLK
        assert past % CMP_BLK == 0 and nb == nbp + 1
        pt = sample_ctx["page_table"]
        new_block = lambda x: jnp.pad(x.reshape(batch, t, NSA_KV_DIM), ((0, 0), (0, CMP_BLK - t), (0, 0))).reshape(batch, width)
        k_tabs = tuple(jnp.concatenate([a, b], axis=0) for a, b in zip(end_tables(0, nbp), end_tables(nbp, 1)))
        halves = []
        for cache, new, w, tabs in ((sample_ctx["cmp_k"], ck, cw[0], k_tabs), (sample_ctx["cmp_v"], cv, cw[1], None)):
            x_past = _cmp_gather(cache, w[0], pt).reshape(batch * nbp, width)
            c = _compress([x_past, new_block(new)], *w, tabs, [False, True])
            halves.append(jnp.concatenate([c[:batch * nbp].reshape(batch, nbp, NSA_KV_DIM),
                                           c[batch * nbp:].reshape(batch, 1, NSA_KV_DIM)], axis=1))
        kc, vc = halves
    kc = kc.reshape(batch, nb, NSA_KV_DIM)
    vc = vc.reshape(batch, nb, NSA_KV_DIM)
    padc = lambda x: jnp.pad(x, ((0, 0), (0, NB_PAD - nb), (0, 0))).astype(BF16)
    kvshape = (batch, t, NSA_KVH, HEAD_DIM)
    if prompt:
        hm = lambda x: jnp.transpose(padc(x).reshape(batch, NB_PAD, NSA_KVH, HEAD_DIM), (0, 2, 1, 3))
        t32 = res[2:8]
        q_t, gates_t, sk_hm, sv_t, wk_hm, wv_t = res[8:]
        o = _nsa_prompt(q_t, hm(kc), jnp.swapaxes(hm(vc), 2, 3), sk_hm, sv_t, wk_hm, wv_t, gates_t, nb)
        keep = min(WINDOW, t)
        outs = [_from_transposed(x) for x in t32[:4]] + [_from_transposed(x[..., -keep:]) for x in t32[4:]]
    else:
        sk, sv, wk, wv = res[2:6]
        q, gates = res[6:]
        gates = gates.reshape(batch, t, NSA_KVH, GATE_PAD)[..., :3 * NSA_GQA]
        r3 = lambda x: x.reshape(batch, t, NSA_KV_DIM)
        wkc, wvc = sample_ctx["win_k"], sample_ctx["win_v"]
        wb = wkc.shape[1]
        o = _nsa_sample(q.reshape(batch, t, ODD_Q), padc(kc), padc(vc), gates,
                        r3(sk), r3(sv), r3(wk), r3(wv), _keys_minor(wkc), _keys_minor(wvc),
                        sample_ctx["sel_k"], sample_ctx["sel_v"], sample_ctx["page_table"], past, nb)
        r4 = lambda x: x.reshape(kvshape)
        outs = [r4(ck), r4(cv), r4(sk), r4(sv),
                jnp.concatenate([wkc, r4(wk)], axis=1)[:, -wb:], jnp.concatenate([wvc, r4(wv)], axis=1)[:, -wb:]]
    h2 = _proj_res(h2, [o.reshape(batch * t, ODD_Q)], p["w_mix_out"][1])
    return (h2,) + tuple(outs)


def _trunk(x, p, conv_buf, mem_kv, even_ctx, odd_ctx):
    batch, t, _ = x.shape
    h2 = x.reshape(batch * t, D_MODEL)
    h2, new_buf, sbk, sbv = _even_layer(h2, batch, p, conv_buf, even_ctx)
    h2 = _common_tail(h2, batch, 0, mem_kv[0][0], mem_kv[0][1], p, None)
    h2, ck, cv, sk, sv, wk, wv = _odd_layer(h2, batch, p, odd_ctx)
    h2 = _common_tail(h2, batch, 1, mem_kv[1][0], mem_kv[1][1], p, p["final_norm"][None])
    st = lambda a: a[None]
    return (h2.reshape(batch, t, D_MODEL), st(sbk), st(sbv), st(new_buf),
            st(ck), st(cv), st(sk), st(sv), st(wk), st(wv))


def kernel(x_prompt, x_sample, mem_prompt, cache_sb_k, cache_sb_v, state_conv,
           cache_nsa_cmp_k, cache_nsa_cmp_v, cache_nsa_sel_k, cache_nsa_sel_v,
           cache_nsa_win_k, cache_nsa_win_v, cache_mem_k, cache_mem_v, page_table,
           norm_mix, norm_mem, norm_ffn, final_norm, w_in_even, w_in_odd, w_mix_out,
           conv_w, conv_b, conv_ln_g, conv_ln_b,
           cmp_pe_k, cmp_w1_k, cmp_w2_k, cmp_pe_v, cmp_w1_v, cmp_w2_v,
           w_mem_q, w_mem_k, w_mem_v, w_mem_o, w_ffn_gate, w_ffn_up, w_ffn_down):
    assert norm_mix.shape[0] == 2 and w_in_even.shape[0] == 1 and w_in_odd.shape[0] == 1
    bp = x_prompt.shape[0]
    bs = x_sample.shape[0]
    bf = lambda w: w.astype(BF16)
    p = dict(
        norm_mix=norm_mix, norm_mem=norm_mem, norm_ffn=norm_ffn, final_norm=final_norm,
        w_in_even=bf(w_in_even),
        w_in_odd=_odd_in_weights(w_in_odd[0]),
        w_mix_out=bf(w_mix_out), conv_w=conv_w, conv_b=conv_b, conv_ln_g=conv_ln_g, conv_ln_b=conv_ln_b,
        cmp_pe_k=cmp_pe_k, cmp_w1_k=cmp_w1_k, cmp_w2_k=cmp_w2_k,
        cmp_pe_v=cmp_pe_v, cmp_w1_v=cmp_w1_v, cmp_w2_v=cmp_w2_v,
        w_mem_q=bf(w_mem_q), w_mem_o=bf(w_mem_o),
        w_ffn_gate=bf(w_ffn_gate), w_ffn_up=bf(w_ffn_up), w_ffn_down=bf(w_ffn_down))

    ml = mem_prompt.shape[1]
    mem = _mem_kv(mem_prompt.reshape(bp * ml, D_MODEL),
                  [bf(w_mem_k[0]), bf(w_mem_v[0]), bf(w_mem_k[1]), bf(w_mem_v[1])])
    m3 = lambda a: a.reshape(bp, ml, D_MODEL)
    m4 = lambda a: a.reshape(bp, ml, MEM_HEADS, MEM_HD)
    mem_kv_p = [(m3(mem[1]), m3(mem[3])), (m3(mem[5]), m3(mem[7]))]
    mem_k_p = jnp.stack([m4(mem[0]), m4(mem[4])])
    mem_v_p = jnp.stack([m4(mem[2]), m4(mem[6])])
    zero_buf = jnp.zeros((bp, CONV_W - 1, CONV_CH), F32)
    outs_p = _trunk(x_prompt, p, zero_buf, mem_kv_p, None, None)

    past = page_table.shape[1] * PAGE
    even_ctx = (_keys_minor(cache_sb_k[0]), _keys_minor(cache_sb_v[0]), page_table)
    odd_ctx = dict(past=past, page_table=page_table,
                   cmp_k=_keys_minor(cache_nsa_cmp_k[0]), cmp_v=_keys_minor(cache_nsa_cmp_v[0]),
                   sel_k=_keys_minor(cache_nsa_sel_k[0]), sel_v=_keys_minor(cache_nsa_sel_v[0]),
                   win_k=cache_nsa_win_k[0], win_v=cache_nsa_win_v[0])
    mem_kv_s = [(cache_mem_k, cache_mem_v)] * cache_mem_k.shape[0]
    outs_s = _trunk(x_sample, p, state_conv[0], mem_kv_s, even_ctx, odd_ctx)

    return (outs_p[0], outs_s[0]) + tuple(outs_p[1:]) + (mem_k_p, mem_v_p) + tuple(outs_s[1:])
```

```python
import functools
import math

import jax
import jax.numpy as jnp
from jax import lax
from jax.experimental import pallas as pl
from jax.experimental.pallas import tpu as pltpu

F32 = jnp.float32
BF16 = jnp.bfloat16

D_MODEL = 1024
HEAD_DIM = 64
CONV_CH = 512
CONV_W = 31
SB_HEADS = 8
SB_DIM = 512
NSA_HEADS = 16
NSA_GQA = 4
NSA_KVH = 4
NSA_KV_DIM = 256
CMP_BLK = 64
CMP_HID = 256
SEL_TOPN = 16
FORCE_SCORE = 1.0e4
WINDOW = 512
MEM_HEADS = 4
MEM_HD = 256
D_FF = 2816
ROPE_THETA = 10000.0
NORM_EPS = 1e-6
NEG_INF = -1e30
PAGE = 128

V7X_VMEM_BYTES = 64 * 1024 * 1024
VMEM_LIMIT = V7X_VMEM_BYTES - 8 * 1024 * 1024
ATT_TILE = 256
NSA_TILE = 512


def _cp(sem):
    return pltpu.CompilerParams(dimension_semantics=sem, vmem_limit_bytes=VMEM_LIMIT)


def _nt(a, b):
    return lax.dot_general(a, b, (((1,), (1,)), ((), ())), preferred_element_type=F32)


def _dot(a, b):
    return jnp.dot(a, b, preferred_element_type=F32)


def _rms(x, g):
    y = x * lax.rsqrt(jnp.mean(x * x, axis=-1, keepdims=True) + NORM_EPS)
    return y * g


def _sigmoid(x):
    return 1.0 / (1.0 + jnp.exp(-x))


def _const_spec(shape):
    n = len(shape)
    return pl.BlockSpec(shape, lambda *a: (0,) * n)


def _row_spec(tm, n):
    return pl.BlockSpec((tm, n), lambda i: (i, 0))


def _hm_spec(heads, tm, width, nt):
    return pl.BlockSpec((1, heads, tm, width), lambda i: (i // nt, 0, i % nt, 0))


def _tr_spec(heads, tm, nt):
    return pl.BlockSpec((1, heads, HEAD_DIM, tm), lambda i: (i // nt, 0, 0, i % nt))


def _store_transposed(x, heads, f32_ref, bf16_ref):
    xt = x.T.reshape(heads, HEAD_DIM, x.shape[0])
    if f32_ref is not None:
        f32_ref[0] = xt
    if bf16_ref is not None:
        bf16_ref[0] = xt.astype(BF16)


def _store_head_major(x, heads, bf16_ref):
    for hh in range(heads):
        bf16_ref[0, hh] = x[:, hh * HEAD_DIM:(hh + 1) * HEAD_DIM].astype(BF16)


def _from_transposed(xt):
    return jnp.transpose(xt, (0, 3, 1, 2))


def _row_tile(m):
    return 512 if m >= 4096 else min(256, m)


def _rope_tables(pos, width):
    half = HEAD_DIM // 2
    inv = ROPE_THETA ** (-jnp.arange(half, dtype=F32) / half)
    ang = pos.astype(F32)[:, None] * inv[None, :]
    cos = jnp.cos(ang)
    sin = jnp.sin(ang)
    c = jnp.concatenate([cos, cos], axis=-1)
    s = jnp.concatenate([-sin, sin], axis=-1)
    reps = width // HEAD_DIM
    return jnp.tile(c, (1, reps)), jnp.tile(s, (1, reps))


def _rope128(x, c, s):
    lane = lax.broadcasted_iota(jnp.int32, x.shape, 1)
    first = (lane % HEAD_DIM) < (HEAD_DIM // 2)
    rot = jnp.where(first, pltpu.roll(x, 96, 1), pltpu.roll(x, 32, 1))
    return x * c + rot * s


def _rope_wide(x, c, s):
    n = x.shape[1] // 128
    return jnp.concatenate([_rope128(x[:, i * 128:(i + 1) * 128], c, s) for i in range(n)], axis=1)


def _even_in_body(head_major, h_ref, g_ref, w_ref, glu_ref, k32_ref, v32_ref, q_ref, *hm_refs):
    xn = _rms(h_ref[...], g_ref[...]).astype(BF16)

    def mm(lo, hi):
        return _dot(xn, w_ref[:, lo:hi])

    glu_ref[...] = mm(0, CONV_CH) * _sigmoid(mm(CONV_CH, 2 * CONV_CH))
    base = 2 * CONV_CH
    q = mm(base, base + SB_DIM) * (HEAD_DIM ** -0.5)
    k = mm(base + SB_DIM, base + 2 * SB_DIM)
    v = mm(base + 2 * SB_DIM, base + 3 * SB_DIM)
    if head_major:
        kh_ref, vh_ref = hm_refs
        _store_transposed(q, SB_HEADS, None, q_ref)
        _store_transposed(k, SB_HEADS, k32_ref, None)
        _store_head_major(k, SB_HEADS, kh_ref)
        _store_transposed(v, SB_HEADS, v32_ref, vh_ref)
    else:
        k32_ref[...] = k
        v32_ref[...] = v
        q_ref[...] = q.astype(BF16)


def _even_in(h2, g, w, batch, head_major):
    m = h2.shape[0]
    t = m // batch
    tm = _row_tile(m)
    nt = max(t // tm, 1)
    outs = [jax.ShapeDtypeStruct((m, CONV_CH), F32)]
    specs = [_row_spec(tm, CONV_CH)]
    if head_major:
        kt = lambda dt: jax.ShapeDtypeStruct((batch, SB_HEADS, HEAD_DIM, t), dt)
        outs += [kt(F32), kt(F32), kt(BF16), jax.ShapeDtypeStruct((batch, SB_HEADS, t, HEAD_DIM), BF16), kt(BF16)]
        specs += [_tr_spec(SB_HEADS, tm, nt)] * 3 + [_hm_spec(SB_HEADS, tm, HEAD_DIM, nt), _tr_spec(SB_HEADS, tm, nt)]
    else:
        outs += [jax.ShapeDtypeStruct((m, SB_DIM), F32)] * 2 + [jax.ShapeDtypeStruct((m, SB_DIM), BF16)]
        specs += [_row_spec(tm, SB_DIM)] * 3
    return pl.pallas_call(
        functools.partial(_even_in_body, head_major),
        grid=(m // tm,),
        in_specs=[_row_spec(tm, D_MODEL), _const_spec((1, D_MODEL)), _const_spec(w.shape)],
        out_specs=specs, out_shape=outs,
        compiler_params=_cp(("parallel",)), name="even_in",
    )(h2, g, w)


CONV_PAD = 32


def _conv_body(t, chunk, hp_ref, w_ref, b_ref, lg_ref, lb_ref, o_ref):
    win_rows = chunk + CONV_PAD

    def one_chunk(c, _):
        base = pl.multiple_of(c * chunk, chunk)
        win = hp_ref[0, pl.ds(base, win_rows), :]
        acc = jnp.zeros((chunk, CONV_CH), F32)
        for r in range(8):
            sh = win if r == 0 else pltpu.roll(win, win_rows - r, 0)
            for a in range(4):
                tap = 8 * a + r
                if tap < CONV_W:
                    acc = acc + sh[8 * a:8 * a + chunk] * w_ref[tap:tap + 1, :]
        y = acc + b_ref[...]
        mu = jnp.mean(y, axis=-1, keepdims=True)
        var = jnp.mean(jnp.square(y - mu), axis=-1, keepdims=True)
        y = (y - mu) * lax.rsqrt(var + NORM_EPS) * lg_ref[...] + lb_ref[...]
        o_ref[0, pl.ds(base, chunk), :] = (y * _sigmoid(y)).astype(o_ref.dtype)
        return 0

    lax.fori_loop(0, t // chunk, one_chunk, 0)


def _conformer_conv(glu, buf, w_dw, b_dw, ln_g, ln_b):
    b, t, c = glu.shape
    hp = jnp.concatenate([buf.astype(F32), glu, jnp.zeros((b, CONV_PAD - (CONV_W - 1), c), F32)], axis=1)
    chunk = min(t, 128)
    return pl.pallas_call(
        functools.partial(_conv_body, t, chunk),
        grid=(b,),
        in_specs=[pl.BlockSpec((1, t + CONV_PAD, c), lambda i: (i, 0, 0)),
                  _const_spec((CONV_PAD, c)), _const_spec((1, c)), _const_spec((1, c)), _const_spec((1, c))],
        out_specs=pl.BlockSpec((1, t, c), lambda i: (i, 0, 0)),
        out_shape=jax.ShapeDtypeStruct((b, t, c), BF16),
        compiler_params=_cp(("parallel",)), name="conformer_conv",
    )(hp, jnp.pad(w_dw, ((0, CONV_PAD - CONV_W), (0, 0))), b_dw[None], ln_g[None], ln_b[None])


def _softplus(z):
    return jnp.maximum(z, 0.0) + jnp.log(1.0 + jnp.exp(-jnp.abs(z)))


def _sb_tiles(zs, carries, valid, tri2, keys_axis):
    valids = valid if isinstance(valid, (list, tuple)) else [valid] * len(zs)
    sps = []
    for z, ok in zip(zs, valids):
        sp = _softplus(z)
        sps.append(sp if ok is None else jnp.where(ok, sp, 0.0))
    laters = []
    for sp in sps:
        hi = sp.astype(BF16)
        lo = (sp - hi.astype(F32)).astype(BF16)
        split = jnp.concatenate([hi, lo], axis=keys_axis)
        laters.append(_dot(split, tri2) if keys_axis == 1 else _dot(tri2, split))
    first = (lambda a: a[:, 0:1]) if keys_axis == 1 else (lambda a: a[0:1, :])
    ws, new = [], []
    for n, (z, sp, later, ok) in enumerate(zip(zs, sps, laters, valids)):
        carry = new[n - len(carries)] if n >= len(carries) else carries[n]
        w = jnp.exp(z - sp - later - carry)
        ws.append((w if ok is None else jnp.where(ok, w, 0.0)).astype(BF16))
        new.append(carry + first(later) + first(sp))
    return ws, new


def _tri2(tk, keys_axis):
    s = jnp.arange(tk)[:, None]
    j = jnp.arange(tk)[None, :]
    if keys_axis == 1:
        tri = (s > j).astype(BF16)
        return jnp.concatenate([tri, tri], axis=0)
    tri = (j > s).astype(BF16)
    return jnp.concatenate([tri, tri], axis=1)


def _sb_prompt_body(q_ref, k_ref, v_ref, tri_ref, o_ref):
    i = pl.program_id(2)
    tq = q_ref.shape[3]
    key = lax.broadcasted_iota(jnp.int32, (tq, tq), 0)
    qry = lax.broadcasted_iota(jnp.int32, (tq, tq), 1)
    diag_valid = key < qry
    tri2 = tri_ref[...]
    nh = q_ref.shape[1]

    def tiles(js, st, valid):
        starts = [pl.multiple_of(j * tq, tq) for j in js]
        zs = [_dot(k_ref[0, hh, pl.ds(s0, tq), :], q_ref[0, hh]) for s0 in starts for hh in range(nh)]
        ws, carries = _sb_tiles(zs, [s[0] for s in st], valid, tri2, 0)
        accs = [s[1] for s in st]
        for n, s0 in enumerate(starts):
            for hh in range(nh):
                accs[hh] = accs[hh] + _dot(v_ref[0, hh, :, pl.ds(s0, tq)], ws[n * nh + hh])
        return tuple(zip(carries[-nh:], accs))

    init = tuple((jnp.zeros((1, tq), F32), jnp.zeros((HEAD_DIM, tq), F32)) for _ in range(nh))
    st = tiles([i], init, diag_valid)
    odd = i % 2
    st = lax.cond(odd == 1, lambda s: tiles([i - 1], s, None), lambda s: s, st)
    top = i - 1 - odd
    st = lax.fori_loop(0, i // 2, lambda n, s: tiles([top - 2 * n, top - 2 * n - 1], s, None), st)
    o_ref[0] = jnp.concatenate([s[1] for s in st], axis=0).T.astype(o_ref.dtype)


SB_HEADS_PER_STEP = 4


def _sb_prompt(q_t, k_hm, v_t):
    b, h, d, t = q_t.shape
    tq = min(ATT_TILE, t)
    nh = SB_HEADS_PER_STEP
    return pl.pallas_call(
        _sb_prompt_body,
        grid=(b, h // nh, t // tq),
        in_specs=[pl.BlockSpec((1, nh, d, tq), lambda bi, hp, i: (bi, hp, 0, i)),
                  pl.BlockSpec((1, nh, t, d), lambda bi, hp, i: (bi, hp, 0, 0)),
                  pl.BlockSpec((1, nh, d, t), lambda bi, hp, i: (bi, hp, 0, 0)),
                  _const_spec((tq, 2 * tq))],
        out_specs=pl.BlockSpec((1, tq, nh * d), lambda bi, hp, i: (bi, i, hp)),
        out_shape=jax.ShapeDtypeStruct((b, t, h * d), BF16),
        compiler_params=_cp(("parallel", "parallel", "arbitrary")), name="sb_prompt",
    )(q_t, k_hm, v_t, _tri2(tq, 0))


def _block_diag_rows(q, groups):
    b, t, h, d = q.shape
    per = h // groups
    qg = jnp.transpose(q.reshape(b, t, groups, per, d), (0, 2, 3, 1, 4))
    eye = jnp.eye(groups, dtype=q.dtype)[None, :, None, None, :, None]
    return (qg[:, :, :, :, None, :] * eye).reshape(b, h * t, groups * d)


def _diag_rows_out(o, t, groups):
    b, rows, gd = o.shape
    d = gd // groups
    per = rows // (groups * t)
    o6 = o.reshape(b, groups, per, t, groups, d)
    idx = jnp.arange(groups)
    og = o6[:, idx, :, :, idx, :]
    return jnp.transpose(og, (1, 3, 0, 2, 4)).reshape(b, t, groups * per * d)


def _sb_sample_body(npages, t_new, pt_ref, qbd_ref, kn_ref, vn_ref, tri_ref, *refs):
    k_refs = refs[:npages]
    v_refs = refs[npages:2 * npages]
    o_ref = refs[2 * npages]
    qbd = qbd_ref[0]
    rows = qbd.shape[0]
    tri2 = tri_ref[...]
    width = kn_ref.shape[2]
    pad = jnp.zeros((PAGE - t_new, width), F32)
    kn = jnp.concatenate([kn_ref[0], pad], axis=0).astype(BF16)
    vn = jnp.concatenate([vn_ref[0], pad], axis=0).astype(BF16)
    qoff = lax.broadcasted_iota(jnp.int32, (rows, PAGE), 0) % t_new
    col = lax.broadcasted_iota(jnp.int32, (rows, PAGE), 1)
    order = list(reversed(range(npages)))
    zs = [_nt(qbd, kn)] + [_dot(qbd, k_refs[p][0].reshape(width, PAGE).astype(BF16)) for p in order]
    ws, _ = _sb_tiles(zs, [jnp.zeros((rows, 1), F32)], [col < qoff] + [None] * npages, tri2, 1)
    acc = _dot(ws[0], vn)
    for w, p in zip(ws[1:], order):
        acc = acc + _nt(w, v_refs[p][0].reshape(width, PAGE).astype(BF16))
    o_ref[0] = acc


def _page_specs(npages, heads):
    return [pl.BlockSpec((1, heads, HEAD_DIM, PAGE), lambda b, pt, p=p: (pt[b, p], 0, 0, 0))
            for p in range(npages)]


def _keys_minor(cache):
    nd = cache.ndim
    return jnp.transpose(cache, tuple(range(nd - 3)) + (nd - 2, nd - 1, nd - 3))


def _sb_sample(q, k_new, v_new, cache_k, cache_v, page_table):
    b, t, w = k_new.shape
    npages = page_table.shape[1]
    qbd = _block_diag_rows(q.reshape(b, t, SB_HEADS, HEAD_DIM), SB_HEADS)
    rows = SB_HEADS * t
    bspec = lambda r, c: pl.BlockSpec((1, r, c), lambda bi, pt: (bi, 0, 0))
    grid_spec = pltpu.PrefetchScalarGridSpec(
        num_scalar_prefetch=1, grid=(b,),
        in_specs=[bspec(rows, w), bspec(t, w), bspec(t, w),
                  pl.BlockSpec((2 * PAGE, PAGE), lambda bi, pt: (0, 0))]
                 + _page_specs(npages, SB_HEADS) + _page_specs(npages, SB_HEADS),
        out_specs=bspec(rows, w))
    o = pl.pallas_call(
        functools.partial(_sb_sample_body, npages, t),
        grid_spec=grid_spec,
        out_shape=jax.ShapeDtypeStruct((b, rows, w), F32),
        compiler_params=_cp(("parallel",)), name="sb_sample",
    )(page_table, qbd, k_new, v_new, _tri2(PAGE, 1), *([cache_k] * npages), *([cache_v] * npages))
    return _diag_rows_out(o, t, SB_HEADS)


def _proj_res_body(n_in, *refs):
    h_ref = refs[0]
    x_refs = refs[1:1 + n_in]
    w_ref = refs[1 + n_in]
    o_ref = refs[2 + n_in]
    acc = h_ref[...]
    off = 0
    for x_ref in x_refs:
        k = x_ref.shape[1]
        acc = acc + _dot(x_ref[...].astype(BF16), w_ref[off:off + k, :])
        off += k
    o_ref[...] = acc


def _proj_res(h2, xs, w):
    m = h2.shape[0]
    tm = _row_tile(m)
    return pl.pallas_call(
        functools.partial(_proj_res_body, len(xs)),
        grid=(m // tm,),
        in_specs=[_row_spec(tm, D_MODEL)] + [_row_spec(tm, x.shape[1]) for x in xs] + [_const_spec(w.shape)],
        out_specs=_row_spec(tm, D_MODEL),
        out_shape=jax.ShapeDtypeStruct((m, D_MODEL), F32),
        compiler_params=_cp(("parallel",)), name="proj_res",
    )(h2, *xs, w)


def _norm_proj_body(scale, h_ref, g_ref, w_ref, o_ref):
    xn = _rms(h_ref[...], g_ref[...]).astype(BF16)
    o_ref[...] = (_dot(xn, w_ref[...]) * scale).astype(o_ref.dtype)


def _norm_proj(h2, g, w, scale):
    m = h2.shape[0]
    tm = _row_tile(m)
    return pl.pallas_call(
        functools.partial(_norm_proj_body, scale),
        grid=(m // tm,),
        in_specs=[_row_spec(tm, D_MODEL), _const_spec((1, D_MODEL)), _const_spec(w.shape)],
        out_specs=_row_spec(tm, w.shape[1]),
        out_shape=jax.ShapeDtypeStruct((m, w.shape[1]), BF16),
        compiler_params=_cp(("parallel",)), name="norm_proj",
    )(h2, g, w)


def _mem_kv_body(x_ref, wk0, wv0, wk1, wv1, *o_refs):
    x = x_ref[...].astype(BF16)
    for n, w_ref in enumerate((wk0, wv0, wk1, wv1)):
        y = _dot(x, w_ref[...])
        o_refs[2 * n][...] = y
        o_refs[2 * n + 1][...] = y.astype(BF16)


def _mem_kv(x2, ws):
    m = x2.shape[0]
    tm = _row_tile(m)
    outs, specs = [], []
    for _ in ws:
        outs += [jax.ShapeDtypeStruct((m, D_MODEL), F32), jax.ShapeDtypeStruct((m, D_MODEL), BF16)]
        specs += [_row_spec(tm, D_MODEL)] * 2
    return pl.pallas_call(
        _mem_kv_body, grid=(m // tm,),
        in_specs=[_row_spec(tm, D_MODEL)] + [_const_spec(w.shape) for w in ws],
        out_specs=specs, out_shape=outs,
        compiler_params=_cp(("parallel",)), name="mem_kv",
    )(x2, *ws)


def _mem_block_body(h_ref, g_ref, wq_ref, k_ref, v_ref, wo_ref, o_ref):
    h = h_ref[...]
    q = (_dot(_rms(h, g_ref[...]).astype(BF16), wq_ref[...]) * (MEM_HD ** -0.5)).astype(BF16)
    outs = []
    for hh in range(MEM_HEADS):
        sl = slice(hh * MEM_HD, (hh + 1) * MEM_HD)
        s = _nt(q[:, sl], k_ref[0, :, sl])
        m = jnp.max(s, axis=-1, keepdims=True)
        e = jnp.exp(s - m)
        p = e / jnp.sum(e, axis=-1, keepdims=True)
        outs.append(_dot(p.astype(BF16), v_ref[0, :, sl]).astype(BF16))
    o_ref[...] = h + _dot(jnp.concatenate(outs, axis=1), wo_ref[...])


def _mem_block(h2, g, wq, mk, mv, wo, batch):
    m = h2.shape[0]
    tm = _row_tile(m)
    nt = (m // batch) // tm
    ml = mk.shape[1]
    kv_spec = pl.BlockSpec((1, ml, D_MODEL), lambda i: (i // nt, 0, 0))
    return pl.pallas_call(
        _mem_block_body, grid=(m // tm,),
        in_specs=[_row_spec(tm, D_MODEL), _const_spec((1, D_MODEL)), _const_spec(wq.shape), kv_spec, kv_spec,
                  _const_spec(wo.shape)],
        out_specs=_row_spec(tm, D_MODEL),
        out_shape=jax.ShapeDtypeStruct((m, D_MODEL), F32),
        compiler_params=_cp(("parallel",)), name="mem_block",
    )(h2, g, wq, mk, mv, wo)


def _mem_attn_cached_body(q_ref, k_ref, v_ref, o_ref):
    ml, heads, hd = k_ref.shape[2:]
    t = q_ref.shape[1] // heads
    shape = (ml * heads, heads * t)
    same = (lax.broadcasted_iota(jnp.int32, shape, 0) % heads) == (lax.broadcasted_iota(jnp.int32, shape, 1) // t)
    reqs = range(q_ref.shape[0])
    rows = lambda ref, n: ref[0, n].reshape(ml * heads, hd).astype(BF16)
    sms = [jnp.where(same, _nt(rows(k_ref, n), q_ref[n]), NEG_INF) for n in reqs]
    ps = []
    for sm in sms:
        m = jnp.max(sm, axis=0, keepdims=True)
        e = jnp.where(same, jnp.exp(sm - m), 0.0)
        ps.append((e / jnp.sum(e, axis=0, keepdims=True)).astype(BF16))
    for n in reqs:
        o_ref[n] = lax.dot_general(ps[n], rows(v_ref, n), (((0,), (0,)), ((), ())),
                                   preferred_element_type=F32).astype(o_ref.dtype)


MEM_REQS_PER_STEP = 2


def _mem_attn_cached(q, cache_k, cache_v, layer):
    b, t, d = q.shape
    _, _, ml, heads, hd = cache_k.shape
    rows = heads * t
    q_rows = jnp.transpose(q.reshape(b, t, heads, hd), (0, 2, 1, 3)).reshape(b, rows, hd)
    per = MEM_REQS_PER_STEP if b % MEM_REQS_PER_STEP == 0 else 1
    kv_spec = pl.BlockSpec((1, per, ml, heads, hd), lambda bi: (layer, bi, 0, 0, 0))
    o = pl.pallas_call(
        _mem_attn_cached_body, grid=(b // per,),
        in_specs=[pl.BlockSpec((per, rows, hd), lambda bi: (bi, 0, 0)), kv_spec, kv_spec],
        out_specs=pl.BlockSpec((per, rows, hd), lambda bi: (bi, 0, 0)),
        out_shape=jax.ShapeDtypeStruct((b, rows, hd), BF16),
        compiler_params=_cp(("parallel",)), name="mem_attn_cached",
    )(q_rows, cache_k, cache_v)
    return jnp.transpose(o.reshape(b, heads, t, hd), (0, 2, 1, 3)).reshape(b, t, d)


FFN_CHUNK = 512


def _ffn_body(final, h_ref, g_ref, wg_ref, wu_ref, wd_ref, *rest):
    if final:
        gf_ref, o_ref = rest
    else:
        (o_ref,) = rest
    h = h_ref[...]
    xn = _rms(h, g_ref[...]).astype(BF16)
    acc = h
    for lo in range(0, D_FF, FFN_CHUNK):
        hi = min(lo + FFN_CHUNK, D_FF)
        gate = _dot(xn, wg_ref[:, lo:hi])
        up = _dot(xn, wu_ref[:, lo:hi])
        act = (gate * _sigmoid(gate) * up).astype(BF16)
        acc = acc + _dot(act, wd_ref[lo:hi, :])
    if final:
        acc = _rms(acc, gf_ref[...])
    o_ref[...] = acc


def _ffn(h2, g, wg, wu, wd, final_g=None):
    m = h2.shape[0]
    tm = _row_tile(m)
    final = final_g is not None
    once = lambda shape: pl.BlockSpec(shape, lambda i: (0, 0), pipeline_mode=pl.Buffered(1))
    in_specs = [_row_spec(tm, D_MODEL), _const_spec((1, D_MODEL)), once(wg.shape), once(wu.shape), once(wd.shape)]
    args = [h2, g, wg, wu, wd]
    if final:
        in_specs.append(_const_spec((1, D_MODEL)))
        args.append(final_g)
    return pl.pallas_call(
        functools.partial(_ffn_body, final), grid=(m // tm,),
        in_specs=in_specs, out_specs=_row_spec(tm, D_MODEL),
        out_shape=jax.ShapeDtypeStruct((m, D_MODEL), F32),
        compiler_params=_cp(("parallel",)), name="ffn",
    )(*args)


ODD_Q = NSA_HEADS * HEAD_DIM
ODD_GATES = 3 * NSA_HEADS
ODD_IN = ODD_Q + 6 * NSA_KV_DIM + ODD_GATES
GATE_PAD = 16
GATE_LANES = NSA_KVH * GATE_PAD
ODD_IN_PAD = ODD_Q + 6 * NSA_KV_DIM + 128


def _odd_in_weights(w_in_odd):
    gates = w_in_odd[:, ODD_Q + 6 * NSA_KV_DIM:].reshape(D_MODEL, NSA_KVH, 3 * NSA_GQA)
    gates = jnp.pad(gates, ((0, 0), (0, 0), (0, GATE_PAD - 3 * NSA_GQA))).reshape(D_MODEL, GATE_LANES)
    return jnp.concatenate([w_in_odd[:, :ODD_Q + 6 * NSA_KV_DIM],
                            jnp.pad(gates, ((0, 0), (0, 128 - GATE_LANES)))], axis=1).astype(BF16)


def _odd_in_body(head_major, h_ref, g_ref, w_ref, c_ref, s_ref, *o_refs):
    xn = _rms(h_ref[...], g_ref[...]).astype(BF16)
    c = c_ref[...]
    s = s_ref[...]

    def mm(lo, hi):
        return _dot(xn, w_ref[:, lo:hi])

    kv = lambda n: mm(ODD_Q + n * NSA_KV_DIM, ODD_Q + (n + 1) * NSA_KV_DIM)
    q = _rope_wide(mm(0, ODD_Q), c, s) * (HEAD_DIM ** -0.5)
    ck, cv = kv(0), kv(1)
    sk, sv = _rope_wide(kv(2), c, s), kv(3)
    wk, wv = _rope_wide(kv(4), c, s), kv(5)
    gates = _sigmoid(mm(ODD_Q + 6 * NSA_KV_DIM, ODD_IN_PAD))[:, :GATE_LANES]
    if head_major:
        ck_rows, cv_rows = o_refs[:2]
        t32 = o_refs[2:8]
        q_ref, g_out, sk_rows, sv_t, wk_rows, wv_t = o_refs[8:14]
        ck_rows[...] = ck
        cv_rows[...] = cv
        for val, r32, r16 in zip((ck, cv, sk, sv, wk, wv), t32, (None, None, None, sv_t, None, wv_t)):
            _store_transposed(val, NSA_KVH, r32, r16)
        _store_transposed(q, NSA_HEADS, None, q_ref)
        _store_head_major(sk, NSA_KVH, sk_rows)
        _store_head_major(wk, NSA_KVH, wk_rows)
        g_out[0] = gates.T.reshape(NSA_KVH, GATE_PAD, gates.shape[0])
    else:
        for ref, val in zip(o_refs[:6], (ck, cv, sk, sv, wk, wv)):
            ref[...] = val
        q_ref, g_out = o_refs[6:]
        q_ref[...] = q.astype(BF16)
        g_out[...] = gates


def _odd_in(h2, g, w, cos_t, sin_t, batch, head_major):
    m = h2.shape[0]
    t = m // batch
    tm = _row_tile(m)
    nt = max(t // tm, 1)
    ntab = cos_t.shape[0] // tm
    tab_spec = pl.BlockSpec((tm, 128), lambda i: (i % ntab, 0))
    kv = jax.ShapeDtypeStruct((m, NSA_KV_DIM), F32)
    if head_major:
        kt = lambda dt: jax.ShapeDtypeStruct((batch, NSA_KVH, HEAD_DIM, t), dt)
        rows16 = jax.ShapeDtypeStruct((batch, NSA_KVH, t, HEAD_DIM), BF16)
        outs = [kv] * 2 + [kt(F32)] * 6
        outs += [jax.ShapeDtypeStruct((batch, NSA_HEADS, HEAD_DIM, t), BF16),
                 jax.ShapeDtypeStruct((batch, NSA_KVH, GATE_PAD, t), F32), rows16, kt(BF16), rows16, kt(BF16)]
        rows_spec = _hm_spec(NSA_KVH, tm, HEAD_DIM, nt)
        specs = [_row_spec(tm, NSA_KV_DIM)] * 2 + [_tr_spec(NSA_KVH, tm, nt)] * 6
        specs += [_tr_spec(NSA_HEADS, tm, nt),
                  pl.BlockSpec((1, NSA_KVH, GATE_PAD, tm), lambda i: (i // nt, 0, 0, i % nt)),
                  rows_spec, _tr_spec(NSA_KVH, tm, nt), rows_spec, _tr_spec(NSA_KVH, tm, nt)]
    else:
        outs = [kv] * 6 + [jax.ShapeDtypeStruct((m, ODD_Q), BF16), jax.ShapeDtypeStruct((m, GATE_LANES), F32)]
        specs = [_row_spec(tm, NSA_KV_DIM)] * 6 + [_row_spec(tm, ODD_Q), _row_spec(tm, GATE_LANES)]
    return pl.pallas_call(
        functools.partial(_odd_in_body, head_major), grid=(m // tm,),
        in_specs=[_row_spec(tm, D_MODEL), _const_spec((1, D_MODEL)), _const_spec(w.shape), tab_spec, tab_spec],
        out_specs=specs, out_shape=outs,
        compiler_params=_cp(("parallel",)), name="odd_in",
    )(h2, g, w, cos_t, sin_t)


def _gelu_tanh(x):
    return 0.5 * x * (1.0 + jnp.tanh(math.sqrt(2.0 / math.pi) * (x + 0.044715 * x * x * x)))


def _compress_body(add_pes, use_rope, *refs):
    n = len(add_pes)
    x_refs = refs[:n]
    pe_ref, w1_ref, w2_ref, c_ref, s_ref, o_ref, acc_ref, wbd_ref = refs[n:]
    l = pl.program_id(0)

    @pl.when(l == 0)
    def _():
        acc_ref[...] = jnp.zeros_like(acc_ref)
        wbd_ref[...] = jnp.zeros_like(wbd_ref)

    for gg in range(NSA_KVH):
        wbd_ref[gg * HEAD_DIM:(gg + 1) * HEAD_DIM, gg * CMP_HID:(gg + 1) * CMP_HID] = w1_ref[0]
    off = 0
    for x_ref, add_pe in zip(x_refs, add_pes):
        x = x_ref[...]
        if add_pe:
            x = (x + pe_ref[0]).astype(BF16)
        acc_ref[off:off + x.shape[0], :] += _dot(x, wbd_ref[...])
        off += x.shape[0]

    @pl.when(l == CMP_BLK - 1)
    def _():
        hid = _gelu_tanh(acc_ref[...]).astype(BF16)
        y = _dot(hid, w2_ref[...])
        if use_rope:
            y = _rope_wide(y, c_ref[...], s_ref[...])
        o_ref[...] = y


def _compress(xs, pe, w1, w2, tables, add_pes):
    rows = sum(x.shape[0] for x in xs)
    pe_t = jnp.tile(pe, (1, NSA_KVH))[:, None, :]
    w2bd = jnp.kron(jnp.eye(NSA_KVH, dtype=F32), w2).astype(BF16)
    use_rope = tables is not None
    if use_rope:
        c_t, s_t = tables
    else:
        c_t = s_t = jnp.zeros((rows, 128), F32)
    return pl.pallas_call(
        functools.partial(_compress_body, tuple(add_pes), use_rope), grid=(CMP_BLK,),
        in_specs=[pl.BlockSpec((x.shape[0], NSA_KV_DIM), lambda l: (0, l)) for x in xs]
                 + [pl.BlockSpec((1, 1, NSA_KV_DIM), lambda l: (l, 0, 0)),
                  pl.BlockSpec((1, HEAD_DIM, CMP_HID), lambda l: (l, 0, 0)),
                  _const_spec(w2bd.shape), _const_spec((rows, 128)), _const_spec((rows, 128))],
        out_specs=_const_spec((rows, NSA_KV_DIM)),
        out_shape=jax.ShapeDtypeStruct((rows, NSA_KV_DIM), F32),
        scratch_shapes=[pltpu.VMEM((rows, NSA_KVH * CMP_HID), F32),
                        pltpu.VMEM((NSA_KV_DIM, NSA_KVH * CMP_HID), BF16)],
        compiler_params=_cp(("arbitrary",)), name="compress",
    )(*xs, pe_t, w1.astype(BF16), w2bd, c_t, s_t)


NB_PAD = 128
RANK_ROWS = 64


def _cmp_and_select(q, kc, vc, qpos, groups, rows_per_q, nb):
    tq = qpos.shape[0]
    rows = groups * rows_per_q * tq
    s = _nt(q, kc).reshape(groups, rows_per_q, tq, NB_PAD)
    n_idx = lax.broadcasted_iota(jnp.int32, (tq, NB_PAD), 1)
    valid = ((n_idx * CMP_BLK + (CMP_BLK - 1)) <= qpos)[None, None]
    sm = jnp.where(valid, s, NEG_INF)
    m = jnp.max(sm, axis=-1, keepdims=True)
    p = jnp.where(valid, jnp.exp(sm - m), 0.0)
    p = p / jnp.maximum(jnp.sum(p, axis=-1, keepdims=True), 1e-30)
    o_cmp = _dot(p.reshape(rows, NB_PAD).astype(BF16), vc)
    cur = qpos // CMP_BLK
    forced = ((n_idx == 0) | (n_idx == cur) | (n_idx == cur - 1))[None]
    score = jnp.where((n_idx > cur)[None], -1.0, jnp.where(forced, FORCE_SCORE, jnp.sum(p, axis=1)))
    nq = groups * tq
    nqp = -(-nq // 128) * 128
    score = score.reshape(nq, NB_PAD)
    if nqp > nq:
        score = jnp.concatenate([score, jnp.zeros((nqp - nq, NB_PAD), F32)], axis=0)
    sel_t = _top_n_mask(score.T[:RANK_ROWS], nb)
    sel = jnp.concatenate([sel_t, jnp.zeros((NB_PAD - RANK_ROWS, nqp), F32)], axis=0).T[:nq]
    sel = jnp.broadcast_to(sel.reshape(groups, 1, tq, NB_PAD), (groups, rows_per_q, tq, NB_PAD))
    return o_cmp, sel.reshape(rows, NB_PAD)


def _top_n_mask(st, nb):
    groups = RANK_ROWS // 8
    blocks = [st[8 * g:8 * g + 8] for g in range(groups)]
    cnt = [jnp.zeros_like(b) for b in blocks]
    row = lax.broadcasted_iota(jnp.int32, blocks[0].shape, 0)
    for i in range(nb):
        r = st[i:i + 1]
        for g in range(groups):
            gt = jnp.where(r > blocks[g], 1.0, 0.0)
            if 8 * g + 7 <= i:
                inc = gt
            else:
                ge = jnp.where(r >= blocks[g], 1.0, 0.0)
                inc = ge if 8 * g > i else jnp.where(row + 8 * g > i, ge, gt)
            cnt[g] = cnt[g] + inc
    return jnp.concatenate([jnp.where(c < float(SEL_TOPN), 1.0, 0.0) for c in cnt], axis=0)


def _osm(z, st, v, keys_axis, v_keys_minor):
    m, l, acc = st
    m_new = jnp.maximum(m, jnp.max(z, axis=keys_axis, keepdims=True))
    alpha = jnp.exp(m - m_new)
    p = jnp.exp(z - m_new)
    l = alpha * l + jnp.sum(p, axis=keys_axis, keepdims=True)
    pb = p.astype(BF16)
    if keys_axis == 1:
        pv = _nt(pb, v) if v_keys_minor else _dot(pb, v)
    else:
        pv = _dot(v, pb)
    return m_new, l, alpha * acc + pv


def _softmax_tiles(zs, vs):
    m = functools.reduce(jnp.maximum, [jnp.max(z, axis=1, keepdims=True) for z in zs])
    ps = [jnp.exp(z - m) for z in zs]
    l = functools.reduce(lambda a, b: a + b, [jnp.sum(p, axis=1, keepdims=True) for p in ps])
    acc = None
    for p, (v, keys_minor) in zip(ps, vs):
        pv = _nt(p.astype(BF16), v) if keys_minor else _dot(p.astype(BF16), v)
        acc = pv if acc is None else acc + pv
    return acc / jnp.maximum(l, 1e-30)


def _osm_groups_t(q_ts, k_rows, v_t, bias, sts):
    n = len(q_ts)
    z_next = _dot(k_rows, q_ts[0]) + bias
    out = []
    for r in range(n):
        z = z_next
        if r + 1 < n:
            z_next = _dot(k_rows, q_ts[r + 1]) + bias
        out.append(_osm(z, sts[r], v_t, 0, True))
    return tuple(out)


def _osm_init(rows, width, keys_axis=1):
    if keys_axis == 1:
        return (jnp.full((rows, 1), NEG_INF, F32), jnp.zeros((rows, 1), F32), jnp.zeros((rows, width), F32))
    return (jnp.full((1, rows), NEG_INF, F32), jnp.zeros((1, rows), F32), jnp.zeros((width, rows), F32))


def _osm_out(st):
    return st[2] / jnp.maximum(st[1], 1e-30)


def _nsa_prompt_body(nb, q_ref, kc_ref, vc_ref, sk_ref, sv_ref, wk_ref, wv_ref, g_ref, o_ref, sb_ref):
    i = pl.program_id(2)
    tq = q_ref.shape[3]
    tk = tq
    rq = NSA_GQA
    q_ts = [q_ref[0, r] for r in range(rq)]
    qpos = i * tq + lax.broadcasted_iota(jnp.int32, (1, tq), 1)

    n_idx = lax.broadcasted_iota(jnp.int32, (NB_PAD, tq), 0)
    valid = (n_idx * CMP_BLK + (CMP_BLK - 1)) <= qpos
    kc = kc_ref[0, 0]
    vc_t = vc_ref[0, 0]
    o_cmp, score = [], None
    for r in range(rq):
        sm = jnp.where(valid, _dot(kc, q_ts[r]), NEG_INF)
        m = jnp.max(sm, axis=0, keepdims=True)
        p = jnp.where(valid, jnp.exp(sm - m), 0.0)
        p = p / jnp.maximum(jnp.sum(p, axis=0, keepdims=True), 1e-30)
        o_cmp.append(_dot(vc_t, p.astype(BF16)))
        score = p if score is None else score + p
    cur = qpos // CMP_BLK
    forced = (n_idx == 0) | (n_idx == cur) | (n_idx == cur - 1)
    score = jnp.where(n_idx > cur, -1.0, jnp.where(forced, FORCE_SCORE, score))
    sel_t = _top_n_mask(score[:RANK_ROWS], nb)
    sb_ref[...] = (sel_t - 1.0) * 1e30

    key = lax.broadcasted_iota(jnp.int32, (tk, tq), 0)
    qry = lax.broadcasted_iota(jnp.int32, (tk, tq), 1)
    init = tuple(_osm_init(tq, HEAD_DIM, 0) for _ in range(rq))

    def sel_tile(j, sts, diag):
        start = pl.multiple_of(j * tk, tk)
        per = tk // CMP_BLK
        bias = jnp.concatenate([jnp.broadcast_to(sb_ref[pl.ds(j * per + b, 1), :], (CMP_BLK, tq))
                                for b in range(per)], axis=0)
        if diag:
            bias = jnp.where(key <= qry, bias, NEG_INF)
        return _osm_groups_t(q_ts, sk_ref[0, 0, pl.ds(start, tk), :], sv_ref[0, 0, :, pl.ds(start, tk)], bias, sts)

    odd = i % 2
    sts = lax.cond(odd == 1, lambda s: sel_tile(0, s, False), lambda s: s, init)
    sts = lax.fori_loop(
        0, i // 2, lambda n, s: sel_tile(odd + 2 * n + 1, sel_tile(odd + 2 * n, s, False), False), sts)
    o_sel = [_osm_out(st) for st in sel_tile(i, sts, True)]

    def win_tile(back, sts):
        j_raw = i - back
        start = pl.multiple_of(jnp.maximum(j_raw, 0) * tk, tk)
        if back == 0:
            ok = key <= qry
        elif back < WINDOW // tk:
            ok = key >= 0
        else:
            ok = key > qry
        bias = jnp.where(ok & (j_raw >= 0), 0.0, NEG_INF)
        return _osm_groups_t(q_ts, wk_ref[0, 0, pl.ds(start, tk), :], wv_ref[0, 0, :, pl.ds(start, tk)], bias, sts)

    sts = init
    for back in range(WINDOW // tk + 1):
        sts = win_tile(back, sts)
    o_win = [_osm_out(st) for st in sts]

    gates = g_ref[0, 0]
    outs = []
    for r in range(rq):
        gc, gs, gw = (gates[3 * r + n:3 * r + n + 1, :] for n in range(3))
        outs.append(gc * o_cmp[r] + gs * o_sel[r] + gw * o_win[r])
    o_ref[0] = jnp.concatenate(outs, axis=0).T.astype(o_ref.dtype)


def _nsa_prompt(q_t, kc_hm, vc_t, sk_hm, sv_t, wk_hm, wv_t, gates_t, nb):
    b, _, d, t = q_t.shape
    tq = min(NSA_TILE, t)
    assert WINDOW % tq == 0 and nb <= RANK_ROWS and tq % 128 == 0 and tq % CMP_BLK == 0
    per_group = lambda shape: pl.BlockSpec((1, 1) + shape, lambda bi, g, i: (bi, g, 0, 0))
    return pl.pallas_call(
        functools.partial(_nsa_prompt_body, nb),
        grid=(b, NSA_KVH, t // tq),
        in_specs=[pl.BlockSpec((1, NSA_GQA, d, tq), lambda bi, g, i: (bi, g, 0, i)),
                  per_group((NB_PAD, d)), per_group((d, NB_PAD)),
                  per_group((t, d)), per_group((d, t)), per_group((t, d)), per_group((d, t)),
                  pl.BlockSpec((1, 1, GATE_PAD, tq), lambda bi, g, i: (bi, g, 0, i))],
        out_specs=pl.BlockSpec((1, tq, NSA_GQA * d), lambda bi, g, i: (bi, i, g)),
        out_shape=jax.ShapeDtypeStruct((b, t, NSA_HEADS * d), BF16),
        scratch_shapes=[pltpu.VMEM((RANK_ROWS, tq), F32)],
        compiler_params=_cp(("parallel", "parallel", "arbitrary")), name="nsa_prompt",
    )(q_t, kc_hm, vc_t, sk_hm, sv_t, wk_hm, wv_t, gates_t)


def _cmp_gather_body(npages, pt_ref, pe_ref, *refs):
    page_refs = refs[:npages]
    o_ref = refs[npages]
    tok_ref = refs[npages + 1]
    pe2 = pe_ref[...]
    lane_tiles = NSA_KV_DIM // 128
    for p in range(npages):
        tok = page_refs[p][0].reshape(NSA_KV_DIM, PAGE).T + pe2
        for c in range(lane_tiles):
            tok_ref[c, p * PAGE:(p + 1) * PAGE, :] = tok[:, c * 128:(c + 1) * 128]
    nblk = npages * PAGE // CMP_BLK
    mid_ref = refs[npages + 2]
    for lo in range(8):
        for c in range(lane_tiles):
            mid_ref[c, lo] = tok_ref[c, pl.ds(lo, npages * PAGE // 8, stride=8), :]

    for l in range(CMP_BLK):
        for c in range(lane_tiles):
            lanes = l * NSA_KV_DIM + c * 128
            o_ref[0, :, lanes:lanes + 128] = mid_ref[c, l % 8, pl.ds(l // 8, nblk, stride=8), :].astype(BF16)


def _cmp_gather(cache, pe, page_table):
    b, npages = page_table.shape
    nblk = npages * PAGE // CMP_BLK
    width = CMP_BLK * NSA_KV_DIM
    pe2 = jnp.tile(pe, (PAGE // CMP_BLK, NSA_KVH))
    grid_spec = pltpu.PrefetchScalarGridSpec(
        num_scalar_prefetch=1, grid=(b,),
        in_specs=[pl.BlockSpec((PAGE, NSA_KV_DIM), lambda bi, pt: (0, 0))] + _page_specs(npages, NSA_KVH),
        out_specs=pl.BlockSpec((1, nblk, width), lambda bi, pt: (bi, 0, 0)),
        scratch_shapes=[pltpu.VMEM((NSA_KV_DIM // 128, npages * PAGE, 128), F32),
                        pltpu.VMEM((NSA_KV_DIM // 128, 8, npages * PAGE // 8, 128), F32)])
    return pl.pallas_call(
        functools.partial(_cmp_gather_body, npages), grid_spec=grid_spec,
        out_shape=jax.ShapeDtypeStruct((b, nblk, width), BF16),
        compiler_params=_cp(("parallel",)), name="cmp_gather",
    )(page_table, pe2, *([cache] * npages))


def _nsa_sample_body(npages, t_new, past, nb, pt_ref, qbd_ref, kc_ref, vc_ref, g_ref,
                     skn_ref, svn_ref, wkn_ref, wvn_ref, wkc_ref, wvc_ref, *refs):
    sk_refs = refs[:npages]
    sv_refs = refs[npages:2 * npages]
    o_ref = refs[2 * npages]
    qbd = qbd_ref[0]
    rows = qbd.shape[0]
    w = qbd.shape[1]
    rq = rows // t_new
    t_idx = lax.broadcasted_iota(jnp.int32, (t_new, 1), 0)
    qpos = past + t_idx

    o_cmp, sel = _cmp_and_select(qbd, kc_ref[0], vc_ref[0], qpos, NSA_KVH, NSA_GQA, nb)
    sel_bias = (sel - 1.0) * 1e30

    qoff = lax.broadcasted_iota(jnp.int32, (rows, PAGE), 0) % t_new
    col = lax.broadcasted_iota(jnp.int32, (rows, PAGE), 1)
    pad = jnp.zeros((PAGE - t_new, w), F32)
    new_bias = jnp.where(col <= qoff, 0.0, NEG_INF)

    def padded(ref):
        return jnp.concatenate([ref[0], pad], axis=0).astype(BF16)

    half = col < CMP_BLK
    zs, vs = [], []
    for p in range(npages):
        b0 = sel_bias[:, 2 * p:2 * p + 1]
        b1 = sel_bias[:, 2 * p + 1:2 * p + 2]
        zs.append(_dot(qbd, sk_refs[p][0].reshape(w, PAGE).astype(BF16)) + jnp.where(half, b0, b1))
        vs.append((sv_refs[p][0].reshape(w, PAGE).astype(BF16), True))
    last = (past // CMP_BLK)
    zs.append(_nt(qbd, padded(skn_ref)) + new_bias + sel_bias[:, last:last + 1])
    vs.append((padded(svn_ref), False))
    o_sel = _softmax_tiles(zs, vs)

    zs, vs = [_nt(qbd, padded(wkn_ref)) + new_bias], [(padded(wvn_ref), False)]
    wb = wkc_ref.shape[3]
    for c in range(wb // PAGE):
        kpos = (past - wb) + c * PAGE + col
        diff = (past + qoff) - kpos
        bias = jnp.where((diff < WINDOW) & (kpos >= 0), 0.0, NEG_INF)
        chunk = lambda ref: ref[0, :, :, c * PAGE:(c + 1) * PAGE].reshape(w, PAGE).astype(BF16)
        zs.append(_dot(qbd, chunk(wkc_ref)) + bias)
        vs.append((chunk(wvc_ref), True))
    o_win = _softmax_tiles(zs, vs)

    gates = g_ref[0]
    o_ref[0] = gates[:, 0:1] * o_cmp + gates[:, 1:2] * o_sel + gates[:, 2:3] * o_win


def _nsa_sample(q, kc, vc, gates, sk_new, sv_new, wk_new, wv_new, win_k, win_v,
                cache_sk, cache_sv, page_table, past, nb):
    b, t, w = sk_new.shape
    npages = page_table.shape[1]
    assert nb <= RANK_ROWS and (past // CMP_BLK) < nb
    qbd = _block_diag_rows(q.reshape(b, t, NSA_HEADS, HEAD_DIM), NSA_KVH)
    rows = NSA_HEADS * t
    g_rows = jnp.transpose(gates.reshape(b, t, NSA_KVH, NSA_GQA, 3), (0, 2, 3, 1, 4)).reshape(b, rows, 3)
    bspec = lambda r, c: pl.BlockSpec((1, r, c), lambda bi, pt: (bi, 0, 0))
    wb = win_k.shape[3]
    win_spec = pl.BlockSpec((1, NSA_KVH, HEAD_DIM, wb), lambda bi, pt: (bi, 0, 0, 0))
    grid_spec = pltpu.PrefetchScalarGridSpec(
        num_scalar_prefetch=1, grid=(b,),
        in_specs=[bspec(rows, w), bspec(NB_PAD, w), bspec(NB_PAD, w), bspec(rows, 3),
                  bspec(t, w), bspec(t, w), bspec(t, w), bspec(t, w), win_spec, win_spec]
                 + _page_specs(npages, NSA_KVH) + _page_specs(npages, NSA_KVH),
        out_specs=bspec(rows, w))
    o = pl.pallas_call(
        functools.partial(_nsa_sample_body, npages, t, past, nb), grid_spec=grid_spec,
        out_shape=jax.ShapeDtypeStruct((b, rows, w), F32),
        compiler_params=_cp(("parallel",)), name="nsa_sample",
    )(page_table, qbd, kc, vc, g_rows, sk_new, sv_new, wk_new, wv_new, win_k, win_v,
      *([cache_sk] * npages), *([cache_sv] * npages))
    return _diag_rows_out(o, t, NSA_KVH)


def _common_tail(h2, batch, l, mem_k, mem_v, p, final_g):
    t = h2.shape[0] // batch
    if mem_k.ndim == 5:
        q = _norm_proj(h2, p["norm_mem"][l][None], p["w_mem_q"][l], MEM_HD ** -0.5).reshape(batch, t, D_MODEL)
        o = _mem_attn_cached(q, mem_k, mem_v, l)
        h2 = _proj_res(h2, [o.reshape(batch * t, D_MODEL)], p["w_mem_o"][l])
    else:
        h2 = _mem_block(h2, p["norm_mem"][l][None], p["w_mem_q"][l], mem_k, mem_v, p["w_mem_o"][l], batch)
    return _ffn(h2, p["norm_ffn"][l][None], p["w_ffn_gate"][l], p["w_ffn_up"][l], p["w_ffn_down"][l], final_g)


def _even_layer(h2, batch, p, conv_buf, sample_ctx):
    t = h2.shape[0] // batch
    prompt = sample_ctx is None
    res = _even_in(h2, p["norm_mix"][0][None], p["w_in_even"][0], batch, prompt)
    glu3 = res[0].reshape(batch, t, CONV_CH)
    y_conv = _conformer_conv(glu3, conv_buf, p["conv_w"][0], p["conv_b"][0], p["conv_ln_g"][0], p["conv_ln_b"][0])
    new_buf = jnp.concatenate([conv_buf.astype(F32), glu3], axis=1)[:, -(CONV_W - 1):]
    shp = (batch, t, SB_HEADS, HEAD_DIM)
    if prompt:
        o = _sb_prompt(res[3], res[4], res[5])
        k_out, v_out = _from_transposed(res[1]), _from_transposed(res[2])
    else:
        cache_k, cache_v, page_table = sample_ctx
        k32, v32 = res[1], res[2]
        o = _sb_sample(res[3].reshape(batch, t, SB_DIM), k32.reshape(batch, t, SB_DIM),
                       v32.reshape(batch, t, SB_DIM), cache_k, cache_v, page_table)
        k_out, v_out = k32.reshape(shp), v32.reshape(shp)
    h2 = _proj_res(h2, [y_conv.reshape(batch * t, CONV_CH), o.reshape(batch * t, SB_DIM)], p["w_mix_out"][0])
    return h2, new_buf, k_out, v_out


def _odd_layer(h2, batch, p, sample_ctx):
    t = h2.shape[0] // batch
    prompt = sample_ctx is None
    past = 0 if prompt else sample_ctx["past"]
    tm = _row_tile(h2.shape[0])
    pos = past + jnp.arange(max(t, tm), dtype=jnp.int32) % t
    cos_t, sin_t = _rope_tables(pos, 128)
    res = _odd_in(h2, p["norm_mix"][1][None], p["w_in_odd"], cos_t, sin_t, batch, prompt)
    ck, cv = res[0], res[1]
    nb = -(-(past + t) // CMP_BLK)
    cw = (p["cmp_pe_k"][0], p["cmp_w1_k"][0], p["cmp_w2_k"][0]), (p["cmp_pe_v"][0], p["cmp_w1_v"][0], p["cmp_w2_v"][0])
    width = CMP_BLK * NSA_KV_DIM

    def end_tables(first, count):
        blk_end = (first + jnp.arange(count, dtype=jnp.int32)) * CMP_BLK + (CMP_BLK - 1)
        return tuple(jnp.tile(x, (batch, 1)) for x in _rope_tables(blk_end, 128))

    if prompt:
        kc = _compress([ck.reshape(batch * nb, width)], *cw[0], end_tables(0, nb), [True])
        vc = _compress([cv.reshape(batch * nb, width)], *cw[1], None, [True])
    else:
        nbp = past // CMP_BLK
        assert past % CMP_BLK == 0 and nb == nbp + 1
        pt = sample_ctx["page_table"]
        new_block = lambda x: jnp.pad(x.reshape(batch, t, NSA_KV_DIM), ((0, 0), (0, CMP_BLK - t), (0, 0))).reshape(batch, width)
        k_tabs = tuple(jnp.concatenate([a, b], axis=0) for a, b in zip(end_tables(0, nbp), end_tables(nbp, 1)))
        halves = []
        for cache, new, w, tabs in ((sample_ctx["cmp_k"], ck, cw[0], k_tabs), (sample_ctx["cmp_v"], cv, cw[1], None)):
            x_past = _cmp_gather(cache, w[0], pt).reshape(batch * nbp, width)
            c = _compress([x_past, new_block(new)], *w, tabs, [False, True])
            halves.append(jnp.concatenate([c[:batch * nbp].reshape(batch, nbp, NSA_KV_DIM),
                                           c[batch * nbp:].reshape(batch, 1, NSA_KV_DIM)], axis=1))
        kc, vc = halves
    kc = kc.reshape(batch, nb, NSA_KV_DIM)
    vc = vc.reshape(batch, nb, NSA_KV_DIM)
    padc = lambda x: jnp.pad(x, ((0, 0), (0, NB_PAD - nb), (0, 0))).astype(BF16)
    kvshape = (batch, t, NSA_KVH, HEAD_DIM)
    if prompt:
        hm = lambda x: jnp.transpose(padc(x).reshape(batch, NB_PAD, NSA_KVH, HEAD_DIM), (0, 2, 1, 3))
        t32 = res[2:8]
        q_t, gates_t, sk_hm, sv_t, wk_hm, wv_t = res[8:]
        o = _nsa_prompt(q_t, hm(kc), jnp.swapaxes(hm(vc), 2, 3), sk_hm, sv_t, wk_hm, wv_t, gates_t, nb)
        keep = min(WINDOW, t)
        outs = [_from_transposed(x) for x in t32[:4]] + [_from_transposed(x[..., -keep:]) for x in t32[4:]]
    else:
        sk, sv, wk, wv = res[2:6]
        q, gates = res[6:]
        gates = gates.reshape(batch, t, NSA_KVH, GATE_PAD)[..., :3 * NSA_GQA]
        r3 = lambda x: x.reshape(batch, t, NSA_KV_DIM)
        wkc, wvc = sample_ctx["win_k"], sample_ctx["win_v"]
        wb = wkc.shape[1]
        o = _nsa_sample(q.reshape(batch, t, ODD_Q), padc(kc), padc(vc), gates,
                        r3(sk), r3(sv), r3(wk), r3(wv), _keys_minor(wkc), _keys_minor(wvc),
                        sample_ctx["sel_k"], sample_ctx["sel_v"], sample_ctx["page_table"], past, nb)
        r4 = lambda x: x.reshape(kvshape)
        outs = [r4(ck), r4(cv), r4(sk), r4(sv),
                jnp.concatenate([wkc, r4(wk)], axis=1)[:, -wb:], jnp.concatenate([wvc, r4(wv)], axis=1)[:, -wb:]]
    h2 = _proj_res(h2, [o.reshape(batch * t, ODD_Q)], p["w_mix_out"][1])
    return (h2,) + tuple(outs)


def _trunk(x, p, conv_buf, mem_kv, even_ctx, odd_ctx):
    batch, t, _ = x.shape
    h2 = x.reshape(batch * t, D_MODEL)
    h2, new_buf, sbk, sbv = _even_layer(h2, batch, p, conv_buf, even_ctx)
    h2 = _common_tail(h2, batch, 0, mem_kv[0][0], mem_kv[0][1], p, None)
    h2, ck, cv, sk, sv, wk, wv = _odd_layer(h2, batch, p, odd_ctx)
    h2 = _common_tail(h2, batch, 1, mem_kv[1][0], mem_kv[1][1], p, p["final_norm"][None])
    st = lambda a: a[None]
    return (h2.reshape(batch, t, D_MODEL), st(sbk), st(sbv), st(new_buf),
            st(ck), st(cv), st(sk), st(sv), st(wk), st(wv))


def kernel(x_prompt, x_sample, mem_prompt, cache_sb_k, cache_sb_v, state_conv,
           cache_nsa_cmp_k, cache_nsa_cmp_v, cache_nsa_sel_k, cache_nsa_sel_v,
           cache_nsa_win_k, cache_nsa_win_v, cache_mem_k, cache_mem_v, page_table,
           norm_mix, norm_mem, norm_ffn, final_norm, w_in_even, w_in_odd, w_mix_out,
           conv_w, conv_b, conv_ln_g, conv_ln_b,
           cmp_pe_k, cmp_w1_k, cmp_w2_k, cmp_pe_v, cmp_w1_v, cmp_w2_v,
           w_mem_q, w_mem_k, w_mem_v, w_mem_o, w_ffn_gate, w_ffn_up, w_ffn_down):
    assert norm_mix.shape[0] == 2 and w_in_even.shape[0] == 1 and w_in_odd.shape[0] == 1
    bp = x_prompt.shape[0]
    bs = x_sample.shape[0]
    bf = lambda w: w.astype(BF16)
    p = dict(
        norm_mix=norm_mix, norm_mem=norm_mem, norm_ffn=norm_ffn, final_norm=final_norm,
        w_in_even=bf(w_in_even),
        w_in_odd=_odd_in_weights(w_in_odd[0]),
        w_mix_out=bf(w_mix_out), conv_w=conv_w, conv_b=conv_b, conv_ln_g=conv_ln_g, conv_ln_b=conv_ln_b,
        cmp_pe_k=cmp_pe_k, cmp_w1_k=cmp_w1_k, cmp_w2_k=cmp_w2_k,
        cmp_pe_v=cmp_pe_v, cmp_w1_v=cmp_w1_v, cmp_w2_v=cmp_w2_v,
        w_mem_q=bf(w_mem_q), w_mem_o=bf(w_mem_o),
        w_ffn_gate=bf(w_ffn_gate), w_ffn_up=bf(w_ffn_up), w_ffn_down=bf(w_ffn_down))

    ml = mem_prompt.shape[1]
    mem = _mem_kv(mem_prompt.reshape(bp * ml, D_MODEL),
                  [bf(w_mem_k[0]), bf(w_mem_v[0]), bf(w_mem_k[1]), bf(w_mem_v[1])])
    m3 = lambda a: a.reshape(bp, ml, D_MODEL)
    m4 = lambda a: a.reshape(bp, ml, MEM_HEADS, MEM_HD)
    mem_kv_p = [(m3(mem[1]), m3(mem[3])), (m3(mem[5]), m3(mem[7]))]
    mem_k_p = jnp.stack([m4(mem[0]), m4(mem[4])])
    mem_v_p = jnp.stack([m4(mem[2]), m4(mem[6])])
    zero_buf = jnp.zeros((bp, CONV_W - 1, CONV_CH), F32)
    outs_p = _trunk(x_prompt, p, zero_buf, mem_kv_p, None, None)

    past = page_table.shape[1] * PAGE
    even_ctx = (_keys_minor(cache_sb_k[0]), _keys_minor(cache_sb_v[0]), page_table)
    odd_ctx = dict(past=past, page_table=page_table,
                   cmp_k=_keys_minor(cache_nsa_cmp_k[0]), cmp_v=_keys_minor(cache_nsa_cmp_v[0]),
                   sel_k=_keys_minor(cache_nsa_sel_k[0]), sel_v=_keys_minor(cache_nsa_sel_v[0]),
                   win_k=cache_nsa_win_k[0], win_v=cache_nsa_win_v[0])
    mem_kv_s = [(cache_mem_k, cache_mem_v)] * cache_mem_k.shape[0]
    outs_s = _trunk(x_sample, p, state_conv[0], mem_kv_s, even_ctx, odd_ctx)

    return (outs_p[0], outs_s[0]) + tuple(outs_p[1:]) + (mem_k_p, mem_v_p) + tuple(outs_s[1:])
```
